```python
import math
import jax, jax.numpy as jnp
from jax import lax
import numpy as np

D_MODEL = 1024
BATCH = 8
SEQ = 8192
DEPTH = 1
DEC_BATCH = 8
DEC_SEQ = 32
PAST_LEN = 2048

CHUNK = 64
N_META = 16
Q_BLOCK = 128
ROPE_THETA = 500000.0
RMS_EPS = 1e-6
N_HEADS_A = 4
HEAD_DIM_A = 64
N_HEADS_B = 8
N_KV_B = 2
HEAD_DIM_B = 64
N_IDX_HEADS = 8
IDX_DIM = 32
TOPK_MAX = 256
N_EXPERTS = 32
TOP_K_EXPERTS = 4
D_EXPERT = 1024
SWIGLU_LIMIT = 7.0
SWIGLU_ALPHA = 1.702
MOE_BLOCK = 256

QA_W = N_HEADS_A * 2 * HEAD_DIM_A
VA_W = N_HEADS_A * 2 * HEAD_DIM_A
QB_W = N_HEADS_B * HEAD_DIM_B
KB_W = N_KV_B * HEAD_DIM_B
QI_W = N_IDX_HEADS * IDX_DIM
SPLITS = (QA_W, QA_W, VA_W, QB_W, KB_W, KB_W, QI_W, IDX_DIM, N_IDX_HEADS, D_MODEL, D_MODEL)
D_IN = QA_W + QA_W + VA_W + QB_W + KB_W + KB_W + QI_W + IDX_DIM + N_IDX_HEADS + D_MODEL + D_MODEL

kernel_name = "hybrid_diffattn_dsa_moe_stream_step"


def rmsnorm(x, g):
    xf = x.astype(jnp.float32)
    xf = xf * lax.rsqrt(jnp.mean(xf * xf, axis=-1, keepdims=True) + RMS_EPS)
    return (xf * g.astype(jnp.float32)).astype(x.dtype)


def rope(x, pos):
    d = x.shape[-1]
    rd = d // 4
    half = rd // 2
    inv_freq = ROPE_THETA ** (-jnp.arange(half, dtype=jnp.float32) * 2.0 / rd)
    ang = pos.astype(jnp.float32)[:, None] * inv_freq[None, :]
    shape = (pos.shape[0],) + (1,) * (x.ndim - 3) + (half,)
    cos = jnp.cos(ang).reshape(shape).astype(x.dtype)
    sin = jnp.sin(ang).reshape(shape).astype(x.dtype)
    x1 = x[..., :half]
    x2 = x[..., half:rd]
    return jnp.concatenate([x1 * cos - x2 * sin, x2 * cos + x1 * sin, x[..., rd:]], axis=-1)


def project(h, w_in, pos):
    B, T, _ = h.shape
    z = h @ w_in
    offsets = [int(o) for o in np.cumsum(SPLITS)[:-1]]
    qa, ka, va, qb, kb, vb, qi, ki, wi, ga, gb = jnp.split(z, offsets, axis=-1)
    qa = rope(qa.reshape(B, T, N_HEADS_A, 2, HEAD_DIM_A), pos)
    ka = rope(ka.reshape(B, T, N_HEADS_A, 2, HEAD_DIM_A), pos).reshape(B, T, N_HEADS_A, 2 * HEAD_DIM_A)
    va = va.reshape(B, T, N_HEADS_A, 2 * HEAD_DIM_A)
    qb = rope(qb.reshape(B, T, N_HEADS_B, HEAD_DIM_B), pos)
    kb = rope(kb.reshape(B, T, N_KV_B, HEAD_DIM_B), pos)
    vb = vb.reshape(B, T, N_KV_B, HEAD_DIM_B)
    qi = rope(qi.reshape(B, T, N_IDX_HEADS, IDX_DIM), pos)
    ki = rope(ki, pos)
    wi = wi * (N_IDX_HEADS ** -0.5)
    return qa, ka, va, qb, kb, vb, qi, ki, wi, ga, gb


def diff_attend(qa, q_chunk, ka, va, k_chunk, lam, subln_w, lam_init):
    B, Q = qa.shape[:2]
    L = ka.shape[1]
    k = ka.reshape(B, L, N_HEADS_A, 2, HEAD_DIM_A)
    s = jnp.einsum('bqhcd,bkhcd->bhcqk', qa, k).astype(jnp.float32) * (HEAD_DIM_A ** -0.5)
    visible = k_chunk[None, :] <= q_chunk[:, None]
    p = jax.nn.softmax(jnp.where(visible, s, -jnp.inf), axis=-1)
    a = (p[:, :, 0] - lam * p[:, :, 1]).astype(va.dtype)
    o = jnp.einsum('bhqk,bkhe->bqhe', a, va)
    o = rmsnorm(o, subln_w) * (1.0 - lam_init)
    return o.reshape(B, Q, VA_W)


def dsa_attend(qb, qi, wi, q_chunk, kb, vb, ki, k_chunk, topk):
    B, Q = qb.shape[:2]
    logits = jnp.einsum('bqhd,bkd->bqhk', qi, ki).astype(jnp.float32) * (IDX_DIM ** -0.5)
    score = jnp.einsum('bqh,bqhk->bqk', wi.astype(jnp.float32), jax.nn.relu(logits))
    visible = k_chunk[None, :] <= q_chunk[:, None]
    score = jnp.where(visible[None], score, -jnp.inf)
    _, sel = lax.top_k(score, topk)
    ok = k_chunk[sel] <= q_chunk[None, :, None]
    gather = jax.vmap(lambda rows, idx: rows[idx])
    k_sel = gather(kb, sel)
    v_sel = gather(vb, sel)
    q = qb.reshape(B, Q, N_KV_B, N_HEADS_B // N_KV_B, HEAD_DIM_B)
    s = jnp.einsum('bqngd,bqknd->bqngk', q, k_sel).astype(jnp.float32) * (HEAD_DIM_B ** -0.5)
    s = jnp.where(ok[:, :, None, None, :], s, -jnp.inf)
    p = jax.nn.softmax(s, axis=-1).astype(vb.dtype)
    o = jnp.einsum('bqngk,bqknd->bqngd', p, v_sel)
    return o.reshape(B, Q, QB_W)


def sweep_query_blocks(fn, q_args, q_chunk):
    B, T = q_args[0].shape[:2]
    nb = -(-T // Q_BLOCK)
    pad = nb * Q_BLOCK - T

    def to_blocks(a):
        a = jnp.pad(a, [(0, 0), (0, pad)] + [(0, 0)] * (a.ndim - 2))
        return jnp.moveaxis(a.reshape((B, nb, Q_BLOCK) + a.shape[2:]), 1, 0)

    qc_blocks = jnp.pad(q_chunk, (0, pad), mode='edge').reshape(nb, Q_BLOCK)
    out = lax.map(lambda xs: fn(*xs[0], xs[1]), (tuple(to_blocks(a) for a in q_args), qc_blocks))
    out = jnp.moveaxis(out, 0, 1).reshape((B, nb * Q_BLOCK) + out.shape[3:])
    return out[:, :T]


def moe(h, w_router, b_router, w_gu, b_gu, w_dn, b_dn):
    shape = h.shape
    xt = h.reshape(-1, shape[-1])
    T = xt.shape[0]
    logits = (xt @ w_router).astype(jnp.float32) + b_router.astype(jnp.float32)
    top_val, top_idx = lax.top_k(logits, TOP_K_EXPERTS)
    gate = jax.nn.softmax(top_val, axis=-1)
    A = T * TOP_K_EXPERTS
    e_flat = top_idx.reshape(-1)
    order = jnp.argsort(e_flat)
    e_sorted = e_flat[order]
    tok_sorted = (order // TOP_K_EXPERTS).astype(jnp.int32)
    counts = jnp.zeros((N_EXPERTS,), jnp.int32).at[e_flat].add(1)
    blocks_per = (counts + MOE_BLOCK - 1) // MOE_BLOCK
    padded = blocks_per * MOE_BLOCK
    pstart = jnp.cumsum(padded) - padded
    start = jnp.cumsum(counts) - counts
    dest = pstart[e_sorted] + jnp.arange(A, dtype=jnp.int32) - start[e_sorted]
    NB = -(-A // MOE_BLOCK) + N_EXPERTS
    slot_tok = jnp.full((NB * MOE_BLOCK,), T, jnp.int32).at[dest].set(tok_sorted)
    block_expert = jnp.minimum(
        jnp.searchsorted(jnp.cumsum(blocks_per), jnp.arange(NB), side='right'), N_EXPERTS - 1)
    x_pad = jnp.concatenate([xt, jnp.zeros((1, xt.shape[1]), xt.dtype)], axis=0)

    def expert_block(args):
        toks, e = args
        gu = x_pad[toks] @ w_gu[e] + b_gu[e]
        glu = jnp.minimum(gu[:, :D_EXPERT], SWIGLU_LIMIT)
        lin = jnp.clip(gu[:, D_EXPERT:], -SWIGLU_LIMIT, SWIGLU_LIMIT)
        act = glu * jax.nn.sigmoid(SWIGLU_ALPHA * glu) * (lin + 1.0)
        return act @ w_dn[e] + b_dn[e]

    y_slots = lax.map(expert_block, (slot_tok.reshape(NB, MOE_BLOCK), block_expert))
    y_sorted = y_slots.reshape(NB * MOE_BLOCK, -1)[dest]
    g_sorted = gate.reshape(-1)[order].astype(y_sorted.dtype)
    out = jax.ops.segment_sum(y_sorted * g_sorted[:, None], tok_sorted, num_segments=T)
    return out.reshape(shape)


def layer(x, pos, q_chunk, k_chunk, past, topk, lam_init,
          g_mix, w_in, lambda_q1, lambda_k1, lambda_q2, lambda_k2, subln_w,
          w_pa, w_pb, w_o, g_ffn, w_router, b_router, w_gu, b_gu, w_dn, b_dn):
    h = rmsnorm(x, g_mix)
    qa, ka, va, qb, kb, vb, qi, ki, wi, ga, gb = project(h, w_in, pos)
    new_rows = (ka, va, kb, vb, ki)
    if past is not None:
        ka, va, kb, vb, ki = (jnp.concatenate([c.astype(n.dtype), n], axis=1) for c, n in zip(past, new_rows))
    lam = (jnp.exp(jnp.sum(lambda_q1.astype(jnp.float32) * lambda_k1.astype(jnp.float32)))
           - jnp.exp(jnp.sum(lambda_q2.astype(jnp.float32) * lambda_k2.astype(jnp.float32)))
           + lam_init)

    def attn_a(q, c):
        return diff_attend(q, c, ka, va, k_chunk, lam, subln_w, lam_init)

    def attn_b(q, i, w, c):
        return dsa_attend(q, i, w, c, kb, vb, ki, k_chunk, topk)

    o_a = sweep_query_blocks(attn_a, (qa,), q_chunk)
    o_b = sweep_query_blocks(attn_b, (qb, qi, wi), q_chunk)
    merged = jax.nn.sigmoid(ga) * (o_a @ w_pa) + jax.nn.sigmoid(gb) * (o_b @ w_pb)
    x = x + merged @ w_o
    x = x + moe(rmsnorm(x, g_ffn), w_router, b_router, w_gu, b_gu, w_dn, b_dn)
    return x, new_rows


def setup_inputs(seed: int = 0) -> dict:
    key = jax.random.key(seed)
    ks = jax.random.split(key, 32)

    def nrm(k, shape, scale):
        return jax.random.normal(k, shape, jnp.float32) * scale

    return {
        "x_prompt": nrm(ks[0], (BATCH, SEQ, D_MODEL), 1.0),
        "x_sample": nrm(ks[1], (DEC_BATCH, DEC_SEQ, D_MODEL), 1.0),
        "cache_a_k": nrm(ks[2], (DEPTH, DEC_BATCH, PAST_LEN, N_HEADS_A, 2 * HEAD_DIM_A), 1.0),
        "cache_a_v": nrm(ks[3], (DEPTH, DEC_BATCH, PAST_LEN, N_HEADS_A, 2 * HEAD_DIM_A), 1.0),
        "cache_b_k": nrm(ks[4], (DEPTH, DEC_BATCH, PAST_LEN, N_KV_B, HEAD_DIM_B), 1.0),
        "cache_b_v": nrm(ks[5], (DEPTH, DEC_BATCH, PAST_LEN, N_KV_B, HEAD_DIM_B), 1.0),
        "cache_idx_k": nrm(ks[6], (DEPTH, DEC_BATCH, PAST_LEN, IDX_DIM), 1.0),
        "meta_tokens": nrm(ks[7], (N_META, D_MODEL), 1.0),
        "g_mix": 1.0 + nrm(ks[8], (DEPTH, D_MODEL), 0.02),
        "w_in": nrm(ks[9], (DEPTH, D_MODEL, D_IN), D_MODEL ** -0.5),
        "lambda_q1": nrm(ks[10], (DEPTH, HEAD_DIM_A), 0.1),
        "lambda_k1": nrm(ks[11], (DEPTH, HEAD_DIM_A), 0.1),
        "lambda_q2": nrm(ks[12], (DEPTH, HEAD_DIM_A), 0.1),
        "lambda_k2": nrm(ks[13], (DEPTH, HEAD_DIM_A), 0.1),
        "subln_w": 1.0 + nrm(ks[14], (DEPTH, 2 * HEAD_DIM_A), 0.02),
        "w_pa": nrm(ks[15], (DEPTH, VA_W, D_MODEL), VA_W ** -0.5),
        "w_pb": nrm(ks[16], (DEPTH, QB_W, D_MODEL), QB_W ** -0.5),
        "w_o": nrm(ks[17], (DEPTH, D_MODEL, D_MODEL), D_MODEL ** -0.5),
        "g_ffn": 1.0 + nrm(ks[18], (DEPTH, D_MODEL), 0.02),
        "w_router": nrm(ks[19], (DEPTH, D_MODEL, N_EXPERTS), D_MODEL ** -0.5),
        "b_router": nrm(ks[20], (DEPTH, N_EXPERTS), 0.01),
        "w_gu": nrm(ks[21], (DEPTH, N_EXPERTS, D_MODEL, 2 * D_EXPERT), D_MODEL ** -0.5),
        "b_gu": nrm(ks[22], (DEPTH, N_EXPERTS, 2 * D_EXPERT), 0.02),
        "w_dn": nrm(ks[23], (DEPTH, N_EXPERTS, D_EXPERT, D_MODEL), D_EXPERT ** -0.5),
        "b_dn": nrm(ks[24], (DEPTH, N_EXPERTS, D_MODEL), 0.02),
        "g_final": 1.0 + nrm(ks[25], (D_MODEL,), 0.02),
    }


def reference(x_prompt, x_sample, cache_a_k, cache_a_v, cache_b_k, cache_b_v, cache_idx_k,
              meta_tokens, g_mix, w_in, lambda_q1, lambda_k1, lambda_q2, lambda_k2, subln_w,
              w_pa, w_pb, w_o, g_ffn, w_router, b_router, w_gu, b_gu, w_dn, b_dn, g_final):
    B, S = x_prompt.shape[:2]
    S_dec = x_sample.shape[1]
    P = cache_a_k.shape[2]
    meta = jnp.broadcast_to(meta_tokens[None].astype(x_prompt.dtype), (B, N_META, D_MODEL))
    xp = jnp.concatenate([meta, x_prompt], axis=1)
    pos_p = jnp.arange(N_META + S, dtype=jnp.int32)
    chunk_p = (pos_p - N_META) // CHUNK
    topk_p = min(TOPK_MAX, S // 4)
    xs = x_sample
    pos_s = P + jnp.arange(S_dec, dtype=jnp.int32)
    chunk_s_q = pos_s // CHUNK
    chunk_s_k = jnp.arange(P + S_dec, dtype=jnp.int32) // CHUNK
    topk_s = min(TOPK_MAX, (P + S_dec) // 4)

    rows_p = []
    rows_s = []
    for l in range(DEPTH):
        lam_init = 0.8 - 0.6 * math.exp(-0.3 * l)
        lw = (g_mix[l], w_in[l], lambda_q1[l], lambda_k1[l], lambda_q2[l], lambda_k2[l], subln_w[l],
              w_pa[l], w_pb[l], w_o[l], g_ffn[l], w_router[l], b_router[l], w_gu[l], b_gu[l],
              w_dn[l], b_dn[l])
        xp, rp = layer(xp, pos_p, chunk_p, chunk_p, None, topk_p, lam_init, *lw)
        past = (cache_a_k[l], cache_a_v[l], cache_b_k[l], cache_b_v[l], cache_idx_k[l])
        xs, rs = layer(xs, pos_s, chunk_s_q, chunk_s_k, past, topk_s, lam_init, *lw)
        rows_p.append(rp)
        rows_s.append(rs)

    y_prompt = rmsnorm(xp[:, N_META:], g_final)
    y_sample = rmsnorm(xs, g_final)
    new_a_k_prompt = jnp.stack([r[0] for r in rows_p])
    new_a_v_prompt = jnp.stack([r[1] for r in rows_p])
    new_b_k_prompt = jnp.stack([r[2] for r in rows_p])
    new_b_v_prompt = jnp.stack([r[3] for r in rows_p])
    new_idx_k_prompt = jnp.stack([r[4] for r in rows_p])
    new_a_k_sample = jnp.stack([r[0] for r in rows_s])
    new_a_v_sample = jnp.stack([r[1] for r in rows_s])
    new_b_k_sample = jnp.stack([r[2] for r in rows_s])
    new_b_v_sample = jnp.stack([r[3] for r in rows_s])
    new_idx_k_sample = jnp.stack([r[4] for r in rows_s])
    return (y_prompt, y_sample,
            new_a_k_prompt, new_a_v_prompt, new_b_k_prompt, new_b_v_prompt, new_idx_k_prompt,
            new_a_k_sample, new_a_v_sample, new_b_k_sample, new_b_v_sample, new_idx_k_sample)
```

```python
import functools
import math

import numpy as np
import jax
import jax.numpy as jnp
from jax import lax
from jax.experimental import pallas as pl
from jax.experimental.pallas import tpu as pltpu

D_MODEL = 1024
CHUNK = 64
N_META = 16
ROPE_THETA = 500000.0
RMS_EPS = 1e-6
N_HEADS_A = 4
HEAD_DIM_A = 64
N_HEADS_B = 8
N_KV_B = 2
HEAD_DIM_B = 64
N_IDX_HEADS = 8
IDX_DIM = 32
TOPK_MAX = 256
N_EXPERTS = 32
TOP_K_EXPERTS = 4
D_EXPERT = 1024
SWIGLU_LIMIT = 7.0
SWIGLU_ALPHA = 1.702
MOE_BLOCK = 256

QA_W = N_HEADS_A * 2 * HEAD_DIM_A
VA_W = QA_W
QB_W = N_HEADS_B * HEAD_DIM_B
KB_W = N_KV_B * HEAD_DIM_B
QI_W = N_IDX_HEADS * IDX_DIM
SPLITS = (QA_W, QA_W, VA_W, QB_W, KB_W, KB_W, QI_W, IDX_DIM, N_IDX_HEADS, D_MODEL, D_MODEL)

LANES = 128
TAIL = LANES
QB_EXP_W = N_HEADS_B * LANES

_SEC = {}
_off = 0
for _name, _w in (("qa", QA_W), ("ka", QA_W), ("va", VA_W), ("qb", QB_EXP_W), ("kb", KB_W),
                  ("vb", KB_W), ("qi", QI_W), ("kiwi", LANES), ("kir", QI_W),
                  ("ga", D_MODEL), ("gb", D_MODEL)):
    _SEC[_name] = (_off, _off + _w)
    _off += _w
W_LAYOUT = _off

_MXU = jnp.bfloat16
_F32 = jnp.float32
_NEG = -1e30
_BIG_CHUNK = 2 ** 30
_INT_MIN = -2 ** 31
_KEY_NEG_INF = int(np.int32(np.uint32(0xFF800000) ^ np.uint32(0x7FFFFFFF)))
_VMEM_LIMIT = 56 * 1024 * 1024


def _cparams(sem):
    return pltpu.CompilerParams(dimension_semantics=sem, vmem_limit_bytes=_VMEM_LIMIT)


def _const_spec(shape):
    nd = len(shape)
    return pl.BlockSpec(shape, lambda *_: (0,) * nd)


def _nt_dot(a, b):
    return lax.dot_general(a, b, (((1,), (1,)), ((), ())), preferred_element_type=_F32)


def _sortable(x):
    b = lax.bitcast_convert_type(x, jnp.int32)
    return b ^ ((b >> 31) & 0x7FFFFFFF)


def _rope_block(z, cos, sin_lo, sin_hi, half):
    return (z * cos + pltpu.roll(z, LANES - half, 1) * sin_lo + pltpu.roll(z, half, 1) * sin_hi)


def _proj_kernel(x_ref, g_ref, w_ref, tab_ref,
                 qa_o, kaf_o, kab_o, vaf_o, vab_o, qb_o, kbf_o, kbb_o, vbf_o, vbb_o,
                 qi_o, kiwi_o, kir_o, ga_o, gb_o):
    x = x_ref[...]
    ms = jnp.mean(x * x, axis=-1, keepdims=True)
    h = (x * lax.rsqrt(ms + RMS_EPS) * g_ref[...]).astype(w_ref.dtype)

    def sec(name):
        a, b = _SEC[name]
        return jnp.dot(h, w_ref[:, a:b], preferred_element_type=_F32)

    t64 = tuple(tab_ref[:, k * LANES:(k + 1) * LANES] for k in (0, 1, 2))
    t32 = tuple(tab_ref[:, k * LANES:(k + 1) * LANES] for k in (3, 4, 5))
    tki = tuple(tab_ref[:, k * LANES:(k + 1) * LANES] for k in (6, 7, 8))

    def roped(z, tab, half):
        return [_rope_block(z[:, c * LANES:(c + 1) * LANES], *tab, half)
                for c in range(z.shape[1] // LANES)]

    scale_a = HEAD_DIM_A ** -0.5
    scale_b = HEAD_DIM_B ** -0.5
    for c, blk in enumerate(roped(sec("qa"), t64, 8)):
        qa_o[:, c * LANES:(c + 1) * LANES] = (blk * scale_a).astype(qa_o.dtype)
    for c, blk in enumerate(roped(sec("ka"), t64, 8)):
        kaf_o[:, c * LANES:(c + 1) * LANES] = blk
        kab_o[:, c * LANES:(c + 1) * LANES] = blk.astype(kab_o.dtype)
    va = sec("va")
    vaf_o[...] = va
    vab_o[...] = va.astype(vab_o.dtype)
    for c, blk in enumerate(roped(sec("qb"), t64, 8)):
        qb_o[:, c * LANES:(c + 1) * LANES] = (blk * scale_b).astype(qb_o.dtype)
    kb = roped(sec("kb"), t64, 8)[0]
    kbf_o[...] = kb
    kbb_o[...] = kb.astype(kbb_o.dtype)
    vb = sec("vb")
    vbf_o[...] = vb
    vbb_o[...] = vb.astype(vbb_o.dtype)
    for c, blk in enumerate(roped(sec("qi"), t32, 4)):
        qi_o[:, c * LANES:(c + 1) * LANES] = blk.astype(qi_o.dtype)
    kiwi_o[...] = roped(sec("kiwi"), tki, 4)[0]
    for c, blk in enumerate(roped(sec("kir"), t32, 4)):
        kir_o[:, c * LANES:(c + 1) * LANES] = blk.astype(kir_o.dtype)
    ga_o[...] = jax.nn.sigmoid(sec("ga"))
    gb_o[...] = jax.nn.sigmoid(sec("gb"))


def _rope_tables(pos):
    def cs(d):
        rd = d // 4
        half = rd // 2
        inv_freq = ROPE_THETA ** (-jnp.arange(half, dtype=_F32) * 2.0 / rd)
        ang = pos.astype(_F32)[:, None] * inv_freq[None, :]
        return jnp.cos(ang), jnp.sin(ang), half

    T = pos.shape[0]

    def pattern(d, width):
        c, s, half = cs(d)
        one = jnp.ones((T, width - 2 * half), _F32)
        zero = lambda n: jnp.zeros((T, n), _F32)
        cos = jnp.concatenate([c, c, one], axis=1)
        lo = jnp.concatenate([-s, zero(width - half)], axis=1)
        hi = jnp.concatenate([zero(half), s, zero(width - 2 * half)], axis=1)
        return cos, lo, hi

    tabs = []
    for d, width in ((64, 64), (32, 32), (32, LANES)):
        tabs += [jnp.tile(t, (1, LANES // width)) for t in pattern(d, width)]
    return jnp.concatenate(tabs, axis=1)


def _layout_w_in(w_in):
    offs = np.cumsum((0,) + SPLITS)
    qa, ka, va, qb, kb, vb, qi, ki, wi, ga, gb = (w_in[:, offs[i]:offs[i + 1]] for i in range(11))
    D = w_in.shape[0]
    z64 = jnp.zeros((D, HEAD_DIM_B), w_in.dtype)
    per_kv = N_HEADS_B // N_KV_B
    qb_exp = []
    for hq in range(N_HEADS_B):
        col = qb[:, hq * HEAD_DIM_B:(hq + 1) * HEAD_DIM_B]
        qb_exp += [col, z64] if hq // per_kv == 0 else [z64, col]
    kiwi = jnp.concatenate([ki, wi, jnp.zeros((D, LANES - IDX_DIM - N_IDX_HEADS), w_in.dtype)], axis=1)
    kir = jnp.tile(ki, (1, N_IDX_HEADS))
    w = jnp.concatenate([qa, ka, va] + qb_exp + [kb, vb, qi, kiwi, kir, ga, gb], axis=1)
    assert w.shape[1] == W_LAYOUT
    return w.astype(_MXU)


def _layout_w_pb(w_pb):
    D = w_pb.shape[1]
    z64 = jnp.zeros((HEAD_DIM_B, D), w_pb.dtype)
    per_kv = N_HEADS_B // N_KV_B
    rows = []
    for hq in range(N_HEADS_B):
        r = w_pb[hq * HEAD_DIM_B:(hq + 1) * HEAD_DIM_B]
        rows += [r, z64] if hq // per_kv == 0 else [z64, r]
    return jnp.concatenate(rows, axis=0).astype(_MXU)


def _project(x, pos, g_mix, w_lay, tm):
    B, T, D = x.shape
    assert T % tm == 0
    tab = _rope_tables(pos)
    row = lambda w: pl.BlockSpec((None, tm, w), lambda b, i: (b, i, 0))
    outs = [("qa", QA_W, _MXU), ("ka_f", QA_W, _F32), ("ka", QA_W, _MXU), ("va_f", VA_W, _F32),
            ("va", VA_W, _MXU), ("qb", QB_EXP_W, _MXU), ("kb_f", KB_W, _F32), ("kb", KB_W, _MXU),
            ("vb_f", KB_W, _F32), ("vb", KB_W, _MXU), ("qi", QI_W, _MXU), ("kiwi", LANES, _F32),
            ("kir", QI_W, _MXU), ("ga", D_MODEL, _F32), ("gb", D_MODEL, _F32)]
    res = pl.pallas_call(
        _proj_kernel,
        grid=(B, T // tm),
        in_specs=[row(D), _const_spec((1, D)),
                  pl.BlockSpec((D, W_LAYOUT), lambda b, i: (0, 0), pipeline_mode=pl.Buffered(1)),
                  pl.BlockSpec((tm, 9 * LANES), lambda b, i: (i, 0))],
        out_specs=[row(w) for _, w, _ in outs],
        out_shape=[jax.ShapeDtypeStruct((B, T, w), dt) for _, w, dt in outs],
        compiler_params=_cparams(("parallel", "parallel")),
        name="proj",
    )(x, g_mix.reshape(1, D), w_lay, tab)
    return {n: r for (n, _, _), r in zip(outs, res)}


def _tile_lanes(v, n):
    return v if n == 1 else jnp.concatenate([v] * n, axis=1)


def _softmax_step(s, v, m_sc, l_sc, acc_sc):
    m_prev = m_sc[...]
    m_new = jnp.maximum(m_prev, jnp.max(s, axis=1, keepdims=True))
    alpha = jnp.exp(m_prev - m_new)
    p = jnp.exp(s - _tile_lanes(m_new, s.shape[1] // LANES))
    l_sc[...] = alpha * l_sc[...] + jnp.sum(p, axis=1, keepdims=True)
    acc_sc[...] = alpha * acc_sc[...] + jnp.dot(p.astype(v.dtype), v, preferred_element_type=_F32)
    m_sc[...] = m_new


def _num_key_tiles(qchunk, kchunk_main, TQ, TK):
    first = kchunk_main.reshape(-1, TK)[:, 0]
    qmax = qchunk.reshape(-1, TQ).max(axis=1)
    return (first[None, :] <= qmax[:, None]).sum(axis=1).astype(np.int32)


def _diff_attn_kernel(nk_ref, q_ref, k_ref, v_ref, qc_ref, kcm_ref, kct_ref, lam_ref, sub_ref,
                      o_ref, m_sc, l_sc, acc_sc, *, TQ, TK, Lmain, lam_init):
    i = pl.program_id(2)
    q = q_ref[...]
    lane = lax.broadcasted_iota(jnp.int32, q.shape, 1)
    zero = jnp.zeros_like(q)
    qs = jnp.concatenate([jnp.where(lane < HEAD_DIM_A, q, zero),
                          jnp.where(lane >= HEAD_DIM_A, q, zero)], axis=0)
    qc = qc_ref[...]
    qc2 = jnp.concatenate([qc, qc], axis=0)
    m_sc[...] = jnp.full(m_sc.shape, _NEG, _F32)
    l_sc[...] = jnp.zeros(l_sc.shape, _F32)
    acc_sc[...] = jnp.zeros(acc_sc.shape, _F32)

    def step(k, v, kc):
        s = _nt_dot(qs, k)
        s = jnp.where(kc <= qc2, s, _NEG)
        _softmax_step(s, v, m_sc, l_sc, acc_sc)

    def body(j, carry):
        off = pl.multiple_of(j * TK, TK)
        step(k_ref[pl.ds(off, TK), :], v_ref[pl.ds(off, TK), :], kcm_ref[j])
        return carry

    lax.fori_loop(0, nk_ref[i], body, 0)
    step(k_ref[Lmain:Lmain + TAIL, :], v_ref[Lmain:Lmain + TAIL, :], kct_ref[...])

    lam = (jnp.exp(jnp.sum(lam_ref[0:1, :] * lam_ref[1:2, :], axis=1, keepdims=True))
           - jnp.exp(jnp.sum(lam_ref[2:3, :] * lam_ref[3:4, :], axis=1, keepdims=True))
           + lam_init)
    o = acc_sc[...] / l_sc[...]
    d = o[:TQ] - lam * o[TQ:]
    ms = jnp.mean(d * d, axis=-1, keepdims=True)
    o_ref[...] = ((d * lax.rsqrt(ms + RMS_EPS) * sub_ref[...]) * (1.0 - lam_init)).astype(o_ref.dtype)


def _diff_attn(q, k, v, qchunk, kchunk, lam_vecs, subln_w, lam_init, Tq, TQ, TK):
    B = q.shape[0]
    L = k.shape[1]
    Lmain = L - TAIL
    assert Tq % TQ == 0 and Lmain % TK == 0
    nq, nt = Tq // TQ, Lmain // TK
    nk = jnp.asarray(_num_key_tiles(qchunk[:Tq], kchunk[:Lmain], TQ, TK))
    qc = jnp.asarray(qchunk[:Tq].reshape(Tq, 1).astype(np.int32))
    kcm = jnp.asarray(kchunk[:Lmain].reshape(nt, 1, TK).astype(np.int32))
    kct = jnp.asarray(kchunk[Lmain:].reshape(1, TAIL).astype(np.int32))
    R = 2 * TQ
    kern = functools.partial(_diff_attn_kernel, TQ=TQ, TK=TK, Lmain=Lmain, lam_init=lam_init)
    grid_spec = pltpu.PrefetchScalarGridSpec(
        num_scalar_prefetch=1,
        grid=(B, N_HEADS_A, nq),
        in_specs=[pl.BlockSpec((None, TQ, LANES), lambda b, h, i, nk: (b, i, h)),
                  pl.BlockSpec((None, L, LANES), lambda b, h, i, nk: (b, 0, h)),
                  pl.BlockSpec((None, L, LANES), lambda b, h, i, nk: (b, 0, h)),
                  pl.BlockSpec((TQ, 1), lambda b, h, i, nk: (i, 0)),
                  pl.BlockSpec((nt, 1, TK), lambda b, h, i, nk: (0, 0, 0)),
                  pl.BlockSpec((1, TAIL), lambda b, h, i, nk: (0, 0)),
                  pl.BlockSpec((4, HEAD_DIM_A), lambda b, h, i, nk: (0, 0)),
                  pl.BlockSpec((1, LANES), lambda b, h, i, nk: (0, 0))],
        out_specs=pl.BlockSpec((None, TQ, LANES), lambda b, h, i, nk: (b, i, h)),
        scratch_shapes=[pltpu.VMEM((R, LANES), _F32)] * 3,
    )
    return pl.pallas_call(
        kern, grid_spec=grid_spec,
        out_shape=jax.ShapeDtypeStruct((B, Tq, VA_W), _MXU),
        compiler_params=_cparams(("parallel", "parallel", "parallel")),
        name="diff_attn",
    )(nk, q, k, v, qc, kcm, kct, lam_vecs, subln_w.reshape(1, LANES))


def _dsa_kernel(nk_ref, qi_ref, kiwi_ref, qb_ref, kir_ref, kb_ref, vb_ref, qc_ref,
                kcm_ref, kct_ref, kom_ref, kot_ref, o_ref,
                key_sc, keyt_sc, m_sc, l_sc, acc_sc, *, TQ, TK, Lmain, topk):
    i = pl.program_id(1)
    nk = nk_ref[i]
    H = N_IDX_HEADS
    nblk = TK // LANES

    qi = qi_ref[...]
    head = lax.broadcasted_iota(jnp.int32, qi.shape, 1) // IDX_DIM
    zero = jnp.zeros_like(qi)
    qis = jnp.concatenate([jnp.where(head == h, qi, zero) for h in range(H)], axis=0)
    wi = kiwi_ref[:, IDX_DIM:IDX_DIM + H] * (H ** -0.5)
    qc = qc_ref[...]

    def score_keys(kir, kc):
        lg = _nt_dot(qis, kir) * (IDX_DIM ** -0.5)
        sc = wi[:, 0:1] * jnp.maximum(lg[0:TQ], 0.0)
        for h in range(1, H):
            sc = sc + wi[:, h:h + 1] * jnp.maximum(lg[h * TQ:(h + 1) * TQ], 0.0)
        return _sortable(jnp.where(kc <= qc, sc, -jnp.inf))

    def fill(j, carry):
        off = pl.multiple_of(j * TK, TK)
        key_sc[j] = score_keys(kir_ref[pl.ds(off, TK), :], kcm_ref[j])
        return carry

    lax.fori_loop(0, nk, fill, 0)
    keyt_sc[...] = score_keys(kir_ref[Lmain:Lmain + TAIL, :], kct_ref[...])

    def fold(ind):
        c = ind[:, 0:LANES]
        for b in range(1, ind.shape[1] // LANES):
            c = c + ind[:, b * LANES:(b + 1) * LANES]
        return c

    def count(pred_main, pred_tail):
        def body(j, c):
            return c + fold(jnp.where(pred_main(j), 1.0, 0.0))
        c = lax.fori_loop(0, nk, body, jnp.zeros((TQ, LANES), _F32))
        c = c + jnp.where(pred_tail(), 1.0, 0.0)
        return jnp.sum(c, axis=1, keepdims=True)

    def count_ge(t):
        return count(lambda j: key_sc[j] >= t, lambda: keyt_sc[...] >= t)

    kf = float(topk)

    def bisect(it, t_u):
        bit = jnp.left_shift(jnp.int32(1), 31 - it)
        cand_u = t_u | bit
        cnt = count_ge(cand_u ^ _INT_MIN)
        return jnp.where(cnt >= kf, cand_u, t_u)

    t_u = lax.fori_loop(0, 32, bisect, jnp.zeros((TQ, 1), jnp.int32))
    thr = jnp.maximum(t_u ^ _INT_MIN, _KEY_NEG_INF + 1)
    n_ge = count_ge(thr)
    excess = n_ge > kf

    @pl.when(jnp.max(jnp.where(excess, 1.0, 0.0)) > 0.0)
    def _():
        need = kf - count_ge(thr + 1)

        def tied_before(J):
            return count(lambda j: (key_sc[j] == thr) & (kom_ref[j] < J),
                         lambda: (keyt_sc[...] == thr) & (kot_ref[...] < J))

        def search(it, J):
            cand = J | jnp.left_shift(jnp.int32(1), 13 - it)
            return jnp.where(tied_before(cand) < need, cand, J)

        J = lax.fori_loop(0, 14, search, jnp.zeros((TQ, 1), jnp.int32))

        def drop(j, carry):
            kj = key_sc[j]
            key_sc[j] = jnp.where(excess & (kj == thr) & (kom_ref[j] > J), _KEY_NEG_INF, kj)
            return carry

        lax.fori_loop(0, nk, drop, 0)
        kt = keyt_sc[...]
        keyt_sc[...] = jnp.where(excess & (kt == thr) & (kot_ref[...] > J), _KEY_NEG_INF, kt)

    qs = jnp.concatenate([qb_ref[:, h * LANES:(h + 1) * LANES] for h in range(N_HEADS_B)], axis=0)
    m_sc[...] = jnp.full(m_sc.shape, _NEG, _F32)
    l_sc[...] = jnp.zeros(l_sc.shape, _F32)
    acc_sc[...] = jnp.zeros(acc_sc.shape, _F32)

    def attend(keys, kb, vb):
        bias = jnp.where(keys >= thr, 0.0, _NEG)
        s = _nt_dot(qs, kb) + jnp.concatenate([bias] * N_HEADS_B, axis=0)
        _softmax_step(s, vb, m_sc, l_sc, acc_sc)

    def body(j, carry):
        off = pl.multiple_of(j * TK, TK)
        attend(key_sc[j], kb_ref[pl.ds(off, TK), :], vb_ref[pl.ds(off, TK), :])
        return carry

    lax.fori_loop(0, nk, body, 0)
    attend(keyt_sc[...], kb_ref[Lmain:Lmain + TAIL, :], vb_ref[Lmain:Lmain + TAIL, :])
    o = acc_sc[...] / l_sc[...]
    for h in range(N_HEADS_B):
        o_ref[:, h * LANES:(h + 1) * LANES] = o[h * TQ:(h + 1) * TQ].astype(o_ref.dtype)


def _dsa(qi, kiwi, qb, kir, kb, vb, qchunk, kchunk, korder, topk, Tq, TQ, TK):
    B = qi.shape[0]
    L = kir.shape[1]
    Lmain = L - TAIL
    assert Tq % TQ == 0 and Lmain % TK == 0
    nq, nt = Tq // TQ, Lmain // TK
    nk = jnp.asarray(_num_key_tiles(qchunk[:Tq], kchunk[:Lmain], TQ, TK))
    qc = jnp.asarray(qchunk[:Tq].reshape(Tq, 1).astype(np.int32))
    main = lambda a: jnp.asarray(a[:Lmain].reshape(nt, 1, TK).astype(np.int32))
    tail = lambda a: jnp.asarray(a[Lmain:].reshape(1, TAIL).astype(np.int32))
    R = N_HEADS_B * TQ
    kern = functools.partial(_dsa_kernel, TQ=TQ, TK=TK, Lmain=Lmain, topk=topk)
    qrow = lambda w: pl.BlockSpec((None, TQ, w), lambda b, i, nk: (b, i, 0))
    krow = lambda w: pl.BlockSpec((None, L, w), lambda b, i, nk: (b, 0, 0))
    cm = pl.BlockSpec((nt, 1, TK), lambda b, i, nk: (0, 0, 0))
    ct = pl.BlockSpec((1, TAIL), lambda b, i, nk: (0, 0))
    grid_spec = pltpu.PrefetchScalarGridSpec(
        num_scalar_prefetch=1,
        grid=(B, nq),
        in_specs=[qrow(QI_W), qrow(LANES), qrow(QB_EXP_W), krow(QI_W), krow(KB_W), krow(KB_W),
                  pl.BlockSpec((TQ, 1), lambda b, i, nk: (i, 0)), cm, ct, cm, ct],
        out_specs=qrow(QB_EXP_W),
        scratch_shapes=[pltpu.VMEM((nt, TQ, TK), jnp.int32), pltpu.VMEM((TQ, TAIL), jnp.int32),
                        pltpu.VMEM((R, LANES), _F32), pltpu.VMEM((R, LANES), _F32),
                        pltpu.VMEM((R, LANES), _F32)],
    )
    return pl.pallas_call(
        kern, grid_spec=grid_spec,
        out_shape=jax.ShapeDtypeStruct((B, Tq, QB_EXP_W), _MXU),
        compiler_params=_cparams(("parallel", "parallel")),
        name="dsa",
    )(nk, qi, kiwi, qb, kir, kb, vb, qc, main(kchunk), tail(kchunk), main(korder), tail(korder))


def _post_kernel(x_ref, oa_ref, ob_ref, ga_ref, gb_ref, wpa_ref, wpb_ref, wo_ref, g_ref,
                 wr_ref, br_ref, x1_o, h2_o, route_o):
    pa = jnp.dot(oa_ref[...], wpa_ref[...], preferred_element_type=_F32)
    pb = jnp.dot(ob_ref[...], wpb_ref[...], preferred_element_type=_F32)
    merged = ga_ref[...] * pa + gb_ref[...] * pb
    x1 = x_ref[...] + jnp.dot(merged.astype(wo_ref.dtype), wo_ref[...], preferred_element_type=_F32)
    x1_o[...] = x1
    ms = jnp.mean(x1 * x1, axis=-1, keepdims=True)
    h2 = (x1 * lax.rsqrt(ms + RMS_EPS) * g_ref[...]).astype(h2_o.dtype)
    h2_o[...] = h2
    cur = jnp.dot(h2, wr_ref[...], preferred_element_type=_F32) + br_ref[...]
    lane = lax.broadcasted_iota(jnp.int32, cur.shape, 1).astype(_F32)
    vals, idxs = [], []
    for _ in range(TOP_K_EXPERTS):
        m = jnp.max(cur, axis=1, keepdims=True)
        idx = jnp.min(jnp.where(cur == m, lane, float(LANES)), axis=1, keepdims=True)
        vals.append(m)
        idxs.append(idx)
        cur = jnp.where(lane == idx, -jnp.inf, cur)
    es = [jnp.exp(v - vals[0]) for v in vals]
    denom = es[0] + es[1] + es[2] + es[3]
    route = jnp.zeros(cur.shape, _F32)
    for k in range(TOP_K_EXPERTS):
        route = jnp.where(lane == float(k), idxs[k], route)
        route = jnp.where(lane == float(TOP_K_EXPERTS + k), es[k] / denom, route)
    route_o[...] = route


def _post(x, oa, ob, ga, gb, w_pa, w_pb_exp, w_o, g_ffn, w_router, b_router, T, tm):
    B, _, D = x.shape
    assert T % tm == 0
    row = lambda w: pl.BlockSpec((None, tm, w), lambda b, i: (b, i, 0))
    wr = jnp.concatenate([w_router, jnp.zeros((D, LANES - N_EXPERTS), w_router.dtype)], axis=1).astype(_MXU)
    br = jnp.concatenate([b_router.astype(_F32), jnp.full((LANES - N_EXPERTS,), _NEG, _F32)]).reshape(1, LANES)
    return pl.pallas_call(
        _post_kernel,
        grid=(B, T // tm),
        in_specs=[row(D), row(VA_W), row(QB_EXP_W), row(D), row(D),
                  _const_spec((VA_W, D)), _const_spec((QB_EXP_W, D)), _const_spec((D, D)),
                  _const_spec((1, D)), _const_spec((D, LANES)), _const_spec((1, LANES))],
        out_specs=[row(D), row(D), row(LANES)],
        out_shape=[jax.ShapeDtypeStruct((B, T, D), _F32), jax.ShapeDtypeStruct((B, T, D), _MXU),
                   jax.ShapeDtypeStruct((B, T, LANES), _F32)],
        compiler_params=_cparams(("parallel", "parallel")),
        name="post",
    )(x, oa, ob, ga, gb, w_pa.astype(_MXU), w_pb_exp, w_o.astype(_MXU), g_ffn.reshape(1, D), wr, br)


def _moe_kernel(be_ref, nu_ref, x_ref, wgu_ref, bgu_ref, wdn_ref, bdn_ref, y_ref):
    i = pl.program_id(0)

    @pl.when(i < nu_ref[0])
    def _():
        gu = jnp.dot(x_ref[...], wgu_ref[...], preferred_element_type=_F32) + bgu_ref[...]
        glu = jnp.minimum(gu[:, :D_EXPERT], SWIGLU_LIMIT)
        lin = jnp.clip(gu[:, D_EXPERT:], -SWIGLU_LIMIT, SWIGLU_LIMIT)
        act = glu * jax.nn.sigmoid(SWIGLU_ALPHA * glu) * (lin + 1.0)
        y_ref[...] = jnp.dot(act.astype(wdn_ref.dtype), wdn_ref[...],
                             preferred_element_type=_F32) + bdn_ref[...]

    @pl.when(i >= nu_ref[0])
    def _():
        y_ref[...] = jnp.zeros(y_ref.shape, y_ref.dtype)


def _moe(h2, route, w_gu, b_gu, w_dn, b_dn):
    T, D = h2.shape
    K = TOP_K_EXPERTS
    A = T * K
    e_flat = route[:, :K].astype(jnp.int32).reshape(-1)
    gate = route[:, K:2 * K]
    order = jnp.argsort(e_flat, stable=True).astype(jnp.int32)
    tok_sorted = order // K
    counts = jnp.sum((e_flat[:, None] == jnp.arange(N_EXPERTS, dtype=jnp.int32)[None, :]).astype(jnp.int32), axis=0)
    blocks_per = (counts + MOE_BLOCK - 1) // MOE_BLOCK
    padded = blocks_per * MOE_BLOCK
    pstart = jnp.cumsum(padded) - padded
    start = jnp.cumsum(counts) - counts
    NB = -(-A // MOE_BLOCK) + N_EXPERTS
    cum_blocks = jnp.cumsum(blocks_per)
    block_expert = jnp.minimum(
        jnp.searchsorted(cum_blocks, jnp.arange(NB, dtype=jnp.int32), side="right"), N_EXPERTS - 1).astype(jnp.int32)
    n_used = cum_blocks[-1:].astype(jnp.int32)
    slot = jnp.arange(NB * MOE_BLOCK, dtype=jnp.int32)
    e_slot = block_expert[slot // MOE_BLOCK]
    j = slot - pstart[e_slot]
    valid = (j >= 0) & (j < counts[e_slot]) & (slot // MOE_BLOCK < n_used[0])
    slot_tok = jnp.where(valid, tok_sorted[jnp.clip(start[e_slot] + j, 0, A - 1)], T)
    x_pad = jnp.concatenate([h2, jnp.zeros((1, D), h2.dtype)], axis=0)
    x_slots = x_pad[slot_tok]
    y_slots = pl.pallas_call(
        _moe_kernel,
        grid_spec=pltpu.PrefetchScalarGridSpec(
            num_scalar_prefetch=2,
            grid=(NB,),
            in_specs=[pl.BlockSpec((MOE_BLOCK, D), lambda i, be, nu: (i, 0)),
                      pl.BlockSpec((None, D, 2 * D_EXPERT), lambda i, be, nu: (be[i], 0, 0)),
                      pl.BlockSpec((None, 1, 2 * D_EXPERT), lambda i, be, nu: (be[i], 0, 0)),
                      pl.BlockSpec((None, D_EXPERT, D), lambda i, be, nu: (be[i], 0, 0)),
                      pl.BlockSpec((None, 1, D), lambda i, be, nu: (be[i], 0, 0))],
            out_specs=pl.BlockSpec((MOE_BLOCK, D), lambda i, be, nu: (i, 0)),
        ),
        out_shape=jax.ShapeDtypeStruct((NB * MOE_BLOCK, D), _F32),
        compiler_params=_cparams(("arbitrary",)),
        name="moe",
    )(block_expert, n_used, x_slots, w_gu.astype(_MXU), b_gu.reshape(N_EXPERTS, 1, -1).astype(_F32),
      w_dn.astype(_MXU), b_dn.reshape(N_EXPERTS, 1, -1).astype(_F32))
    rank_sorted = jnp.arange(A, dtype=jnp.int32) - start[e_flat[order]]
    dest_sorted = pstart[e_flat[order]] + rank_sorted
    inv = jnp.argsort(order).astype(jnp.int32)
    dest = dest_sorted[inv].reshape(T, K)
    return jnp.sum(y_slots[dest] * gate[:, :, None], axis=1)


def _final_kernel(x_ref, m_ref, g_ref, y_ref):
    x = x_ref[...] + m_ref[...]
    ms = jnp.mean(x * x, axis=-1, keepdims=True)
    y_ref[...] = x * lax.rsqrt(ms + RMS_EPS) * g_ref[...]


def _final(x1, moe_out, g_final, tm):
    T, D = x1.shape
    assert T % tm == 0
    row = pl.BlockSpec((tm, D), lambda i: (i, 0))
    return pl.pallas_call(
        _final_kernel, grid=(T // tm,),
        in_specs=[row, row, _const_spec((1, D))], out_specs=row,
        out_shape=jax.ShapeDtypeStruct((T, D), _F32),
        compiler_params=_cparams(("parallel",)),
        name="final",
    )(x1, moe_out, g_final.reshape(1, D))


def _group(x_rows, proj, keys, qchunk, kchunk, korder, topk, Tq, tiles, weights, lam_init):
    (lam_vecs, subln_w, w_pa, w_pb_exp, w_o, g_ffn, w_router, b_router,
     w_gu, b_gu, w_dn, b_dn, g_final) = weights
    B = x_rows.shape[0]
    oa = _diff_attn(proj["qa"], keys["ka"], keys["va"], qchunk, kchunk, lam_vecs, subln_w,
                    lam_init, Tq, tiles["tq_a"], tiles["tk"])
    ob = _dsa(proj["qi"], proj["kiwi"], proj["qb"], keys["kir"], keys["kb"], keys["vb"],
              qchunk, kchunk, korder, topk, Tq, tiles["tq_b"], tiles["tk"])
    x1, h2, route = _post(x_rows, oa, ob, proj["ga"], proj["gb"], w_pa, w_pb_exp, w_o, g_ffn,
                          w_router, b_router, Tq, tiles["tm_post"])
    T = B * Tq
    moe_out = _moe(h2.reshape(T, D_MODEL), route.reshape(T, LANES), w_gu, b_gu, w_dn, b_dn)
    y = _final(x1.reshape(T, D_MODEL), moe_out, g_final, tiles["tm_post"])
    return y.reshape(B, Tq, D_MODEL)


def kernel(x_prompt, x_sample, cache_a_k, cache_a_v, cache_b_k, cache_b_v, cache_idx_k,
           meta_tokens, g_mix, w_in, lambda_q1, lambda_k1, lambda_q2, lambda_k2, subln_w,
           w_pa, w_pb, w_o, g_ffn, w_router, b_router, w_gu, b_gu, w_dn, b_dn, g_final):
    B, S, D = x_prompt.shape
    Bs, S_dec, _ = x_sample.shape
    P = cache_a_k.shape[2]
    depth = g_mix.shape[0]
    assert depth == 1
    l = 0
    lam_init = 0.8 - 0.6 * math.exp(-0.3 * l)
    w_lay = _layout_w_in(w_in[l])
    w_pb_exp = _layout_w_pb(w_pb[l])
    lam_vecs = jnp.stack([lambda_q1[l], lambda_k1[l], lambda_q2[l], lambda_k2[l]]).astype(_F32)
    weights = (lam_vecs, subln_w[l], w_pa[l], w_pb_exp, w_o[l], g_ffn[l], w_router[l], b_router[l],
               w_gu[l], b_gu[l], w_dn[l], b_dn[l], g_final)

    Lp = S + TAIL
    pad = TAIL - N_META
    meta = jnp.broadcast_to(meta_tokens[None].astype(x_prompt.dtype), (B, N_META, D))
    x_lay = jnp.concatenate([x_prompt, meta, jnp.zeros((B, pad, D), x_prompt.dtype)], axis=1)
    r = np.arange(Lp)
    pos_p = np.where(r < S, r + N_META, np.where(r < S + N_META, r - S, 0)).astype(np.int32)
    chunk_p = np.where(r < S, r // CHUNK, np.where(r < S + N_META, -1, _BIG_CHUNK)).astype(np.int64)
    order_p = np.where(r < S, r + N_META, r - S).astype(np.int32)
    tm_p = 320 if Lp % 320 == 0 else TAIL
    pp = _project(x_lay, jnp.asarray(pos_p), g_mix[l], w_lay, tm_p)
    tiles_p = dict(tq_a=min(256, S), tq_b=min(128, S), tk=min(512, S), tm_post=min(512, S))
    y_prompt = _group(x_prompt, pp, pp, chunk_p, chunk_p, order_p, min(TOPK_MAX, S // 4), S, tiles_p,
                      weights, lam_init)

    def rows_p(a, shape):
        return jnp.concatenate([a[:, S:S + N_META], a[:, :S]], axis=1).reshape((1, B, N_META + S) + shape)

    pos_s = (P + np.arange(S_dec)).astype(np.int32)
    ps = _project(x_sample, jnp.asarray(pos_s), g_mix[l], w_lay, S_dec)
    Ls = P + TAIL
    r = np.arange(Ls)
    chunk_s = np.where(r < P + S_dec, r // CHUNK, _BIG_CHUNK).astype(np.int64)
    qchunk_s = (pos_s // CHUNK).astype(np.int64)

    def with_cache(cache, new):
        c = cache.reshape(Bs, P, -1).astype(_MXU)
        z = jnp.zeros((Bs, TAIL - S_dec, c.shape[-1]), _MXU)
        return jnp.concatenate([c, new, z], axis=1)

    keys_s = dict(ka=with_cache(cache_a_k[l], ps["ka"]), va=with_cache(cache_a_v[l], ps["va"]),
                  kb=with_cache(cache_b_k[l], ps["kb"]), vb=with_cache(cache_b_v[l], ps["vb"]),
                  kir=with_cache(jnp.tile(cache_idx_k[l], (1, 1, N_IDX_HEADS)), ps["kir"]))
    tiles_s = dict(tq_a=S_dec, tq_b=S_dec, tk=min(512, P), tm_post=S_dec)
    y_sample = _group(x_sample, ps, keys_s, qchunk_s, chunk_s, r.astype(np.int32),
                      min(TOPK_MAX, (P + S_dec) // 4), S_dec, tiles_s, weights, lam_init)

    def rows_s(a, shape):
        return a.reshape((1, Bs, S_dec) + shape)

    ha, hb = (N_HEADS_A, 2 * HEAD_DIM_A), (N_KV_B, HEAD_DIM_B)
    return (y_prompt, y_sample,
            rows_p(pp["ka_f"], ha), rows_p(pp["va_f"], ha), rows_p(pp["kb_f"], hb), rows_p(pp["vb_f"], hb),
            rows_p(pp["kiwi"][:, :, :IDX_DIM], (IDX_DIM,)),
            rows_s(ps["ka_f"], ha), rows_s(ps["va_f"], ha), rows_s(ps["kb_f"], hb), rows_s(ps["vb_f"], hb),
            rows_s(ps["kiwi"][:, :, :IDX_DIM], (IDX_DIM,)))
```

```python
import functools
import math

import numpy as np
import jax
import jax.numpy as jnp
from jax import lax
from jax.experimental import pallas as pl
from jax.experimental.pallas import tpu as pltpu

D_MODEL = 1024
CHUNK = 64
N_META = 16
ROPE_THETA = 500000.0
RMS_EPS = 1e-6
N_HEADS_A = 4
HEAD_DIM_A = 64
N_HEADS_B = 8
N_KV_B = 2
HEAD_DIM_B = 64
N_IDX_HEADS = 8
IDX_DIM = 32
TOPK_MAX = 256
N_EXPERTS = 32
TOP_K_EXPERTS = 4
D_EXPERT = 1024
SWIGLU_LIMIT = 7.0
SWIGLU_ALPHA = 1.702
MOE_BLOCK = 256

QA_W = N_HEADS_A * 2 * HEAD_DIM_A
VA_W = QA_W
QB_W = N_HEADS_B * HEAD_DIM_B
KB_W = N_KV_B * HEAD_DIM_B
QI_W = N_IDX_HEADS * IDX_DIM
SPLITS = (QA_W, QA_W, VA_W, QB_W, KB_W, KB_W, QI_W, IDX_DIM, N_IDX_HEADS, D_MODEL, D_MODEL)

LANES = 128
TAIL = LANES
QB_EXP_W = N_HEADS_B * LANES

_SEC = {}
_off = 0
for _name, _w in (("qa", QA_W), ("ka", QA_W), ("va", VA_W), ("qb", QB_EXP_W), ("kb", KB_W),
                  ("vb", KB_W), ("qi", QI_W), ("kiwi", LANES), ("kir", QI_W),
                  ("ga", D_MODEL), ("gb", D_MODEL)):
    _SEC[_name] = (_off, _off + _w)
    _off += _w
W_LAYOUT = _off

_MXU = jnp.bfloat16
_F32 = jnp.float32
_NEG = -1e30
_BIG_CHUNK = 2 ** 30
_INT_MIN = -2 ** 31
_LOG2E = math.log2(math.e)
_HALF = 2 ** 15
_KEY_NEG_INF = int(np.int32(np.uint32(0xFF800000) ^ np.uint32(0x7FFFFFFF)))
_VMEM_LIMIT = 56 * 1024 * 1024


def _cparams(sem):
    return pltpu.CompilerParams(dimension_semantics=sem, vmem_limit_bytes=_VMEM_LIMIT)


def _const_spec(shape):
    nd = len(shape)
    return pl.BlockSpec(shape, lambda *_: (0,) * nd)


def _nt_dot(a, b):
    return lax.dot_general(a, b, (((1,), (1,)), ((), ())), preferred_element_type=_F32)


def _sortable(x):
    b = lax.bitcast_convert_type(x, jnp.int32)
    return b ^ ((b >> 31) & 0x7FFFFFFF)


def _rope_block(z, cos, sin_lo, sin_hi, half):
    return (z * cos + pltpu.roll(z, LANES - half, 1) * sin_lo + pltpu.roll(z, half, 1) * sin_hi)


def _proj_kernel(x_ref, g_ref, w_ref, tab_ref,
                 qa_o, kaf_o, kab_o, vaf_o, vab_o, qb_o, kbf_o, kbb_o, vbf_o, vbb_o,
                 qi_o, kiwi_o, kir_o, ga_o, gb_o):
    x = x_ref[...]
    ms = jnp.mean(x * x, axis=-1, keepdims=True)
    h = (x * lax.rsqrt(ms + RMS_EPS) * g_ref[...]).astype(w_ref.dtype)

    def sec(name):
        a, b = _SEC[name]
        return jnp.dot(h, w_ref[:, a:b], preferred_element_type=_F32)

    t64 = tuple(tab_ref[:, k * LANES:(k + 1) * LANES] for k in (0, 1, 2))
    t32 = tuple(tab_ref[:, k * LANES:(k + 1) * LANES] for k in (3, 4, 5))
    tki = tuple(tab_ref[:, k * LANES:(k + 1) * LANES] for k in (6, 7, 8))

    def roped(z, tab, half):
        return [_rope_block(z[:, c * LANES:(c + 1) * LANES], *tab, half)
                for c in range(z.shape[1] // LANES)]

    scale_a = HEAD_DIM_A ** -0.5 * _LOG2E
    scale_b = HEAD_DIM_B ** -0.5 * _LOG2E
    for c, blk in enumerate(roped(sec("qa"), t64, 8)):
        qa_o[:, c * LANES:(c + 1) * LANES] = (blk * scale_a).astype(qa_o.dtype)
    for c, blk in enumerate(roped(sec("ka"), t64, 8)):
        kaf_o[:, c * LANES:(c + 1) * LANES] = blk
        kab_o[:, c * LANES:(c + 1) * LANES] = blk.astype(kab_o.dtype)
    va = sec("va")
    vaf_o[...] = va
    vab_o[...] = va.astype(vab_o.dtype)
    for c, blk in enumerate(roped(sec("qb"), t64, 8)):
        qb_o[:, c * LANES:(c + 1) * LANES] = (blk * scale_b).astype(qb_o.dtype)
    kb = roped(sec("kb"), t64, 8)[0]
    kbf_o[...] = kb
    kbb_o[...] = kb.astype(kbb_o.dtype)
    vb = sec("vb")
    vbf_o[...] = vb
    vbb_o[...] = vb.astype(vbb_o.dtype)
    for c, blk in enumerate(roped(sec("qi"), t32, 4)):
        qi_o[:, c * LANES:(c + 1) * LANES] = blk.astype(qi_o.dtype)
    kiwi_o[...] = roped(sec("kiwi"), tki, 4)[0]
    for c, blk in enumerate(roped(sec("kir"), t32, 4)):
        kir_o[:, c * LANES:(c + 1) * LANES] = blk.astype(kir_o.dtype)
    ga_o[...] = jax.nn.sigmoid(sec("ga"))
    gb_o[...] = jax.nn.sigmoid(sec("gb"))


def _rope_tables(pos):
    def cs(d):
        rd = d // 4
        half = rd // 2
        inv_freq = ROPE_THETA ** (-jnp.arange(half, dtype=_F32) * 2.0 / rd)
        ang = pos.astype(_F32)[:, None] * inv_freq[None, :]
        return jnp.cos(ang), jnp.sin(ang), half

    T = pos.shape[0]

    def pattern(d, width):
        c, s, half = cs(d)
        one = jnp.ones((T, width - 2 * half), _F32)
        zero = lambda n: jnp.zeros((T, n), _F32)
        cos = jnp.concatenate([c, c, one], axis=1)
        lo = jnp.concatenate([-s, zero(width - half)], axis=1)
        hi = jnp.concatenate([zero(half), s, zero(width - 2 * half)], axis=1)
        return cos, lo, hi

    tabs = []
    for d, width in ((64, 64), (32, 32), (32, LANES)):
        tabs += [jnp.tile(t, (1, LANES // width)) for t in pattern(d, width)]
    return jnp.concatenate(tabs, axis=1)


def _layout_w_in(w_in):
    offs = np.cumsum((0,) + SPLITS)
    qa, ka, va, qb, kb, vb, qi, ki, wi, ga, gb = (w_in[:, offs[i]:offs[i + 1]] for i in range(11))
    D = w_in.shape[0]
    z64 = jnp.zeros((D, HEAD_DIM_B), w_in.dtype)
    per_kv = N_HEADS_B // N_KV_B
    qb_exp = []
    for hq in range(N_HEADS_B):
        col = qb[:, hq * HEAD_DIM_B:(hq + 1) * HEAD_DIM_B]
        qb_exp += [col, z64] if hq // per_kv == 0 else [z64, col]
    kiwi = jnp.concatenate([ki, wi, jnp.zeros((D, LANES - IDX_DIM - N_IDX_HEADS), w_in.dtype)], axis=1)
    kir = jnp.tile(ki, (1, N_IDX_HEADS))
    w = jnp.concatenate([qa, ka, va] + qb_exp + [kb, vb, qi, kiwi, kir, ga, gb], axis=1)
    assert w.shape[1] == W_LAYOUT
    return w.astype(_MXU)


def _layout_w_pb(w_pb):
    D = w_pb.shape[1]
    z64 = jnp.zeros((HEAD_DIM_B, D), w_pb.dtype)
    per_kv = N_HEADS_B // N_KV_B
    rows = []
    for hq in range(N_HEADS_B):
        r = w_pb[hq * HEAD_DIM_B:(hq + 1) * HEAD_DIM_B]
        rows += [r, z64] if hq // per_kv == 0 else [z64, r]
    return jnp.concatenate(rows, axis=0).astype(_MXU)


def _project(x, pos, g_mix, w_lay, tm):
    B, T, D = x.shape
    assert T % tm == 0
    tab = _rope_tables(pos)
    row = lambda w: pl.BlockSpec((None, tm, w), lambda b, i: (b, i, 0))
    outs = [("qa", QA_W, _MXU), ("ka_f", QA_W, _F32), ("ka", QA_W, _MXU), ("va_f", VA_W, _F32),
            ("va", VA_W, _MXU), ("qb", QB_EXP_W, _MXU), ("kb_f", KB_W, _F32), ("kb", KB_W, _MXU),
            ("vb_f", KB_W, _F32), ("vb", KB_W, _MXU), ("qi", QI_W, _MXU), ("kiwi", LANES, _F32),
            ("kir", QI_W, _MXU), ("ga", D_MODEL, _F32), ("gb", D_MODEL, _F32)]
    res = pl.pallas_call(
        _proj_kernel,
        grid=(B, T // tm),
        in_specs=[row(D), _const_spec((1, D)),
                  pl.BlockSpec((D, W_LAYOUT), lambda b, i: (0, 0), pipeline_mode=pl.Buffered(1)),
                  pl.BlockSpec((tm, 9 * LANES), lambda b, i: (i, 0))],
        out_specs=[row(w) for _, w, _ in outs],
        out_shape=[jax.ShapeDtypeStruct((B, T, w), dt) for _, w, dt in outs],
        compiler_params=_cparams(("parallel", "parallel")),
        name="proj",
    )(x, g_mix.reshape(1, D), w_lay, tab)
    return {n: r for (n, _, _), r in zip(outs, res)}


def _tile_lanes(v, n):
    return v if n == 1 else jnp.concatenate([v] * n, axis=1)


def _softmax_step(s, v, m_sc, l_sc, acc_sc):
    m_prev = m_sc[...]
    m_new = jnp.maximum(m_prev, jnp.max(s, axis=1, keepdims=True))
    alpha = jnp.exp2(m_prev - m_new)
    p = jnp.exp2(s - _tile_lanes(m_new, s.shape[1] // LANES))
    l_sc[...] = alpha * l_sc[...] + jnp.sum(p, axis=1, keepdims=True)
    acc_sc[...] = alpha * acc_sc[...] + jnp.dot(p.astype(v.dtype), v, preferred_element_type=_F32)
    m_sc[...] = m_new


def _attend_tiles(n, scores, values, s0_sc, s1_sc, m_sc, l_sc, acc_sc):
    s0_sc[...] = scores(0)

    def pair(p, carry):
        j = 2 * p
        s1_sc[...] = scores(j + 1)
        _softmax_step(s0_sc[...], values(j), m_sc, l_sc, acc_sc)
        s0_sc[...] = scores(jnp.minimum(j + 2, n - 1))
        _softmax_step(s1_sc[...], values(j + 1), m_sc, l_sc, acc_sc)
        return carry

    lax.fori_loop(0, n // 2, pair, 0)

    @pl.when(n % 2 == 1)
    def _():
        _softmax_step(s0_sc[...], values(n - 1), m_sc, l_sc, acc_sc)


def _num_full_tiles(qchunk, kchunk_main, TQ, TK):
    last = kchunk_main.reshape(-1, TK).max(axis=1)
    qmin = qchunk.reshape(-1, TQ).min(axis=1)
    full = last[None, :] <= qmin[:, None]
    return np.cumprod(full, axis=1).sum(axis=1).astype(np.int32)


def _num_key_tiles(qchunk, kchunk_main, TQ, TK):
    first = kchunk_main.reshape(-1, TK)[:, 0]
    qmax = qchunk.reshape(-1, TQ).max(axis=1)
    return (first[None, :] <= qmax[:, None]).sum(axis=1).astype(np.int32)


def _diff_attn_kernel(nk_ref, nf_ref, q_ref, k_ref, v_ref, qc_ref, kcm_ref, kct_ref, lam_ref, sub_ref,
                      o_ref, s0_sc, s1_sc, m_sc, l_sc, acc_sc, *, TQ, TK, Lmain, lam_init):
    i = pl.program_id(2)
    q = q_ref[...]
    lane = lax.broadcasted_iota(jnp.int32, q.shape, 1)
    zero = jnp.zeros_like(q)
    qs = jnp.concatenate([jnp.where(lane < HEAD_DIM_A, q, zero),
                          jnp.where(lane >= HEAD_DIM_A, q, zero)], axis=0)
    qc = qc_ref[...]
    qc2 = jnp.concatenate([qc, qc], axis=0)
    m_sc[...] = jnp.full(m_sc.shape, _NEG, _F32)
    l_sc[...] = jnp.zeros(l_sc.shape, _F32)
    acc_sc[...] = jnp.zeros(acc_sc.shape, _F32)

    def keys(j):
        return k_ref[pl.ds(pl.multiple_of(j * TK, TK), TK), :]

    def values(j):
        return v_ref[pl.ds(pl.multiple_of(j * TK, TK), TK), :]

    def masked_scores(k, kc):
        return jnp.where(kc <= qc2, _nt_dot(qs, k), _NEG)

    _attend_tiles(nf_ref[i], lambda j: _nt_dot(qs, keys(j)), values,
                  s0_sc, s1_sc, m_sc, l_sc, acc_sc)

    def diagonal(j, carry):
        _softmax_step(masked_scores(keys(j), kcm_ref[j]), values(j), m_sc, l_sc, acc_sc)
        return carry

    lax.fori_loop(nf_ref[i], nk_ref[i], diagonal, 0)
    _softmax_step(masked_scores(k_ref[Lmain:Lmain + TAIL, :], kct_ref[...]),
                  v_ref[Lmain:Lmain + TAIL, :], m_sc, l_sc, acc_sc)

    lam = (jnp.exp(jnp.sum(lam_ref[0:1, :] * lam_ref[1:2, :], axis=1, keepdims=True))
           - jnp.exp(jnp.sum(lam_ref[2:3, :] * lam_ref[3:4, :], axis=1, keepdims=True))
           + lam_init)
    o = acc_sc[...] / l_sc[...]
    d = o[:TQ] - lam * o[TQ:]
    ms = jnp.mean(d * d, axis=-1, keepdims=True)
    o_ref[...] = ((d * lax.rsqrt(ms + RMS_EPS) * sub_ref[...]) * (1.0 - lam_init)).astype(o_ref.dtype)


def _diff_attn(q, k, v, qchunk, kchunk, lam_vecs, subln_w, lam_init, Tq, TQ, TK):
    B = q.shape[0]
    L = k.shape[1]
    Lmain = L - TAIL
    assert Tq % TQ == 0 and Lmain % TK == 0
    nq, nt = Tq // TQ, Lmain // TK
    nk = jnp.asarray(_num_key_tiles(qchunk[:Tq], kchunk[:Lmain], TQ, TK))
    nf = jnp.asarray(_num_full_tiles(qchunk[:Tq], kchunk[:Lmain], TQ, TK))
    qc = jnp.asarray(qchunk[:Tq].reshape(Tq, 1).astype(np.int32))
    kcm = jnp.asarray(kchunk[:Lmain].reshape(nt, 1, TK).astype(np.int32))
    kct = jnp.asarray(kchunk[Lmain:].reshape(1, TAIL).astype(np.int32))
    R = 2 * TQ
    kern = functools.partial(_diff_attn_kernel, TQ=TQ, TK=TK, Lmain=Lmain, lam_init=lam_init)
    grid_spec = pltpu.PrefetchScalarGridSpec(
        num_scalar_prefetch=2,
        grid=(B, N_HEADS_A, nq),
        in_specs=[pl.BlockSpec((None, TQ, LANES), lambda b, h, i, *_: (b, i, h)),
                  pl.BlockSpec((None, L, LANES), lambda b, h, i, *_: (b, 0, h)),
                  pl.BlockSpec((None, L, LANES), lambda b, h, i, *_: (b, 0, h)),
                  pl.BlockSpec((TQ, 1), lambda b, h, i, *_: (i, 0)),
                  pl.BlockSpec((nt, 1, TK), lambda b, h, i, *_: (0, 0, 0)),
                  pl.BlockSpec((1, TAIL), lambda b, h, i, *_: (0, 0)),
                  pl.BlockSpec((4, HEAD_DIM_A), lambda b, h, i, *_: (0, 0)),
                  pl.BlockSpec((1, LANES), lambda b, h, i, *_: (0, 0))],
        out_specs=pl.BlockSpec((None, TQ, LANES), lambda b, h, i, *_: (b, i, h)),
        scratch_shapes=[pltpu.VMEM((R, TK), _F32)] * 2 + [pltpu.VMEM((R, LANES), _F32)] * 3,
    )
    return pl.pallas_call(
        kern, grid_spec=grid_spec,
        out_shape=jax.ShapeDtypeStruct((B, Tq, VA_W), _MXU),
        compiler_params=_cparams(("parallel", "parallel", "parallel")),
        name="diff_attn",
    )(nk, nf, q, k, v, qc, kcm, kct, lam_vecs, subln_w.reshape(1, LANES))


def _dsa_kernel(nk_ref, qi_ref, kiwi_ref, qb_ref, kir_ref, kb_ref, vb_ref, qc_ref,
                kcm_ref, kct_ref, kom_ref, kot_ref, o_ref,
                key_sc, keyt_sc, hi_sc, lo_sc, hit_sc, lot_sc, s0_sc, s1_sc, m_sc, l_sc, acc_sc,
                *, TQ, TK, Lmain, topk):
    i = pl.program_id(1)
    nk = nk_ref[i]
    H = N_IDX_HEADS
    nblk = TK // LANES

    qi = qi_ref[...]
    head = lax.broadcasted_iota(jnp.int32, qi.shape, 1) // IDX_DIM
    zero = jnp.zeros_like(qi)
    qis = jnp.concatenate([jnp.where(head == h, qi, zero) for h in range(H)], axis=0)
    wi = kiwi_ref[:, IDX_DIM:IDX_DIM + H] * (H ** -0.5) * (IDX_DIM ** -0.5)
    wi_b = [jnp.broadcast_to(wi[:, h:h + 1], (TQ, LANES)) for h in range(H)]
    qc = qc_ref[...]

    def score_keys(kir, kc):
        lg = _nt_dot(qis, kir)
        n = lg.shape[1] // LANES
        sc = _tile_lanes(wi_b[0], n) * jnp.maximum(lg[0:TQ], 0.0)
        for h in range(1, H):
            sc = sc + _tile_lanes(wi_b[h], n) * jnp.maximum(lg[h * TQ:(h + 1) * TQ], 0.0)
        return _sortable(jnp.where(kc <= qc, sc, -jnp.inf))

    def split16(key):
        return (key >> 16).astype(jnp.int16), ((key & 0xFFFF) - _HALF).astype(jnp.int16)

    def fill(j, carry):
        off = pl.multiple_of(j * TK, TK)
        key = score_keys(kir_ref[pl.ds(off, TK), :], kcm_ref[j])
        key_sc[j] = key
        hi_sc[j], lo_sc[j] = split16(key)
        return carry

    lax.fori_loop(0, nk, fill, 0)
    key_t = score_keys(kir_ref[Lmain:Lmain + TAIL, :], kct_ref[...])
    keyt_sc[...] = key_t
    hit_sc[...], lot_sc[...] = split16(key_t)

    def count(ind_main, ind_tail, dtype):
        def body(j, c):
            ind = ind_main(j)
            for b in range(nblk):
                c = c + ind[:, b * LANES:(b + 1) * LANES]
            return c
        c = lax.fori_loop(0, nk, body, jnp.zeros((TQ, LANES), dtype)) + ind_tail()
        return jnp.sum(c.astype(_F32), axis=1, keepdims=True)

    one16, zero16 = jnp.int16(1), jnp.int16(0)

    def lanes16(v):
        return jnp.broadcast_to(v, (TQ, LANES)).astype(jnp.int16)

    def count16(main_sc, tail_sc, t16, strict=False):
        tm = _tile_lanes(t16, nblk)
        cmp = (lambda a, b: a > b) if strict else (lambda a, b: a >= b)
        return count(lambda j: jnp.where(cmp(main_sc[j], tm), one16, zero16),
                     lambda: jnp.where(cmp(tail_sc[...], t16), one16, zero16), jnp.int16)

    def kth_half(main_sc, tail_sc, target):
        def step(it, t_u):
            cand_u = t_u | jnp.left_shift(jnp.int32(1), 15 - it)
            cnt = count16(main_sc, tail_sc, lanes16(cand_u - _HALF))
            return jnp.where(cnt >= target, cand_u, t_u)
        return lax.fori_loop(0, 16, step, jnp.zeros((TQ, 1), jnp.int32))

    kf = float(topk)
    t_hi = kth_half(hi_sc, hit_sc, kf) - _HALF
    t_hi16 = lanes16(t_hi)
    n_above = count16(hi_sc, hit_sc, t_hi16, strict=True)
    t_hi16m = _tile_lanes(t_hi16, nblk)
    low16 = jnp.int16(-_HALF)

    def bucket(j, carry):
        lo_sc[j] = jnp.where(hi_sc[j] == t_hi16m, lo_sc[j], low16)
        return carry

    lax.fori_loop(0, nk, bucket, 0)
    lot_sc[...] = jnp.where(hit_sc[...] == t_hi16, lot_sc[...], low16)
    t_lo = kth_half(lo_sc, lot_sc, kf - n_above)
    n_ge = n_above + count16(lo_sc, lot_sc, lanes16(t_lo - _HALF))
    kth = (t_hi << 16) | t_lo
    thr1 = jnp.maximum(kth, _KEY_NEG_INF + 1)
    thr = jnp.broadcast_to(thr1, (TQ, LANES))
    thr_m = _tile_lanes(thr, nblk)
    excess = jnp.broadcast_to((n_ge > kf) & (kth > _KEY_NEG_INF), (TQ, LANES))

    @pl.when(jnp.max(jnp.where(excess, 1.0, 0.0)) > 0.0)
    def _():
        def count32(pred_main, pred_tail):
            return count(lambda j: jnp.where(pred_main(j), 1.0, 0.0),
                         lambda: jnp.where(pred_tail(), 1.0, 0.0), _F32)

        need = kf - count32(lambda j: key_sc[j] > thr_m, lambda: keyt_sc[...] > thr)
        excess_m = _tile_lanes(excess, nblk)

        def tied_before(J):
            Jm = _tile_lanes(J, nblk)
            return count32(lambda j: (key_sc[j] == thr_m) & (kom_ref[j] < Jm),
                           lambda: (keyt_sc[...] == thr) & (kot_ref[...] < J))

        def search(it, J):
            cand = J | jnp.left_shift(jnp.int32(1), 13 - it)
            return jnp.where(tied_before(cand) < need, cand, J)

        J = lax.fori_loop(0, 14, search, jnp.zeros((TQ, LANES), jnp.int32))
        Jm = _tile_lanes(J, nblk)

        def drop(j, carry):
            kj = key_sc[j]
            key_sc[j] = jnp.where(excess_m & (kj == thr_m) & (kom_ref[j] > Jm), _KEY_NEG_INF, kj)
            return carry

        lax.fori_loop(0, nk, drop, 0)
        kt = keyt_sc[...]
        keyt_sc[...] = jnp.where(excess & (kt == thr) & (kot_ref[...] > J), _KEY_NEG_INF, kt)

    qs = jnp.concatenate([qb_ref[:, h * LANES:(h + 1) * LANES] for h in range(N_HEADS_B)], axis=0)
    m_sc[...] = jnp.full(m_sc.shape, _NEG, _F32)
    l_sc[...] = jnp.zeros(l_sc.shape, _F32)
    acc_sc[...] = jnp.zeros(acc_sc.shape, _F32)

    def masked_scores(keys, kb):
        bias = jnp.where(keys >= _tile_lanes(thr, keys.shape[1] // LANES), 0.0, _NEG)
        return _nt_dot(qs, kb) + jnp.concatenate([bias] * N_HEADS_B, axis=0)

    def scores(j):
        return masked_scores(key_sc[j], kb_ref[pl.ds(pl.multiple_of(j * TK, TK), TK), :])

    def values(j):
        return vb_ref[pl.ds(pl.multiple_of(j * TK, TK), TK), :]

    _attend_tiles(nk, scores, values, s0_sc, s1_sc, m_sc, l_sc, acc_sc)
    _softmax_step(masked_scores(keyt_sc[...], kb_ref[Lmain:Lmain + TAIL, :]),
                  vb_ref[Lmain:Lmain + TAIL, :], m_sc, l_sc, acc_sc)
    o = acc_sc[...] / l_sc[...]
    for h in range(N_HEADS_B):
        o_ref[:, h * LANES:(h + 1) * LANES] = o[h * TQ:(h + 1) * TQ].astype(o_ref.dtype)


def _dsa(qi, kiwi, qb, kir, kb, vb, qchunk, kchunk, korder, topk, Tq, TQ, TK):
    B = qi.shape[0]
    L = kir.shape[1]
    Lmain = L - TAIL
    assert Tq % TQ == 0 and Lmain % TK == 0
    nq, nt = Tq // TQ, Lmain // TK
    nk = jnp.asarray(_num_key_tiles(qchunk[:Tq], kchunk[:Lmain], TQ, TK))
    qc = jnp.asarray(qchunk[:Tq].reshape(Tq, 1).astype(np.int32))
    main = lambda a: jnp.asarray(a[:Lmain].reshape(nt, 1, TK).astype(np.int32))
    tail = lambda a: jnp.asarray(a[Lmain:].reshape(1, TAIL).astype(np.int32))
    R = N_HEADS_B * TQ
    kern = functools.partial(_dsa_kernel, TQ=TQ, TK=TK, Lmain=Lmain, topk=topk)
    qrow = lambda w: pl.BlockSpec((None, TQ, w), lambda b, i, nk: (b, i, 0))
    krow = lambda w: pl.BlockSpec((None, L, w), lambda b, i, nk: (b, 0, 0))
    cm = pl.BlockSpec((nt, 1, TK), lambda b, i, nk: (0, 0, 0))
    ct = pl.BlockSpec((1, TAIL), lambda b, i, nk: (0, 0))
    grid_spec = pltpu.PrefetchScalarGridSpec(
        num_scalar_prefetch=1,
        grid=(B, nq),
        in_specs=[qrow(QI_W), qrow(LANES), qrow(QB_EXP_W), krow(QI_W), krow(KB_W), krow(KB_W),
                  pl.BlockSpec((TQ, 1), lambda b, i, nk: (i, 0)), cm, ct, cm, ct],
        out_specs=qrow(QB_EXP_W),
        scratch_shapes=[pltpu.VMEM((nt, TQ, TK), jnp.int32), pltpu.VMEM((TQ, TAIL), jnp.int32),
                        pltpu.VMEM((nt, TQ, TK), jnp.int16), pltpu.VMEM((nt, TQ, TK), jnp.int16),
                        pltpu.VMEM((TQ, TAIL), jnp.int16), pltpu.VMEM((TQ, TAIL), jnp.int16),
                        pltpu.VMEM((R, TK), _F32), pltpu.VMEM((R, TK), _F32),
                        pltpu.VMEM((R, LANES), _F32), pltpu.VMEM((R, LANES), _F32),
                        pltpu.VMEM((R, LANES), _F32)],
    )
    return pl.pallas_call(
        kern, grid_spec=grid_spec,
        out_shape=jax.ShapeDtypeStruct((B, Tq, QB_EXP_W), _MXU),
        compiler_params=_cparams(("parallel", "parallel")),
        name="dsa",
    )(nk, qi, kiwi, qb, kir, kb, vb, qc, main(kchunk), tail(kchunk), main(korder), tail(korder))


def _post_kernel(x_ref, oa_ref, ob_ref, ga_ref, gb_ref, wpa_ref, wpb_ref, wo_ref, g_ref,
                 wr_ref, br_ref, x1_o, h2_o, route_o):
    pa = jnp.dot(oa_ref[...], wpa_ref[...], preferred_element_type=_F32)
    pb = jnp.dot(ob_ref[...], wpb_ref[...], preferred_element_type=_F32)
    merged = ga_ref[...] * pa + gb_ref[...] * pb
    x1 = x_ref[...] + jnp.dot(merged.astype(wo_ref.dtype), wo_ref[...], preferred_element_type=_F32)
    x1_o[...] = x1
    ms = jnp.mean(x1 * x1, axis=-1, keepdims=True)
    h2 = (x1 * lax.rsqrt(ms + RMS_EPS) * g_ref[...]).astype(h2_o.dtype)
    h2_o[...] = h2
    cur = jnp.dot(h2, wr_ref[...], preferred_element_type=_F32) + br_ref[...]
    lane = lax.broadcasted_iota(jnp.int32, cur.shape, 1).astype(_F32)
    vals, idxs = [], []
    for _ in range(TOP_K_EXPERTS):
        m = jnp.max(cur, axis=1, keepdims=True)
        idx = jnp.min(jnp.where(cur == m, lane, float(LANES)), axis=1, keepdims=True)
        vals.append(m)
        idxs.append(idx)
        cur = jnp.where(lane == idx, -jnp.inf, cur)
    es = [jnp.exp(v - vals[0]) for v in vals]
    denom = es[0] + es[1] + es[2] + es[3]
    route = jnp.zeros(cur.shape, _F32)
    for k in range(TOP_K_EXPERTS):
        route = jnp.where(lane == float(k), idxs[k], route)
        route = jnp.where(lane == float(TOP_K_EXPERTS + k), es[k] / denom, route)
    route_o[...] = route


def _post(x, oa, ob, ga, gb, w_pa, w_pb_exp, w_o, g_ffn, w_router, b_router, T, tm):
    B, _, D = x.shape
    assert T % tm == 0
    row = lambda w: pl.BlockSpec((None, tm, w), lambda b, i: (b, i, 0))
    wr = jnp.concatenate([w_router, jnp.zeros((D, LANES - N_EXPERTS), w_router.dtype)], axis=1).astype(_MXU)
    br = jnp.concatenate([b_router.astype(_F32), jnp.full((LANES - N_EXPERTS,), _NEG, _F32)]).reshape(1, LANES)
    return pl.pallas_call(
        _post_kernel,
        grid=(B, T // tm),
        in_specs=[row(D), row(VA_W), row(QB_EXP_W), row(D), row(D),
                  _const_spec((VA_W, D)), _const_spec((QB_EXP_W, D)), _const_spec((D, D)),
                  _const_spec((1, D)), _const_spec((D, LANES)), _const_spec((1, LANES))],
        out_specs=[row(D), row(D), row(LANES)],
        out_shape=[jax.ShapeDtypeStruct((B, T, D), _F32), jax.ShapeDtypeStruct((B, T, D), _MXU),
                   jax.ShapeDtypeStruct((B, T, LANES), _F32)],
        compiler_params=_cparams(("parallel", "parallel")),
        name="post",
    )(x, oa, ob, ga, gb, w_pa.astype(_MXU), w_pb_exp, w_o.astype(_MXU), g_ffn.reshape(1, D), wr, br)


def _moe_kernel(be_ref, nu_ref, x_ref, wgu_ref, bgu_ref, wdn_ref, bdn_ref, y_ref):
    i = pl.program_id(0)

    @pl.when(i < nu_ref[0])
    def _():
        gu = jnp.dot(x_ref[...], wgu_ref[...], preferred_element_type=_F32) + bgu_ref[...]
        glu = jnp.minimum(gu[:, :D_EXPERT], SWIGLU_LIMIT)
        lin = jnp.clip(gu[:, D_EXPERT:], -SWIGLU_LIMIT, SWIGLU_LIMIT)
        act = glu * jax.nn.sigmoid(SWIGLU_ALPHA * glu) * (lin + 1.0)
        y_ref[...] = jnp.dot(act.astype(wdn_ref.dtype), wdn_ref[...],
                             preferred_element_type=_F32) + bdn_ref[...]

    @pl.when(i >= nu_ref[0])
    def _():
        y_ref[...] = jnp.zeros(y_ref.shape, y_ref.dtype)


def _moe(h2, route, w_gu, b_gu, w_dn, b_dn):
    T, D = h2.shape
    K = TOP_K_EXPERTS
    A = T * K
    e_flat = route[:, :K].astype(jnp.int32).reshape(-1)
    gate = route[:, K:2 * K]
    order = jnp.argsort(e_flat, stable=True).astype(jnp.int32)
    tok_sorted = order // K
    counts = jnp.sum((e_flat[:, None] == jnp.arange(N_EXPERTS, dtype=jnp.int32)[None, :]).astype(jnp.int32), axis=0)
    blocks_per = (counts + MOE_BLOCK - 1) // MOE_BLOCK
    padded = blocks_per * MOE_BLOCK
    pstart = jnp.cumsum(padded) - padded
    start = jnp.cumsum(counts) - counts
    NB = -(-A // MOE_BLOCK) + N_EXPERTS
    cum_blocks = jnp.cumsum(blocks_per)
    block_expert = jnp.minimum(
        jnp.searchsorted(cum_blocks, jnp.arange(NB, dtype=jnp.int32), side="right"), N_EXPERTS - 1).astype(jnp.int32)
    n_used = cum_blocks[-1:].astype(jnp.int32)
    slot = jnp.arange(NB * MOE_BLOCK, dtype=jnp.int32)
    e_slot = block_expert[slot // MOE_BLOCK]
    j = slot - pstart[e_slot]
    valid = (j >= 0) & (j < counts[e_slot]) & (slot // MOE_BLOCK < n_used[0])
    slot_tok = jnp.where(valid, tok_sorted[jnp.clip(start[e_slot] + j, 0, A - 1)], T)
    x_pad = jnp.concatenate([h2, jnp.zeros((1, D), h2.dtype)], axis=0)
    x_slots = x_pad[slot_tok]
    y_slots = pl.pallas_call(
        _moe_kernel,
        grid_spec=pltpu.PrefetchScalarGridSpec(
            num_scalar_prefetch=2,
            grid=(NB,),
            in_specs=[pl.BlockSpec((MOE_BLOCK, D), lambda i, be, nu: (i, 0)),
                      pl.BlockSpec((None, D, 2 * D_EXPERT), lambda i, be, nu: (be[i], 0, 0)),
                      pl.BlockSpec((None, 1, 2 * D_EXPERT), lambda i, be, nu: (be[i], 0, 0)),
                      pl.BlockSpec((None, D_EXPERT, D), lambda i, be, nu: (be[i], 0, 0)),
                      pl.BlockSpec((None, 1, D), lambda i, be, nu: (be[i], 0, 0))],
            out_specs=pl.BlockSpec((MOE_BLOCK, D), lambda i, be, nu: (i, 0)),
        ),
        out_shape=jax.ShapeDtypeStruct((NB * MOE_BLOCK, D), _F32),
        compiler_params=_cparams(("arbitrary",)),
        name="moe",
    )(block_expert, n_used, x_slots, w_gu.astype(_MXU), b_gu.reshape(N_EXPERTS, 1, -1).astype(_F32),
      w_dn.astype(_MXU), b_dn.reshape(N_EXPERTS, 1, -1).astype(_F32))
    rank_sorted = jnp.arange(A, dtype=jnp.int32) - start[e_flat[order]]
    dest_sorted = pstart[e_flat[order]] + rank_sorted
    inv = jnp.argsort(order).astype(jnp.int32)
    dest = dest_sorted[inv].reshape(T, K)
    return jnp.sum(y_slots[dest] * gate[:, :, None], axis=1)


def _final_kernel(x_ref, m_ref, g_ref, y_ref):
    x = x_ref[...] + m_ref[...]
    ms = jnp.mean(x * x, axis=-1, keepdims=True)
    y_ref[...] = x * lax.rsqrt(ms + RMS_EPS) * g_ref[...]


def _final(x1, moe_out, g_final, tm):
    T, D = x1.shape
    assert T % tm == 0
    row = pl.BlockSpec((tm, D), lambda i: (i, 0))
    return pl.pallas_call(
        _final_kernel, grid=(T // tm,),
        in_specs=[row, row, _const_spec((1, D))], out_specs=row,
        out_shape=jax.ShapeDtypeStruct((T, D), _F32),
        compiler_params=_cparams(("parallel",)),
        name="final",
    )(x1, moe_out, g_final.reshape(1, D))


def _group(x_rows, proj, keys, qchunk, kchunk, korder, topk, Tq, tiles, weights, lam_init):
    (lam_vecs, subln_w, w_pa, w_pb_exp, w_o, g_ffn, w_router, b_router,
     w_gu, b_gu, w_dn, b_dn, g_final) = weights
    B = x_rows.shape[0]
    oa = _diff_attn(proj["qa"], keys["ka"], keys["va"], qchunk, kchunk, lam_vecs, subln_w,
                    lam_init, Tq, tiles["tq_a"], tiles["tk"])
    ob = _dsa(proj["qi"], proj["kiwi"], proj["qb"], keys["kir"], keys["kb"], keys["vb"],
              qchunk, kchunk, korder, topk, Tq, tiles["tq_b"], tiles["tk"])
    x1, h2, route = _post(x_rows, oa, ob, proj["ga"], proj["gb"], w_pa, w_pb_exp, w_o, g_ffn,
                          w_router, b_router, Tq, tiles["tm_post"])
    T = B * Tq
    moe_out = _moe(h2.reshape(T, D_MODEL), route.reshape(T, LANES), w_gu, b_gu, w_dn, b_dn)
    y = _final(x1.reshape(T, D_MODEL), moe_out, g_final, tiles["tm_post"])
    return y.reshape(B, Tq, D_MODEL)


def kernel(x_prompt, x_sample, cache_a_k, cache_a_v, cache_b_k, cache_b_v, cache_idx_k,
           meta_tokens, g_mix, w_in, lambda_q1, lambda_k1, lambda_q2, lambda_k2, subln_w,
           w_pa, w_pb, w_o, g_ffn, w_router, b_router, w_gu, b_gu, w_dn, b_dn, g_final):
    B, S, D = x_prompt.shape
    Bs, S_dec, _ = x_sample.shape
    P = cache_a_k.shape[2]
    depth = g_mix.shape[0]
    assert depth == 1
    l = 0
    lam_init = 0.8 - 0.6 * math.exp(-0.3 * l)
    w_lay = _layout_w_in(w_in[l])
    w_pb_exp = _layout_w_pb(w_pb[l])
    lam_vecs = jnp.stack([lambda_q1[l], lambda_k1[l], lambda_q2[l], lambda_k2[l]]).astype(_F32)
    weights = (lam_vecs, subln_w[l], w_pa[l], w_pb_exp, w_o[l], g_ffn[l], w_router[l], b_router[l],
               w_gu[l], b_gu[l], w_dn[l], b_dn[l], g_final)

    Lp = S + TAIL
    pad = TAIL - N_META
    meta = jnp.broadcast_to(meta_tokens[None].astype(x_prompt.dtype), (B, N_META, D))
    x_lay = jnp.concatenate([x_prompt, meta, jnp.zeros((B, pad, D), x_prompt.dtype)], axis=1)
    r = np.arange(Lp)
    pos_p = np.where(r < S, r + N_META, np.where(r < S + N_META, r - S, 0)).astype(np.int32)
    chunk_p = np.where(r < S, r // CHUNK, np.where(r < S + N_META, -1, _BIG_CHUNK)).astype(np.int64)
    order_p = np.where(r < S, r + N_META, r - S).astype(np.int32)
    tm_p = 320 if Lp % 320 == 0 else TAIL
    pp = _project(x_lay, jnp.asarray(pos_p), g_mix[l], w_lay, tm_p)
    tiles_p = dict(tq_a=min(256, S), tq_b=min(128, S), tk=min(512, S), tm_post=min(512, S))
    y_prompt = _group(x_prompt, pp, pp, chunk_p, chunk_p, order_p, min(TOPK_MAX, S // 4), S, tiles_p,
                      weights, lam_init)

    def rows_p(a, shape):
        return jnp.concatenate([a[:, S:S + N_META], a[:, :S]], axis=1).reshape((1, B, N_META + S) + shape)

    pos_s = (P + np.arange(S_dec)).astype(np.int32)
    ps = _project(x_sample, jnp.asarray(pos_s), g_mix[l], w_lay, S_dec)
    Ls = P + TAIL
    r = np.arange(Ls)
    chunk_s = np.where(r < P + S_dec, r // CHUNK, _BIG_CHUNK).astype(np.int64)
    qchunk_s = (pos_s // CHUNK).astype(np.int64)

    def with_cache(cache, new):
        c = cache.reshape(Bs, P, -1).astype(_MXU)
        z = jnp.zeros((Bs, TAIL - S_dec, c.shape[-1]), _MXU)
        return jnp.concatenate([c, new, z], axis=1)

    keys_s = dict(ka=with_cache(cache_a_k[l], ps["ka"]), va=with_cache(cache_a_v[l], ps["va"]),
                  kb=with_cache(cache_b_k[l], ps["kb"]), vb=with_cache(cache_b_v[l], ps["vb"]),
                  kir=with_cache(jnp.tile(cache_idx_k[l], (1, 1, N_IDX_HEADS)), ps["kir"]))
    tiles_s = dict(tq_a=S_dec, tq_b=S_dec, tk=min(512, P), tm_post=S_dec)
    y_sample = _group(x_sample, ps, keys_s, qchunk_s, chunk_s, r.astype(np.int32),
                      min(TOPK_MAX, (P + S_dec) // 4), S_dec, tiles_s, weights, lam_init)

    def rows_s(a, shape):
        return a.reshape((1, Bs, S_dec) + shape)

    ha, hb = (N_HEADS_A, 2 * HEAD_DIM_A), (N_KV_B, HEAD_DIM_B)
    return (y_prompt, y_sample,
            rows_p(pp["ka_f"], ha), rows_p(pp["va_f"], ha), rows_p(pp["kb_f"], hb), rows_p(pp["vb_f"], hb),
            rows_p(pp["kiwi"][:, :, :IDX_DIM], (IDX_DIM,)),
            rows_s(ps["ka_f"], ha), rows_s(ps["va_f"], ha), rows_s(ps["kb_f"], hb), rows_s(ps["vb_f"], hb),
            rows_s(ps["kiwi"][:, :, :IDX_DIM], (IDX_DIM,)))
```

```python
import functools
import math

import numpy as np
import jax
import jax.numpy as jnp
from jax import lax
from jax.experimental import pallas as pl
from jax.experimental.pallas import tpu as pltpu

D_MODEL = 1024
CHUNK = 64
N_META = 16
ROPE_THETA = 500000.0
RMS_EPS = 1e-6
N_HEADS_A = 4
HEAD_DIM_A = 64
N_HEADS_B = 8
N_KV_B = 2
HEAD_DIM_B = 64
N_IDX_HEADS = 8
IDX_DIM = 32
TOPK_MAX = 256
N_EXPERTS = 32
TOP_K_EXPERTS = 4
D_EXPERT = 1024
SWIGLU_LIMIT = 7.0
SWIGLU_ALPHA = 1.702
MOE_BLOCK = 256

QA_W = N_HEADS_A * 2 * HEAD_DIM_A
VA_W = QA_W
QB_W = N_HEADS_B * HEAD_DIM_B
KB_W = N_KV_B * HEAD_DIM_B
QI_W = N_IDX_HEADS * IDX_DIM
SPLITS = (QA_W, QA_W, VA_W, QB_W, KB_W, KB_W, QI_W, IDX_DIM, N_IDX_HEADS, D_MODEL, D_MODEL)

LANES = 128
TAIL = LANES
QB_EXP_W = N_HEADS_B * LANES

_SEC = {}
_off = 0
for _name, _w in (("qa", QA_W), ("ka", QA_W), ("va", VA_W), ("qb", QB_EXP_W), ("kb", KB_W),
                  ("vb", KB_W), ("qi", QI_W), ("kiwi", LANES), ("kir", QI_W),
                  ("ga", D_MODEL), ("gb", D_MODEL)):
    _SEC[_name] = (_off, _off + _w)
    _off += _w
W_LAYOUT = _off

_MXU = jnp.bfloat16
_F32 = jnp.float32
_NEG = -1e30
_BIG_CHUNK = 2 ** 30
_INT_MIN = -2 ** 31
_LOG2E = math.log2(math.e)
_HALF = 2 ** 15
_KEY_NEG_INF = int(np.int32(np.uint32(0xFF800000) ^ np.uint32(0x7FFFFFFF)))
_VMEM_LIMIT = 56 * 1024 * 1024


def _cparams(sem):
    return pltpu.CompilerParams(dimension_semantics=sem, vmem_limit_bytes=_VMEM_LIMIT)


def _const_spec(shape):
    nd = len(shape)
    return pl.BlockSpec(shape, lambda *_: (0,) * nd)


def _nt_dot(a, b):
    return lax.dot_general(a, b, (((1,), (1,)), ((), ())), preferred_element_type=_F32)


def _sortable(x):
    b = lax.bitcast_convert_type(x, jnp.int32)
    return b ^ ((b >> 31) & 0x7FFFFFFF)


def _rope_block(z, cos, sin_lo, sin_hi, half):
    return (z * cos + pltpu.roll(z, LANES - half, 1) * sin_lo + pltpu.roll(z, half, 1) * sin_hi)


def _proj_kernel(x_ref, g_ref, w_ref, tab_ref,
                 qa_o, kaf_o, kab_o, vaf_o, vab_o, qb_o, kbf_o, kbb_o, vbf_o, vbb_o,
                 qi_o, kiwi_o, kir_o, ga_o, gb_o):
    x = x_ref[...]
    ms = jnp.mean(x * x, axis=-1, keepdims=True)
    h = (x * lax.rsqrt(ms + RMS_EPS) * g_ref[...]).astype(w_ref.dtype)

    def sec(name):
        a, b = _SEC[name]
        return jnp.dot(h, w_ref[:, a:b], preferred_element_type=_F32)

    t64 = tuple(tab_ref[:, k * LANES:(k + 1) * LANES] for k in (0, 1, 2))
    t32 = tuple(tab_ref[:, k * LANES:(k + 1) * LANES] for k in (3, 4, 5))
    tki = tuple(tab_ref[:, k * LANES:(k + 1) * LANES] for k in (6, 7, 8))

    def roped(z, tab, half):
        return [_rope_block(z[:, c * LANES:(c + 1) * LANES], *tab, half)
                for c in range(z.shape[1] // LANES)]

    scale_a = HEAD_DIM_A ** -0.5 * _LOG2E
    scale_b = HEAD_DIM_B ** -0.5 * _LOG2E
    for c, blk in enumerate(roped(sec("qa"), t64, 8)):
        qa_o[:, c * LANES:(c + 1) * LANES] = (blk * scale_a).astype(qa_o.dtype)
    for c, blk in enumerate(roped(sec("ka"), t64, 8)):
        kaf_o[:, c * LANES:(c + 1) * LANES] = blk
        kab_o[:, c * LANES:(c + 1) * LANES] = blk.astype(kab_o.dtype)
    va = sec("va")
    vaf_o[...] = va
    vab_o[...] = va.astype(vab_o.dtype)
    for c, blk in enumerate(roped(sec("qb"), t64, 8)):
        qb_o[:, c * LANES:(c + 1) * LANES] = (blk * scale_b).astype(qb_o.dtype)
    kb = roped(sec("kb"), t64, 8)[0]
    kbf_o[...] = kb
    kbb_o[...] = kb.astype(kbb_o.dtype)
    vb = sec("vb")
    vbf_o[...] = vb
    vbb_o[...] = vb.astype(vbb_o.dtype)
    for c, blk in enumerate(roped(sec("qi"), t32, 4)):
        qi_o[:, c * LANES:(c + 1) * LANES] = blk.astype(qi_o.dtype)
    kiwi_o[...] = roped(sec("kiwi"), tki, 4)[0]
    for c, blk in enumerate(roped(sec("kir"), t32, 4)):
        kir_o[:, c * LANES:(c + 1) * LANES] = blk.astype(kir_o.dtype)
    ga_o[...] = jax.nn.sigmoid(sec("ga"))
    gb_o[...] = jax.nn.sigmoid(sec("gb"))


def _rope_tables(pos):
    def cs(d):
        rd = d // 4
        half = rd // 2
        inv_freq = ROPE_THETA ** (-jnp.arange(half, dtype=_F32) * 2.0 / rd)
        ang = pos.astype(_F32)[:, None] * inv_freq[None, :]
        return jnp.cos(ang), jnp.sin(ang), half

    T = pos.shape[0]

    def pattern(d, width):
        c, s, half = cs(d)
        one = jnp.ones((T, width - 2 * half), _F32)
        zero = lambda n: jnp.zeros((T, n), _F32)
        cos = jnp.concatenate([c, c, one], axis=1)
        lo = jnp.concatenate([-s, zero(width - half)], axis=1)
        hi = jnp.concatenate([zero(half), s, zero(width - 2 * half)], axis=1)
        return cos, lo, hi

    tabs = []
    for d, width in ((64, 64), (32, 32), (32, LANES)):
        tabs += [jnp.tile(t, (1, LANES // width)) for t in pattern(d, width)]
    return jnp.concatenate(tabs, axis=1)


def _layout_w_in(w_in):
    offs = np.cumsum((0,) + SPLITS)
    qa, ka, va, qb, kb, vb, qi, ki, wi, ga, gb = (w_in[:, offs[i]:offs[i + 1]] for i in range(11))
    D = w_in.shape[0]
    z64 = jnp.zeros((D, HEAD_DIM_B), w_in.dtype)
    per_kv = N_HEADS_B // N_KV_B
    qb_exp = []
    for hq in range(N_HEADS_B):
        col = qb[:, hq * HEAD_DIM_B:(hq + 1) * HEAD_DIM_B]
        qb_exp += [col, z64] if hq // per_kv == 0 else [z64, col]
    kiwi = jnp.concatenate([ki, wi, jnp.zeros((D, LANES - IDX_DIM - N_IDX_HEADS), w_in.dtype)], axis=1)
    kir = jnp.tile(ki, (1, N_IDX_HEADS))
    w = jnp.concatenate([qa, ka, va] + qb_exp + [kb, vb, qi, kiwi, kir, ga, gb], axis=1)
    assert w.shape[1] == W_LAYOUT
    return w.astype(_MXU)


def _layout_w_pb(w_pb):
    D = w_pb.shape[1]
    z64 = jnp.zeros((HEAD_DIM_B, D), w_pb.dtype)
    per_kv = N_HEADS_B // N_KV_B
    rows = []
    for hq in range(N_HEADS_B):
        r = w_pb[hq * HEAD_DIM_B:(hq + 1) * HEAD_DIM_B]
        rows += [r, z64] if hq // per_kv == 0 else [z64, r]
    return jnp.concatenate(rows, axis=0).astype(_MXU)


def _project(x, pos, g_mix, w_lay, tm):
    B, T, D = x.shape
    assert T % tm == 0
    tab = _rope_tables(pos)
    row = lambda w: pl.BlockSpec((None, tm, w), lambda b, i: (b, i, 0))
    outs = [("qa", QA_W, _MXU), ("ka_f", QA_W, _F32), ("ka", QA_W, _MXU), ("va_f", VA_W, _F32),
            ("va", VA_W, _MXU), ("qb", QB_EXP_W, _MXU), ("kb_f", KB_W, _F32), ("kb", KB_W, _MXU),
            ("vb_f", KB_W, _F32), ("vb", KB_W, _MXU), ("qi", QI_W, _MXU), ("kiwi", LANES, _F32),
            ("kir", QI_W, _MXU), ("ga", D_MODEL, _F32), ("gb", D_MODEL, _F32)]
    res = pl.pallas_call(
        _proj_kernel,
        grid=(B, T // tm),
        in_specs=[row(D), _const_spec((1, D)),
                  pl.BlockSpec((D, W_LAYOUT), lambda b, i: (0, 0), pipeline_mode=pl.Buffered(1)),
                  pl.BlockSpec((tm, 9 * LANES), lambda b, i: (i, 0))],
        out_specs=[row(w) for _, w, _ in outs],
        out_shape=[jax.ShapeDtypeStruct((B, T, w), dt) for _, w, dt in outs],
        compiler_params=_cparams(("parallel", "parallel")),
        name="proj",
    )(x, g_mix.reshape(1, D), w_lay, tab)
    return {n: r for (n, _, _), r in zip(outs, res)}


def _tile_lanes(v, n):
    return v if n == 1 else jnp.concatenate([v] * n, axis=1)


def _softmax_step(s, v, m_sc, l_sc, acc_sc):
    m_prev = m_sc[...]
    m_new = jnp.maximum(m_prev, jnp.max(s, axis=1, keepdims=True))
    alpha = jnp.exp2(m_prev - m_new)
    p = jnp.exp2(s - _tile_lanes(m_new, s.shape[1] // LANES))
    l_sc[...] = alpha * l_sc[...] + jnp.sum(p, axis=1, keepdims=True)
    acc_sc[...] = alpha * acc_sc[...] + jnp.dot(p.astype(v.dtype), v, preferred_element_type=_F32)
    m_sc[...] = m_new


def _attend_tiles(n, scores, values, s0_sc, s1_sc, m_sc, l_sc, acc_sc):
    s0_sc[...] = scores(0)

    def pair(p, carry):
        j = 2 * p
        s1_sc[...] = scores(j + 1)
        _softmax_step(s0_sc[...], values(j), m_sc, l_sc, acc_sc)
        s0_sc[...] = scores(jnp.minimum(j + 2, n - 1))
        _softmax_step(s1_sc[...], values(j + 1), m_sc, l_sc, acc_sc)
        return carry

    lax.fori_loop(0, n // 2, pair, 0)

    @pl.when(n % 2 == 1)
    def _():
        _softmax_step(s0_sc[...], values(n - 1), m_sc, l_sc, acc_sc)


def _num_full_tiles(qchunk, kchunk_main, TQ, TK):
    last = kchunk_main.reshape(-1, TK).max(axis=1)
    qmin = qchunk.reshape(-1, TQ).min(axis=1)
    full = last[None, :] <= qmin[:, None]
    return np.cumprod(full, axis=1).sum(axis=1).astype(np.int32)


def _num_key_tiles(qchunk, kchunk_main, TQ, TK):
    first = kchunk_main.reshape(-1, TK)[:, 0]
    qmax = qchunk.reshape(-1, TQ).max(axis=1)
    return (first[None, :] <= qmax[:, None]).sum(axis=1).astype(np.int32)


def _diff_attn_kernel(nk_ref, nf_ref, q_ref, k_ref, v_ref, qc_ref, kcm_ref, kct_ref, lam_ref, sub_ref,
                      o_ref, s0_sc, s1_sc, m_sc, l_sc, acc_sc, *, TQ, TK, Lmain, lam_init):
    i = pl.program_id(2)
    q = q_ref[...]
    lane = lax.broadcasted_iota(jnp.int32, q.shape, 1)
    zero = jnp.zeros_like(q)
    qs = jnp.concatenate([jnp.where(lane < HEAD_DIM_A, q, zero),
                          jnp.where(lane >= HEAD_DIM_A, q, zero)], axis=0)
    qc = qc_ref[...]
    qc2 = jnp.concatenate([qc, qc], axis=0)
    m_sc[...] = jnp.full(m_sc.shape, _NEG, _F32)
    l_sc[...] = jnp.zeros(l_sc.shape, _F32)
    acc_sc[...] = jnp.zeros(acc_sc.shape, _F32)

    def keys(j):
        return k_ref[pl.ds(pl.multiple_of(j * TK, TK), TK), :]

    def values(j):
        return v_ref[pl.ds(pl.multiple_of(j * TK, TK), TK), :]

    def masked_scores(k, kc):
        return jnp.where(kc <= qc2, _nt_dot(qs, k), _NEG)

    _attend_tiles(nf_ref[i], lambda j: _nt_dot(qs, keys(j)), values,
                  s0_sc, s1_sc, m_sc, l_sc, acc_sc)

    def diagonal(j, carry):
        _softmax_step(masked_scores(keys(j), kcm_ref[j]), values(j), m_sc, l_sc, acc_sc)
        return carry

    lax.fori_loop(nf_ref[i], nk_ref[i], diagonal, 0)
    _softmax_step(masked_scores(k_ref[Lmain:Lmain + TAIL, :], kct_ref[...]),
                  v_ref[Lmain:Lmain + TAIL, :], m_sc, l_sc, acc_sc)

    lam = (jnp.exp(jnp.sum(lam_ref[0:1, :] * lam_ref[1:2, :], axis=1, keepdims=True))
           - jnp.exp(jnp.sum(lam_ref[2:3, :] * lam_ref[3:4, :], axis=1, keepdims=True))
           + lam_init)
    o = acc_sc[...] / l_sc[...]
    d = o[:TQ] - lam * o[TQ:]
    ms = jnp.mean(d * d, axis=-1, keepdims=True)
    o_ref[...] = ((d * lax.rsqrt(ms + RMS_EPS) * sub_ref[...]) * (1.0 - lam_init)).astype(o_ref.dtype)


def _diff_attn(q, k, v, qchunk, kchunk, lam_vecs, subln_w, lam_init, Tq, TQ, TK):
    B = q.shape[0]
    L = k.shape[1]
    Lmain = L - TAIL
    assert Tq % TQ == 0 and Lmain % TK == 0
    nq, nt = Tq // TQ, Lmain // TK
    nk = jnp.asarray(_num_key_tiles(qchunk[:Tq], kchunk[:Lmain], TQ, TK))
    nf = jnp.asarray(_num_full_tiles(qchunk[:Tq], kchunk[:Lmain], TQ, TK))
    qc = jnp.asarray(qchunk[:Tq].reshape(Tq, 1).astype(np.int32))
    kcm = jnp.asarray(kchunk[:Lmain].reshape(nt, 1, TK).astype(np.int32))
    kct = jnp.asarray(kchunk[Lmain:].reshape(1, TAIL).astype(np.int32))
    R = 2 * TQ
    kern = functools.partial(_diff_attn_kernel, TQ=TQ, TK=TK, Lmain=Lmain, lam_init=lam_init)
    grid_spec = pltpu.PrefetchScalarGridSpec(
        num_scalar_prefetch=2,
        grid=(B, N_HEADS_A, nq),
        in_specs=[pl.BlockSpec((None, TQ, LANES), lambda b, h, i, *_: (b, i, h)),
                  pl.BlockSpec((None, L, LANES), lambda b, h, i, *_: (b, 0, h)),
                  pl.BlockSpec((None, L, LANES), lambda b, h, i, *_: (b, 0, h)),
                  pl.BlockSpec((TQ, 1), lambda b, h, i, *_: (i, 0)),
                  pl.BlockSpec((nt, 1, TK), lambda b, h, i, *_: (0, 0, 0)),
                  pl.BlockSpec((1, TAIL), lambda b, h, i, *_: (0, 0)),
                  pl.BlockSpec((4, HEAD_DIM_A), lambda b, h, i, *_: (0, 0)),
                  pl.BlockSpec((1, LANES), lambda b, h, i, *_: (0, 0))],
        out_specs=pl.BlockSpec((None, TQ, LANES), lambda b, h, i, *_: (b, i, h)),
        scratch_shapes=[pltpu.VMEM((R, TK), _F32)] * 2 + [pltpu.VMEM((R, LANES), _F32)] * 3,
    )
    return pl.pallas_call(
        kern, grid_spec=grid_spec,
        out_shape=jax.ShapeDtypeStruct((B, Tq, VA_W), _MXU),
        compiler_params=_cparams(("parallel", "parallel", "parallel")),
        name="diff_attn",
    )(nk, nf, q, k, v, qc, kcm, kct, lam_vecs, subln_w.reshape(1, LANES))


def _dsa_kernel(nk_ref, qi_ref, kiwi_ref, qb_ref, kir_ref, kb_ref, vb_ref, qc_ref,
                kcm_ref, kct_ref, kom_ref, kot_ref, o_ref,
                key_sc, keyt_sc, hi_sc, lo_sc, hit_sc, lot_sc, s0_sc, s1_sc, m_sc, l_sc, acc_sc,
                *, TQ, TK, Lmain, topk):
    i = pl.program_id(1)
    nk = nk_ref[i]
    H = N_IDX_HEADS
    nblk = TK // LANES

    qi = qi_ref[...]
    head = lax.broadcasted_iota(jnp.int32, qi.shape, 1) // IDX_DIM
    zero = jnp.zeros_like(qi)
    qis = jnp.concatenate([jnp.where(head == h, qi, zero) for h in range(H)], axis=0)
    wi = kiwi_ref[:, IDX_DIM:IDX_DIM + H] * (H ** -0.5) * (IDX_DIM ** -0.5)
    wi_b = [jnp.broadcast_to(wi[:, h:h + 1], (TQ, LANES)) for h in range(H)]
    qc = qc_ref[...]

    def score_keys(kir, kc):
        lg = _nt_dot(qis, kir)
        n = lg.shape[1] // LANES
        sc = _tile_lanes(wi_b[0], n) * jnp.maximum(lg[0:TQ], 0.0)
        for h in range(1, H):
            sc = sc + _tile_lanes(wi_b[h], n) * jnp.maximum(lg[h * TQ:(h + 1) * TQ], 0.0)
        return _sortable(jnp.where(kc <= qc, sc, -jnp.inf))

    def split16(key):
        return (key >> 16).astype(jnp.int16), ((key & 0xFFFF) - _HALF).astype(jnp.int16)

    def fill(j, carry):
        off = pl.multiple_of(j * TK, TK)
        key = score_keys(kir_ref[pl.ds(off, TK), :], kcm_ref[j])
        key_sc[j] = key
        hi_sc[j], lo_sc[j] = split16(key)
        return carry

    lax.fori_loop(0, nk, fill, 0)
    key_t = score_keys(kir_ref[Lmain:Lmain + TAIL, :], kct_ref[...])
    keyt_sc[...] = key_t
    hit_sc[...], lot_sc[...] = split16(key_t)

    def count(ind_main, ind_tail, dtype):
        def body(j, c):
            ind = ind_main(j)
            for b in range(nblk):
                c = c + ind[:, b * LANES:(b + 1) * LANES]
            return c
        c = lax.fori_loop(0, nk, body, jnp.zeros((TQ, LANES), dtype)) + ind_tail()
        return jnp.sum(c.astype(_F32), axis=1, keepdims=True)

    one16, zero16 = jnp.int16(1), jnp.int16(0)

    def lanes16(v):
        return jnp.broadcast_to(v, (TQ, LANES)).astype(jnp.int16)

    def count16(main_sc, tail_sc, t16, strict=False):
        tm = _tile_lanes(t16, nblk)
        cmp = (lambda a, b: a > b) if strict else (lambda a, b: a >= b)
        return count(lambda j: jnp.where(cmp(main_sc[j], tm), one16, zero16),
                     lambda: jnp.where(cmp(tail_sc[...], t16), one16, zero16), jnp.int16)

    def kth_half(main_sc, tail_sc, target):
        def step(it, t_u):
            cand_u = t_u | jnp.left_shift(jnp.int32(1), 15 - it)
            cnt = count16(main_sc, tail_sc, lanes16(cand_u - _HALF))
            return jnp.where(cnt >= target, cand_u, t_u)
        return lax.fori_loop(0, 16, step, jnp.zeros((TQ, 1), jnp.int32))

    kf = float(topk)
    t_hi = kth_half(hi_sc, hit_sc, kf) - _HALF
    t_hi16 = lanes16(t_hi)
    n_above = count16(hi_sc, hit_sc, t_hi16, strict=True)
    t_hi16m = _tile_lanes(t_hi16, nblk)
    low16 = jnp.int16(-_HALF)

    def bucket(j, carry):
        lo_sc[j] = jnp.where(hi_sc[j] == t_hi16m, lo_sc[j], low16)
        return carry

    lax.fori_loop(0, nk, bucket, 0)
    lot_sc[...] = jnp.where(hit_sc[...] == t_hi16, lot_sc[...], low16)
    t_lo = kth_half(lo_sc, lot_sc, kf - n_above)
    n_ge = n_above + count16(lo_sc, lot_sc, lanes16(t_lo - _HALF))
    kth = (t_hi << 16) | t_lo
    thr1 = jnp.maximum(kth, _KEY_NEG_INF + 1)
    thr = jnp.broadcast_to(thr1, (TQ, LANES))
    thr_m = _tile_lanes(thr, nblk)
    excess = jnp.broadcast_to((n_ge > kf) & (kth > _KEY_NEG_INF), (TQ, LANES))

    @pl.when(jnp.max(jnp.where(excess, 1.0, 0.0)) > 0.0)
    def _():
        def count32(pred_main, pred_tail):
            return count(lambda j: jnp.where(pred_main(j), 1.0, 0.0),
                         lambda: jnp.where(pred_tail(), 1.0, 0.0), _F32)

        need = kf - count32(lambda j: key_sc[j] > thr_m, lambda: keyt_sc[...] > thr)
        excess_m = _tile_lanes(excess, nblk)

        def tied_before(J):
            Jm = _tile_lanes(J, nblk)
            return count32(lambda j: (key_sc[j] == thr_m) & (kom_ref[j] < Jm),
                           lambda: (keyt_sc[...] == thr) & (kot_ref[...] < J))

        def search(it, J):
            cand = J | jnp.left_shift(jnp.int32(1), 13 - it)
            return jnp.where(tied_before(cand) < need, cand, J)

        J = lax.fori_loop(0, 14, search, jnp.zeros((TQ, LANES), jnp.int32))
        Jm = _tile_lanes(J, nblk)

        def drop(j, carry):
            kj = key_sc[j]
            key_sc[j] = jnp.where(excess_m & (kj == thr_m) & (kom_ref[j] > Jm), _KEY_NEG_INF, kj)
            return carry

        lax.fori_loop(0, nk, drop, 0)
        kt = keyt_sc[...]
        keyt_sc[...] = jnp.where(excess & (kt == thr) & (kot_ref[...] > J), _KEY_NEG_INF, kt)

    qs = jnp.concatenate([qb_ref[:, h * LANES:(h + 1) * LANES] for h in range(N_HEADS_B)], axis=0)
    m_sc[...] = jnp.full(m_sc.shape, _NEG, _F32)
    l_sc[...] = jnp.zeros(l_sc.shape, _F32)
    acc_sc[...] = jnp.zeros(acc_sc.shape, _F32)

    def masked_scores(keys, kb):
        bias = jnp.where(keys >= _tile_lanes(thr, keys.shape[1] // LANES), 0.0, _NEG)
        return _nt_dot(qs, kb) + jnp.concatenate([bias] * N_HEADS_B, axis=0)

    def scores(j):
        return masked_scores(key_sc[j], kb_ref[pl.ds(pl.multiple_of(j * TK, TK), TK), :])

    def values(j):
        return vb_ref[pl.ds(pl.multiple_of(j * TK, TK), TK), :]

    _attend_tiles(nk, scores, values, s0_sc, s1_sc, m_sc, l_sc, acc_sc)
    _softmax_step(masked_scores(keyt_sc[...], kb_ref[Lmain:Lmain + TAIL, :]),
                  vb_ref[Lmain:Lmain + TAIL, :], m_sc, l_sc, acc_sc)
    o = acc_sc[...] / l_sc[...]
    for h in range(N_HEADS_B):
        o_ref[:, h * LANES:(h + 1) * LANES] = o[h * TQ:(h + 1) * TQ].astype(o_ref.dtype)


def _dsa(qi, kiwi, qb, kir, kb, vb, qchunk, kchunk, korder, topk, Tq, TQ, TK):
    B = qi.shape[0]
    L = kir.shape[1]
    Lmain = L - TAIL
    assert Tq % TQ == 0 and Lmain % TK == 0
    nq, nt = Tq // TQ, Lmain // TK
    nk = jnp.asarray(_num_key_tiles(qchunk[:Tq], kchunk[:Lmain], TQ, TK))
    qc = jnp.asarray(qchunk[:Tq].reshape(Tq, 1).astype(np.int32))
    main = lambda a: jnp.asarray(a[:Lmain].reshape(nt, 1, TK).astype(np.int32))
    tail = lambda a: jnp.asarray(a[Lmain:].reshape(1, TAIL).astype(np.int32))
    R = N_HEADS_B * TQ
    kern = functools.partial(_dsa_kernel, TQ=TQ, TK=TK, Lmain=Lmain, topk=topk)
    qrow = lambda w: pl.BlockSpec((None, TQ, w), lambda b, i, nk: (b, i, 0))
    krow = lambda w: pl.BlockSpec((None, L, w), lambda b, i, nk: (b, 0, 0),
                                  pipeline_mode=pl.Buffered(1))
    cm = pl.BlockSpec((nt, 1, TK), lambda b, i, nk: (0, 0, 0))
    ct = pl.BlockSpec((1, TAIL), lambda b, i, nk: (0, 0))
    grid_spec = pltpu.PrefetchScalarGridSpec(
        num_scalar_prefetch=1,
        grid=(B, nq),
        in_specs=[qrow(QI_W), qrow(LANES), qrow(QB_EXP_W), krow(QI_W), krow(KB_W), krow(KB_W),
                  pl.BlockSpec((TQ, 1), lambda b, i, nk: (i, 0)), cm, ct, cm, ct],
        out_specs=qrow(QB_EXP_W),
        scratch_shapes=[pltpu.VMEM((nt, TQ, TK), jnp.int32), pltpu.VMEM((TQ, TAIL), jnp.int32),
                        pltpu.VMEM((nt, TQ, TK), jnp.int16), pltpu.VMEM((nt, TQ, TK), jnp.int16),
                        pltpu.VMEM((TQ, TAIL), jnp.int16), pltpu.VMEM((TQ, TAIL), jnp.int16),
                        pltpu.VMEM((R, TK), _F32), pltpu.VMEM((R, TK), _F32),
                        pltpu.VMEM((R, LANES), _F32), pltpu.VMEM((R, LANES), _F32),
                        pltpu.VMEM((R, LANES), _F32)],
    )
    return pl.pallas_call(
        kern, grid_spec=grid_spec,
        out_shape=jax.ShapeDtypeStruct((B, Tq, QB_EXP_W), _MXU),
        compiler_params=_cparams(("parallel", "parallel")),
        name="dsa",
    )(nk, qi, kiwi, qb, kir, kb, vb, qc, main(kchunk), tail(kchunk), main(korder), tail(korder))


def _post_kernel(x_ref, oa_ref, ob_ref, ga_ref, gb_ref, wpa_ref, wpb_ref, wo_ref, g_ref,
                 wr_ref, br_ref, x1_o, h2_o, route_o):
    pa = jnp.dot(oa_ref[...], wpa_ref[...], preferred_element_type=_F32)
    pb = jnp.dot(ob_ref[...], wpb_ref[...], preferred_element_type=_F32)
    merged = ga_ref[...] * pa + gb_ref[...] * pb
    x1 = x_ref[...] + jnp.dot(merged.astype(wo_ref.dtype), wo_ref[...], preferred_element_type=_F32)
    x1_o[...] = x1
    ms = jnp.mean(x1 * x1, axis=-1, keepdims=True)
    h2 = (x1 * lax.rsqrt(ms + RMS_EPS) * g_ref[...]).astype(h2_o.dtype)
    h2_o[...] = h2
    cur = jnp.dot(h2, wr_ref[...], preferred_element_type=_F32) + br_ref[...]
    lane = lax.broadcasted_iota(jnp.int32, cur.shape, 1).astype(_F32)
    vals, idxs = [], []
    for _ in range(TOP_K_EXPERTS):
        m = jnp.max(cur, axis=1, keepdims=True)
        idx = jnp.min(jnp.where(cur == m, lane, float(LANES)), axis=1, keepdims=True)
        vals.append(m)
        idxs.append(idx)
        cur = jnp.where(lane == idx, -jnp.inf, cur)
    es = [jnp.exp(v - vals[0]) for v in vals]
    denom = es[0] + es[1] + es[2] + es[3]
    route = jnp.zeros(cur.shape, _F32)
    for k in range(TOP_K_EXPERTS):
        route = jnp.where(lane == float(k), idxs[k], route)
        route = jnp.where(lane == float(TOP_K_EXPERTS + k), es[k] / denom, route)
    route_o[...] = route


def _post(x, oa, ob, ga, gb, w_pa, w_pb_exp, w_o, g_ffn, w_router, b_router, T, tm):
    B, _, D = x.shape
    assert T % tm == 0
    row = lambda w: pl.BlockSpec((None, tm, w), lambda b, i: (b, i, 0))
    wr = jnp.concatenate([w_router, jnp.zeros((D, LANES - N_EXPERTS), w_router.dtype)], axis=1).astype(_MXU)
    br = jnp.concatenate([b_router.astype(_F32), jnp.full((LANES - N_EXPERTS,), _NEG, _F32)]).reshape(1, LANES)
    return pl.pallas_call(
        _post_kernel,
        grid=(B, T // tm),
        in_specs=[row(D), row(VA_W), row(QB_EXP_W), row(D), row(D),
                  _const_spec((VA_W, D)), _const_spec((QB_EXP_W, D)), _const_spec((D, D)),
                  _const_spec((1, D)), _const_spec((D, LANES)), _const_spec((1, LANES))],
        out_specs=[row(D), row(D), row(LANES)],
        out_shape=[jax.ShapeDtypeStruct((B, T, D), _F32), jax.ShapeDtypeStruct((B, T, D), _MXU),
                   jax.ShapeDtypeStruct((B, T, LANES), _F32)],
        compiler_params=_cparams(("parallel", "parallel")),
        name="post",
    )(x, oa, ob, ga, gb, w_pa.astype(_MXU), w_pb_exp, w_o.astype(_MXU), g_ffn.reshape(1, D), wr, br)


def _moe_kernel(be_ref, nu_ref, x_ref, wgu_ref, bgu_ref, wdn_ref, bdn_ref, y_ref, wgu_sc, wdn_sc):
    i = pl.program_id(0)
    used = i < nu_ref[0]
    new_expert = (i == 0) | (be_ref[i] != be_ref[jnp.maximum(i - 1, 0)])

    @pl.when(used & new_expert)
    def _():
        wgu_sc[...] = wgu_ref[...].astype(wgu_sc.dtype)
        wdn_sc[...] = wdn_ref[...].astype(wdn_sc.dtype)

    @pl.when(used)
    def _():
        gu = jnp.dot(x_ref[...], wgu_sc[...], preferred_element_type=_F32) + bgu_ref[...]
        glu = jnp.minimum(gu[:, :D_EXPERT], SWIGLU_LIMIT)
        lin = jnp.clip(gu[:, D_EXPERT:], -SWIGLU_LIMIT, SWIGLU_LIMIT)
        act = glu * jax.nn.sigmoid(SWIGLU_ALPHA * glu) * (lin + 1.0)
        y_ref[...] = jnp.dot(act.astype(wdn_sc.dtype), wdn_sc[...],
                             preferred_element_type=_F32) + bdn_ref[...]

    @pl.when(jnp.logical_not(used))
    def _():
        y_ref[...] = jnp.zeros(y_ref.shape, y_ref.dtype)


def _lookup(idx, table):
    n = table.shape[0]
    hit = idx[:, None] == jnp.arange(n, dtype=idx.dtype)[None, :]
    return jnp.sum(jnp.where(hit, table[None, :], 0), axis=1)


def _moe(h2, route, w_gu, b_gu, w_dn, b_dn):
    T, D = h2.shape
    K = TOP_K_EXPERTS
    A = T * K
    NB = -(-A // MOE_BLOCK) + N_EXPERTS
    e_flat = route[:, :K].astype(jnp.int32).reshape(-1)
    e_sorted, order = lax.sort((e_flat, jnp.arange(A, dtype=jnp.int32)), num_keys=1, is_stable=True)
    experts = jnp.arange(N_EXPERTS, dtype=jnp.int32)
    counts = jnp.sum((e_flat[:, None] == experts[None, :]).astype(jnp.int32), axis=0)
    blocks_per = (counts + MOE_BLOCK - 1) // MOE_BLOCK
    padded = blocks_per * MOE_BLOCK
    pstart = jnp.cumsum(padded) - padded
    start = jnp.cumsum(counts) - counts
    cum_blocks = jnp.cumsum(blocks_per)
    blk = jnp.arange(NB, dtype=jnp.int32)
    block_expert = jnp.minimum(jnp.sum((cum_blocks[None, :] <= blk[:, None]).astype(jnp.int32), axis=1),
                               N_EXPERTS - 1)
    n_used = cum_blocks[-1:].astype(jnp.int32)
    tok_sorted = jnp.concatenate([order // K, jnp.full((MOE_BLOCK,), T, jnp.int32)])
    j0 = blk * MOE_BLOCK - _lookup(block_expert, pstart)
    win_start = jnp.clip(_lookup(block_expert, start) + j0, 0, A)
    window = jax.vmap(lambda o: lax.dynamic_slice(tok_sorted, (o,), (MOE_BLOCK,)))(win_start)
    j = j0[:, None] + jnp.arange(MOE_BLOCK, dtype=jnp.int32)[None, :]
    valid = (j < _lookup(block_expert, counts)[:, None]) & (blk < n_used[0])[:, None]
    slot_tok = jnp.where(valid, window, T).reshape(-1)
    x_pad = jnp.concatenate([h2, jnp.zeros((1, D), h2.dtype)], axis=0)
    x_slots = x_pad[slot_tok]
    y_slots = pl.pallas_call(
        _moe_kernel,
        grid_spec=pltpu.PrefetchScalarGridSpec(
            num_scalar_prefetch=2,
            grid=(NB,),
            in_specs=[pl.BlockSpec((MOE_BLOCK, D), lambda i, be, nu: (i, 0)),
                      pl.BlockSpec((None, D, 2 * D_EXPERT), lambda i, be, nu: (be[i], 0, 0)),
                      pl.BlockSpec((None, 1, 2 * D_EXPERT), lambda i, be, nu: (be[i], 0, 0)),
                      pl.BlockSpec((None, D_EXPERT, D), lambda i, be, nu: (be[i], 0, 0)),
                      pl.BlockSpec((None, 1, D), lambda i, be, nu: (be[i], 0, 0))],
            out_specs=pl.BlockSpec((MOE_BLOCK, D), lambda i, be, nu: (i, 0)),
            scratch_shapes=[pltpu.VMEM((D, 2 * D_EXPERT), _MXU), pltpu.VMEM((D_EXPERT, D), _MXU)],
        ),
        out_shape=jax.ShapeDtypeStruct((NB * MOE_BLOCK, D), _F32),
        compiler_params=_cparams(("arbitrary",)),
        name="moe",
    )(block_expert, n_used, x_slots, w_gu, b_gu.reshape(N_EXPERTS, 1, -1).astype(_F32),
      w_dn, b_dn.reshape(N_EXPERTS, 1, -1).astype(_F32))
    dest_sorted = _lookup(e_sorted, pstart - start) + jnp.arange(A, dtype=jnp.int32)
    _, dest = lax.sort((order, dest_sorted), num_keys=1)
    dest = dest.reshape(T, K)
    return [y_slots[dest[:, k]] for k in range(K)]


def _final_kernel(x_ref, y0_ref, y1_ref, y2_ref, y3_ref, route_ref, g_ref, o_ref):
    K = TOP_K_EXPERTS
    x = x_ref[...]
    for k, y_ref in enumerate((y0_ref, y1_ref, y2_ref, y3_ref)):
        x = x + route_ref[:, K + k:K + k + 1] * y_ref[...]
    ms = jnp.mean(x * x, axis=-1, keepdims=True)
    o_ref[...] = x * lax.rsqrt(ms + RMS_EPS) * g_ref[...]


def _final(x1, ys, route, g_final, tm):
    T, D = x1.shape
    assert T % tm == 0
    row = pl.BlockSpec((tm, D), lambda i: (i, 0))
    return pl.pallas_call(
        _final_kernel, grid=(T // tm,),
        in_specs=[row] * 5 + [pl.BlockSpec((tm, LANES), lambda i: (i, 0)), _const_spec((1, D))],
        out_specs=row,
        out_shape=jax.ShapeDtypeStruct((T, D), _F32),
        compiler_params=_cparams(("parallel",)),
        name="final",
    )(x1, *ys, route, g_final.reshape(1, D))


def _group(x_rows, proj, keys, qchunk, kchunk, korder, topk, Tq, tiles, weights, lam_init):
    (lam_vecs, subln_w, w_pa, w_pb_exp, w_o, g_ffn, w_router, b_router,
     w_gu, b_gu, w_dn, b_dn, g_final) = weights
    B = x_rows.shape[0]
    oa = _diff_attn(proj["qa"], keys["ka"], keys["va"], qchunk, kchunk, lam_vecs, subln_w,
                    lam_init, Tq, tiles["tq_a"], tiles["tk"])
    ob = _dsa(proj["qi"], proj["kiwi"], proj["qb"], keys["kir"], keys["kb"], keys["vb"],
              qchunk, kchunk, korder, topk, Tq, tiles["tq_b"], tiles["tk"])
    x1, h2, route = _post(x_rows, oa, ob, proj["ga"], proj["gb"], w_pa, w_pb_exp, w_o, g_ffn,
                          w_router, b_router, Tq, tiles["tm_post"])
    T = B * Tq
    route = route.reshape(T, LANES)
    ys = _moe(h2.reshape(T, D_MODEL), route, w_gu, b_gu, w_dn, b_dn)
    y = _final(x1.reshape(T, D_MODEL), ys, route, g_final, tiles["tm_post"])
    return y.reshape(B, Tq, D_MODEL)


def kernel(x_prompt, x_sample, cache_a_k, cache_a_v, cache_b_k, cache_b_v, cache_idx_k,
           meta_tokens, g_mix, w_in, lambda_q1, lambda_k1, lambda_q2, lambda_k2, subln_w,
           w_pa, w_pb, w_o, g_ffn, w_router, b_router, w_gu, b_gu, w_dn, b_dn, g_final):
    B, S, D = x_prompt.shape
    Bs, S_dec, _ = x_sample.shape
    P = cache_a_k.shape[2]
    depth = g_mix.shape[0]
    assert depth == 1
    l = 0
    lam_init = 0.8 - 0.6 * math.exp(-0.3 * l)
    w_lay = _layout_w_in(w_in[l])
    w_pb_exp = _layout_w_pb(w_pb[l])
    lam_vecs = jnp.stack([lambda_q1[l], lambda_k1[l], lambda_q2[l], lambda_k2[l]]).astype(_F32)
    weights = (lam_vecs, subln_w[l], w_pa[l], w_pb_exp, w_o[l], g_ffn[l], w_router[l], b_router[l],
               w_gu[l], b_gu[l], w_dn[l], b_dn[l], g_final)

    Lp = S + TAIL
    pad = TAIL - N_META
    meta = jnp.broadcast_to(meta_tokens[None].astype(x_prompt.dtype), (B, N_META, D))
    x_lay = jnp.concatenate([x_prompt, meta, jnp.zeros((B, pad, D), x_prompt.dtype)], axis=1)
    r = np.arange(Lp)
    pos_p = np.where(r < S, r + N_META, np.where(r < S + N_META, r - S, 0)).astype(np.int32)
    chunk_p = np.where(r < S, r // CHUNK, np.where(r < S + N_META, -1, _BIG_CHUNK)).astype(np.int64)
    order_p = np.where(r < S, r + N_META, r - S).astype(np.int32)
    tm_p = 320 if Lp % 320 == 0 else TAIL
    pp = _project(x_lay, jnp.asarray(pos_p), g_mix[l], w_lay, tm_p)
    tiles_p = dict(tq_a=min(512, S), tq_b=min(128, S), tk=min(512, S), tm_post=min(512, S))
    y_prompt = _group(x_prompt, pp, pp, chunk_p, chunk_p, order_p, min(TOPK_MAX, S // 4), S, tiles_p,
                      weights, lam_init)

    def rows_p(a, shape):
        return jnp.concatenate([a[:, S:S + N_META], a[:, :S]], axis=1).reshape((1, B, N_META + S) + shape)

    pos_s = (P + np.arange(S_dec)).astype(np.int32)
    ps = _project(x_sample, jnp.asarray(pos_s), g_mix[l], w_lay, S_dec)
    Ls = P + TAIL
    r = np.arange(Ls)
    chunk_s = np.where(r < P + S_dec, r // CHUNK, _BIG_CHUNK).astype(np.int64)
    qchunk_s = (pos_s // CHUNK).astype(np.int64)

    def with_cache(cache, new):
        c = cache.reshape(Bs, P, -1).astype(_MXU)
        z = jnp.zeros((Bs, TAIL - S_dec, c.shape[-1]), _MXU)
        return jnp.concatenate([c, new, z], axis=1)

    keys_s = dict(ka=with_cache(cache_a_k[l], ps["ka"]), va=with_cache(cache_a_v[l], ps["va"]),
                  kb=with_cache(cache_b_k[l], ps["kb"]), vb=with_cache(cache_b_v[l], ps["vb"]),
                  kir=with_cache(jnp.tile(cache_idx_k[l], (1, 1, N_IDX_HEADS)), ps["kir"]))
    tiles_s = dict(tq_a=S_dec, tq_b=S_dec, tk=min(512, P), tm_post=S_dec)
    y_sample = _group(x_sample, ps, keys_s, qchunk_s, chunk_s, r.astype(np.int32),
                      min(TOPK_MAX, (P + S_dec) // 4), S_dec, tiles_s, weights, lam_init)

    def rows_s(a, shape):
        return a.reshape((1, Bs, S_dec) + shape)

    ha, hb = (N_HEADS_A, 2 * HEAD_DIM_A), (N_KV_B, HEAD_DIM_B)
    return (y_prompt, y_sample,
            rows_p(pp["ka_f"], ha), rows_p(pp["va_f"], ha), rows_p(pp["kb_f"], hb), rows_p(pp["vb_f"], hb),
            rows_p(pp["kiwi"][:, :, :IDX_DIM], (IDX_DIM,)),
            rows_s(ps["ka_f"], ha), rows_s(ps["va_f"], ha), rows_s(ps["kb_f"], hb), rows_s(ps["vb_f"], hb),
            rows_s(ps["kiwi"][:, :, :IDX_DIM], (IDX_DIM,)))
```

```python
import functools
import math

import numpy as np
import jax
import jax.numpy as jnp
from jax import lax
from jax.experimental import pallas as pl
from jax.experimental.pallas import tpu as pltpu

D_MODEL = 1024
CHUNK = 64
N_META = 16
ROPE_THETA = 500000.0
RMS_EPS = 1e-6
N_HEADS_A = 4
HEAD_DIM_A = 64
N_HEADS_B = 8
N_KV_B = 2
HEAD_DIM_B = 64
N_IDX_HEADS = 8
IDX_DIM = 32
TOPK_MAX = 256
N_EXPERTS = 32
TOP_K_EXPERTS = 4
D_EXPERT = 1024
SWIGLU_LIMIT = 7.0
SWIGLU_ALPHA = 1.702
MOE_BLOCK = 256

QA_W = N_HEADS_A * 2 * HEAD_DIM_A
VA_W = QA_W
QB_W = N_HEADS_B * HEAD_DIM_B
KB_W = N_KV_B * HEAD_DIM_B
QI_W = N_IDX_HEADS * IDX_DIM
SPLITS = (QA_W, QA_W, VA_W, QB_W, KB_W, KB_W, QI_W, IDX_DIM, N_IDX_HEADS, D_MODEL, D_MODEL)

LANES = 128
TAIL = LANES
QB_EXP_W = N_HEADS_B * LANES

_SEC = {}
_off = 0
for _name, _w in (("qa", QA_W), ("ka", QA_W), ("va", VA_W), ("qb", QB_EXP_W), ("kb", KB_W),
                  ("vb", KB_W), ("qi", QI_W), ("kiwi", LANES), ("kir", QI_W),
                  ("ga", D_MODEL), ("gb", D_MODEL)):
    _SEC[_name] = (_off, _off + _w)
    _off += _w
W_LAYOUT = _off

_MXU = jnp.bfloat16
_F32 = jnp.float32
_NEG = -1e30
_BIG_CHUNK = 2 ** 30
_INT_MIN = -2 ** 31
_LOG2E = math.log2(math.e)
_HALF = 2 ** 15
_KEY_NEG_INF = int(np.int32(np.uint32(0xFF800000) ^ np.uint32(0x7FFFFFFF)))
_VMEM_LIMIT = 56 * 1024 * 1024


def _cparams(sem):
    return pltpu.CompilerParams(dimension_semantics=sem, vmem_limit_bytes=_VMEM_LIMIT)


def _const_spec(shape):
    nd = len(shape)
    return pl.BlockSpec(shape, lambda *_: (0,) * nd)


def _nt_dot(a, b):
    return lax.dot_general(a, b, (((1,), (1,)), ((), ())), preferred_element_type=_F32)


def _sortable(x):
    b = lax.bitcast_convert_type(x, jnp.int32)
    return b ^ ((b >> 31) & 0x7FFFFFFF)


def _rope_block(z, cos, sin_lo, sin_hi, half):
    return (z * cos + pltpu.roll(z, LANES - half, 1) * sin_lo + pltpu.roll(z, half, 1) * sin_hi)


def _proj_kernel(x_ref, g_ref, w_ref, tab_ref,
                 qa_o, kaf_o, kab_o, vaf_o, vab_o, qb_o, kbf_o, kbb_o, vbf_o, vbb_o,
                 qi_o, kiwi_o, kir_o, ga_o, gb_o, wi_o):
    kaf_o, vaf_o, kbf_o, vbf_o, kiwi_o = (o.at[0] if len(o.shape) == 3 else o
                                          for o in (kaf_o, vaf_o, kbf_o, vbf_o, kiwi_o))
    x = x_ref[...]
    ms = jnp.mean(x * x, axis=-1, keepdims=True)
    h = (x * lax.rsqrt(ms + RMS_EPS) * g_ref[...]).astype(w_ref.dtype)

    def sec(name):
        a, b = _SEC[name]
        return jnp.dot(h, w_ref[:, a:b], preferred_element_type=_F32)

    t64 = tuple(tab_ref[:, k * LANES:(k + 1) * LANES] for k in (0, 1, 2))
    t32 = tuple(tab_ref[:, k * LANES:(k + 1) * LANES] for k in (3, 4, 5))
    tki = tuple(tab_ref[:, k * LANES:(k + 1) * LANES] for k in (6, 7, 8))

    def roped(z, tab, half):
        return [_rope_block(z[:, c * LANES:(c + 1) * LANES], *tab, half)
                for c in range(z.shape[1] // LANES)]

    scale_a = HEAD_DIM_A ** -0.5 * _LOG2E
    scale_b = HEAD_DIM_B ** -0.5 * _LOG2E
    for c, blk in enumerate(roped(sec("qa"), t64, 8)):
        qa_o[:, c * LANES:(c + 1) * LANES] = (blk * scale_a).astype(qa_o.dtype)
    for c, blk in enumerate(roped(sec("ka"), t64, 8)):
        kaf_o[:, c * LANES:(c + 1) * LANES] = blk
        kab_o[:, c * LANES:(c + 1) * LANES] = blk.astype(kab_o.dtype)
    va = sec("va")
    vaf_o[...] = va
    vab_o[...] = va.astype(vab_o.dtype)
    for c, blk in enumerate(roped(sec("qb"), t64, 8)):
        qb_o[:, c * LANES:(c + 1) * LANES] = (blk * scale_b).astype(qb_o.dtype)
    kb = roped(sec("kb"), t64, 8)[0]
    kbf_o[...] = kb
    kbb_o[...] = kb.astype(kbb_o.dtype)
    vb = sec("vb")
    vbf_o[...] = vb
    vbb_o[...] = vb.astype(vbb_o.dtype)
    for c, blk in enumerate(roped(sec("qi"), t32, 4)):
        qi_o[:, c * LANES:(c + 1) * LANES] = blk.astype(qi_o.dtype)
    kiwi = roped(sec("kiwi"), tki, 4)[0]
    kiwi_o[...] = kiwi
    wi_o[...] = kiwi
    for c, blk in enumerate(roped(sec("kir"), t32, 4)):
        kir_o[:, c * LANES:(c + 1) * LANES] = blk.astype(kir_o.dtype)
    ga_o[...] = jax.nn.sigmoid(sec("ga"))
    gb_o[...] = jax.nn.sigmoid(sec("gb"))


def _rope_tables(pos):
    def cs(d):
        rd = d // 4
        half = rd // 2
        inv_freq = ROPE_THETA ** (-jnp.arange(half, dtype=_F32) * 2.0 / rd)
        ang = pos.astype(_F32)[:, None] * inv_freq[None, :]
        return jnp.cos(ang), jnp.sin(ang), half

    T = pos.shape[0]

    def pattern(d, width):
        c, s, half = cs(d)
        one = jnp.ones((T, width - 2 * half), _F32)
        zero = lambda n: jnp.zeros((T, n), _F32)
        cos = jnp.concatenate([c, c, one], axis=1)
        lo = jnp.concatenate([-s, zero(width - half)], axis=1)
        hi = jnp.concatenate([zero(half), s, zero(width - 2 * half)], axis=1)
        return cos, lo, hi

    tabs = []
    for d, width in ((64, 64), (32, 32), (32, LANES)):
        tabs += [jnp.tile(t, (1, LANES // width)) for t in pattern(d, width)]
    return jnp.concatenate(tabs, axis=1)


def _layout_w_in(w_in):
    offs = np.cumsum((0,) + SPLITS)
    qa, ka, va, qb, kb, vb, qi, ki, wi, ga, gb = (w_in[:, offs[i]:offs[i + 1]] for i in range(11))
    D = w_in.shape[0]
    z64 = jnp.zeros((D, HEAD_DIM_B), w_in.dtype)
    per_kv = N_HEADS_B // N_KV_B
    qb_exp = []
    for hq in range(N_HEADS_B):
        col = qb[:, hq * HEAD_DIM_B:(hq + 1) * HEAD_DIM_B]
        qb_exp += [col, z64] if hq // per_kv == 0 else [z64, col]
    kiwi = jnp.concatenate([ki, wi, jnp.zeros((D, LANES - IDX_DIM - N_IDX_HEADS), w_in.dtype)], axis=1)
    kir = jnp.tile(ki, (1, N_IDX_HEADS))
    w = jnp.concatenate([qa, ka, va] + qb_exp + [kb, vb, qi, kiwi, kir, ga, gb], axis=1)
    assert w.shape[1] == W_LAYOUT
    return w.astype(_MXU)


def _layout_w_pb(w_pb):
    D = w_pb.shape[1]
    z64 = jnp.zeros((HEAD_DIM_B, D), w_pb.dtype)
    per_kv = N_HEADS_B // N_KV_B
    rows = []
    for hq in range(N_HEADS_B):
        r = w_pb[hq * HEAD_DIM_B:(hq + 1) * HEAD_DIM_B]
        rows += [r, z64] if hq // per_kv == 0 else [z64, r]
    return jnp.concatenate(rows, axis=0).astype(_MXU)


def _project(x, pos, g_mix, w_lay, tm, cache_rows=None):
    B, T, D = x.shape
    assert T % tm == 0
    total, off = cache_rows or (T, 0)
    assert off % 8 == 0 and off + T <= total
    tab = _rope_tables(pos)
    row = lambda w: pl.BlockSpec((None, tm, w), lambda b, i: (b, i, 0))
    shifted = lambda w: pl.BlockSpec((pl.Element(1), pl.Element(tm), pl.Element(w)),
                                     lambda b, i: (b, pl.multiple_of(off + i * tm, 8), 0))
    outs = [("qa", QA_W, _MXU), ("ka_f", QA_W, _F32), ("ka", QA_W, _MXU), ("va_f", VA_W, _F32),
            ("va", VA_W, _MXU), ("qb", QB_EXP_W, _MXU), ("kb_f", KB_W, _F32), ("kb", KB_W, _MXU),
            ("vb_f", KB_W, _F32), ("vb", KB_W, _MXU), ("qi", QI_W, _MXU), ("kiwi", LANES, _F32),
            ("kir", QI_W, _MXU), ("ga", D_MODEL, _F32), ("gb", D_MODEL, _F32), ("wi", LANES, _F32)]
    is_cache = lambda n: n.endswith("_f") or n == "kiwi"
    res = pl.pallas_call(
        _proj_kernel,
        grid=(B, T // tm),
        in_specs=[row(D), _const_spec((1, D)),
                  pl.BlockSpec((D, W_LAYOUT), lambda b, i: (0, 0), pipeline_mode=pl.Buffered(1)),
                  pl.BlockSpec((tm, 9 * LANES), lambda b, i: (i, 0))],
        out_specs=[shifted(w) if (is_cache(n) and off) else row(w) for n, w, _ in outs],
        out_shape=[jax.ShapeDtypeStruct((B, total if is_cache(n) else T, w), dt) for n, w, dt in outs],
        compiler_params=_cparams(("parallel", "parallel")),
        name="proj",
    )(x, g_mix.reshape(1, D), w_lay, tab)
    return {n: r for (n, _, _), r in zip(outs, res)}


def _tile_lanes(v, n):
    return v if n == 1 else jnp.concatenate([v] * n, axis=1)


def _softmax_step(s, v, m_sc, l_sc, acc_sc):
    m_prev = m_sc[...]
    m_new = jnp.maximum(m_prev, jnp.max(s, axis=1, keepdims=True))
    alpha = jnp.exp2(m_prev - m_new)
    p = jnp.exp2(s - _tile_lanes(m_new, s.shape[1] // LANES))
    l_sc[...] = alpha * l_sc[...] + jnp.sum(p, axis=1, keepdims=True)
    acc_sc[...] = alpha * acc_sc[...] + jnp.dot(p.astype(v.dtype), v, preferred_element_type=_F32)
    m_sc[...] = m_new


def _attend_tiles(n, scores, values, s0_sc, s1_sc, m_sc, l_sc, acc_sc):
    s0_sc[...] = scores(0)

    def pair(p, carry):
        j = 2 * p
        s1_sc[...] = scores(j + 1)
        _softmax_step(s0_sc[...], values(j), m_sc, l_sc, acc_sc)
        s0_sc[...] = scores(jnp.minimum(j + 2, n - 1))
        _softmax_step(s1_sc[...], values(j + 1), m_sc, l_sc, acc_sc)
        return carry

    lax.fori_loop(0, n // 2, pair, 0)

    @pl.when(n % 2 == 1)
    def _():
        _softmax_step(s0_sc[...], values(n - 1), m_sc, l_sc, acc_sc)


def _num_full_tiles(qchunk, kchunk_main, TQ, TK):
    last = kchunk_main.reshape(-1, TK).max(axis=1)
    qmin = qchunk.reshape(-1, TQ).min(axis=1)
    full = last[None, :] <= qmin[:, None]
    return np.cumprod(full, axis=1).sum(axis=1).astype(np.int32)


def _num_key_tiles(qchunk, kchunk_main, TQ, TK):
    first = kchunk_main.reshape(-1, TK)[:, 0]
    qmax = qchunk.reshape(-1, TQ).max(axis=1)
    return (first[None, :] <= qmax[:, None]).sum(axis=1).astype(np.int32)


def _diff_attn_kernel(nk_ref, nf_ref, q_ref, k_ref, v_ref, kt_ref, vt_ref, qc_ref, kcm_ref, kct_ref,
                      lam_ref, sub_ref, o_ref, s0_sc, s1_sc, m_sc, l_sc, acc_sc, *, TQ, TK, lam_init):
    i = pl.program_id(2)
    q = q_ref[...]
    lane = lax.broadcasted_iota(jnp.int32, q.shape, 1)
    zero = jnp.zeros_like(q)
    qs = jnp.concatenate([jnp.where(lane < HEAD_DIM_A, q, zero),
                          jnp.where(lane >= HEAD_DIM_A, q, zero)], axis=0)
    qc = qc_ref[...]
    qc2 = jnp.concatenate([qc, qc], axis=0)
    m_sc[...] = jnp.full(m_sc.shape, _NEG, _F32)
    l_sc[...] = jnp.zeros(l_sc.shape, _F32)
    acc_sc[...] = jnp.zeros(acc_sc.shape, _F32)

    def keys(j):
        return k_ref[pl.ds(pl.multiple_of(j * TK, TK), TK), :]

    def values(j):
        return v_ref[pl.ds(pl.multiple_of(j * TK, TK), TK), :]

    def masked_scores(k, kc):
        return jnp.where(kc <= qc2, _nt_dot(qs, k), _NEG)

    _attend_tiles(nf_ref[i], lambda j: _nt_dot(qs, keys(j)), values,
                  s0_sc, s1_sc, m_sc, l_sc, acc_sc)

    def diagonal(j, carry):
        _softmax_step(masked_scores(keys(j), kcm_ref[j]), values(j), m_sc, l_sc, acc_sc)
        return carry

    lax.fori_loop(nf_ref[i], nk_ref[i], diagonal, 0)
    _softmax_step(masked_scores(kt_ref[...], kct_ref[...]), vt_ref[...], m_sc, l_sc, acc_sc)

    lam = (jnp.exp(jnp.sum(lam_ref[0:1, :] * lam_ref[1:2, :], axis=1, keepdims=True))
           - jnp.exp(jnp.sum(lam_ref[2:3, :] * lam_ref[3:4, :], axis=1, keepdims=True))
           + lam_init)
    o = acc_sc[...] / l_sc[...]
    d = o[:TQ] - lam * o[TQ:]
    ms = jnp.mean(d * d, axis=-1, keepdims=True)
    o_ref[...] = ((d * lax.rsqrt(ms + RMS_EPS) * sub_ref[...]) * (1.0 - lam_init)).astype(o_ref.dtype)


def _tail_spec(tail, width, index):
    per_batch = tail.shape[0] > 1
    return pl.BlockSpec((None, TAIL, width), lambda b, *rest: (b if per_batch else 0, 0, index(*rest)))


def _diff_attn(q, k, v, kt, vt, qchunk, kchunk, lam_vecs, subln_w, lam_init, Tq, TQ, TK):
    B = q.shape[0]
    Lmain = k.shape[1]
    L = Lmain
    assert Tq % TQ == 0 and Lmain % TK == 0 and kchunk.shape[0] == Lmain + TAIL
    nq, nt = Tq // TQ, Lmain // TK
    nk = jnp.asarray(_num_key_tiles(qchunk[:Tq], kchunk[:Lmain], TQ, TK))
    nf = jnp.asarray(_num_full_tiles(qchunk[:Tq], kchunk[:Lmain], TQ, TK))
    qc = jnp.asarray(qchunk[:Tq].reshape(Tq, 1).astype(np.int32))
    kcm = jnp.asarray(kchunk[:Lmain].reshape(nt, 1, TK).astype(np.int32))
    kct = jnp.asarray(kchunk[Lmain:].reshape(1, TAIL).astype(np.int32))
    R = 2 * TQ
    kern = functools.partial(_diff_attn_kernel, TQ=TQ, TK=TK, lam_init=lam_init)
    head = lambda h, i, *_: h
    grid_spec = pltpu.PrefetchScalarGridSpec(
        num_scalar_prefetch=2,
        grid=(B, N_HEADS_A, nq),
        in_specs=[pl.BlockSpec((None, TQ, LANES), lambda b, h, i, *_: (b, i, h)),
                  pl.BlockSpec((None, L, LANES), lambda b, h, i, *_: (b, 0, h)),
                  pl.BlockSpec((None, L, LANES), lambda b, h, i, *_: (b, 0, h)),
                  _tail_spec(kt, LANES, head), _tail_spec(vt, LANES, head),
                  pl.BlockSpec((TQ, 1), lambda b, h, i, *_: (i, 0)),
                  pl.BlockSpec((nt, 1, TK), lambda b, h, i, *_: (0, 0, 0)),
                  pl.BlockSpec((1, TAIL), lambda b, h, i, *_: (0, 0)),
                  pl.BlockSpec((4, HEAD_DIM_A), lambda b, h, i, *_: (0, 0)),
                  pl.BlockSpec((1, LANES), lambda b, h, i, *_: (0, 0))],
        out_specs=pl.BlockSpec((None, TQ, LANES), lambda b, h, i, *_: (b, i, h)),
        scratch_shapes=[pltpu.VMEM((R, TK), _F32)] * 2 + [pltpu.VMEM((R, LANES), _F32)] * 3,
    )
    return pl.pallas_call(
        kern, grid_spec=grid_spec,
        out_shape=jax.ShapeDtypeStruct((B, Tq, VA_W), _MXU),
        compiler_params=_cparams(("parallel", "parallel", "parallel")),
        name="diff_attn",
    )(nk, nf, q, k, v, kt, vt, qc, kcm, kct, lam_vecs, subln_w.reshape(1, LANES))


def _dsa_kernel(nk_ref, qi_ref, kiwi_ref, qb_ref, kir_ref, kb_ref, vb_ref, kirt_ref, kbt_ref, vbt_ref,
                qc_ref, kcm_ref, kct_ref, kom_ref, kot_ref, o_ref,
                key_sc, keyt_sc, hi_sc, lo_sc, hit_sc, lot_sc, s0_sc, s1_sc, m_sc, l_sc, acc_sc,
                *, TQ, TK, topk):
    i = pl.program_id(1)
    nk = nk_ref[i]
    H = N_IDX_HEADS
    nblk = TK // LANES

    qi = qi_ref[...]
    head = lax.broadcasted_iota(jnp.int32, qi.shape, 1) // IDX_DIM
    zero = jnp.zeros_like(qi)
    qis = jnp.concatenate([jnp.where(head == h, qi, zero) for h in range(H)], axis=0)
    wi = kiwi_ref[:, IDX_DIM:IDX_DIM + H] * (H ** -0.5) * (IDX_DIM ** -0.5)
    wi_b = [jnp.broadcast_to(wi[:, h:h + 1], (TQ, LANES)) for h in range(H)]
    qc = qc_ref[...]

    def score_keys(kir, kc):
        lg = _nt_dot(qis, kir)
        n = lg.shape[1] // LANES
        sc = _tile_lanes(wi_b[0], n) * jnp.maximum(lg[0:TQ], 0.0)
        for h in range(1, H):
            sc = sc + _tile_lanes(wi_b[h], n) * jnp.maximum(lg[h * TQ:(h + 1) * TQ], 0.0)
        return _sortable(jnp.where(kc <= qc, sc, -jnp.inf))

    def split16(key):
        return (key >> 16).astype(jnp.int16), ((key & 0xFFFF) - _HALF).astype(jnp.int16)

    def fill(j, carry):
        off = pl.multiple_of(j * TK, TK)
        key = score_keys(kir_ref[pl.ds(off, TK), :], kcm_ref[j])
        key_sc[j] = key
        hi_sc[j], lo_sc[j] = split16(key)
        return carry

    lax.fori_loop(0, nk, fill, 0)
    key_t = score_keys(kirt_ref[...], kct_ref[...])
    keyt_sc[...] = key_t
    hit_sc[...], lot_sc[...] = split16(key_t)

    def count(ind_main, ind_tail, dtype):
        def body(j, c):
            ind = ind_main(j)
            for b in range(nblk):
                c = c + ind[:, b * LANES:(b + 1) * LANES]
            return c
        c = lax.fori_loop(0, nk, body, jnp.zeros((TQ, LANES), dtype)) + ind_tail()
        return jnp.sum(c.astype(_F32), axis=1, keepdims=True)

    one16, zero16 = jnp.int16(1), jnp.int16(0)

    def lanes16(v):
        return jnp.broadcast_to(v, (TQ, LANES)).astype(jnp.int16)

    def count16(main_sc, tail_sc, t16, strict=False):
        tm = _tile_lanes(t16, nblk)
        cmp = (lambda a, b: a > b) if strict else (lambda a, b: a >= b)
        return count(lambda j: jnp.where(cmp(main_sc[j], tm), one16, zero16),
                     lambda: jnp.where(cmp(tail_sc[...], t16), one16, zero16), jnp.int16)

    def kth_half(main_sc, tail_sc, target):
        def step(it, t_u):
            cand_u = t_u | jnp.left_shift(jnp.int32(1), 15 - it)
            cnt = count16(main_sc, tail_sc, lanes16(cand_u - _HALF))
            return jnp.where(cnt >= target, cand_u, t_u)
        return lax.fori_loop(0, 16, step, jnp.zeros((TQ, 1), jnp.int32))

    kf = float(topk)
    t_hi = kth_half(hi_sc, hit_sc, kf) - _HALF
    t_hi16 = lanes16(t_hi)
    n_above = count16(hi_sc, hit_sc, t_hi16, strict=True)
    t_hi16m = _tile_lanes(t_hi16, nblk)
    low16 = jnp.int16(-_HALF)

    def bucket(j, carry):
        lo_sc[j] = jnp.where(hi_sc[j] == t_hi16m, lo_sc[j], low16)
        return carry

    lax.fori_loop(0, nk, bucket, 0)
    lot_sc[...] = jnp.where(hit_sc[...] == t_hi16, lot_sc[...], low16)
    t_lo = kth_half(lo_sc, lot_sc, kf - n_above)
    n_ge = n_above + count16(lo_sc, lot_sc, lanes16(t_lo - _HALF))
    kth = (t_hi << 16) | t_lo
    thr1 = jnp.maximum(kth, _KEY_NEG_INF + 1)
    thr = jnp.broadcast_to(thr1, (TQ, LANES))
    thr_m = _tile_lanes(thr, nblk)
    excess = jnp.broadcast_to((n_ge > kf) & (kth > _KEY_NEG_INF), (TQ, LANES))

    @pl.when(jnp.max(jnp.where(excess, 1.0, 0.0)) > 0.0)
    def _():
        def count32(pred_main, pred_tail):
            return count(lambda j: jnp.where(pred_main(j), 1.0, 0.0),
                         lambda: jnp.where(pred_tail(), 1.0, 0.0), _F32)

        need = kf - count32(lambda j: key_sc[j] > thr_m, lambda: keyt_sc[...] > thr)
        excess_m = _tile_lanes(excess, nblk)

        def tied_before(J):
            Jm = _tile_lanes(J, nblk)
            return count32(lambda j: (key_sc[j] == thr_m) & (kom_ref[j] < Jm),
                           lambda: (keyt_sc[...] == thr) & (kot_ref[...] < J))

        def search(it, J):
            cand = J | jnp.left_shift(jnp.int32(1), 13 - it)
            return jnp.where(tied_before(cand) < need, cand, J)

        J = lax.fori_loop(0, 14, search, jnp.zeros((TQ, LANES), jnp.int32))
        Jm = _tile_lanes(J, nblk)

        def drop(j, carry):
            kj = key_sc[j]
            key_sc[j] = jnp.where(excess_m & (kj == thr_m) & (kom_ref[j] > Jm), _KEY_NEG_INF, kj)
            return carry

        lax.fori_loop(0, nk, drop, 0)
        kt = keyt_sc[...]
        keyt_sc[...] = jnp.where(excess & (kt == thr) & (kot_ref[...] > J), _KEY_NEG_INF, kt)

    qs = jnp.concatenate([qb_ref[:, h * LANES:(h + 1) * LANES] for h in range(N_HEADS_B)], axis=0)
    m_sc[...] = jnp.full(m_sc.shape, _NEG, _F32)
    l_sc[...] = jnp.zeros(l_sc.shape, _F32)
    acc_sc[...] = jnp.zeros(acc_sc.shape, _F32)

    def masked_scores(keys, kb):
        bias = jnp.where(keys >= _tile_lanes(thr, keys.shape[1] // LANES), 0.0, _NEG)
        return _nt_dot(qs, kb) + jnp.concatenate([bias] * N_HEADS_B, axis=0)

    def scores(j):
        return masked_scores(key_sc[j], kb_ref[pl.ds(pl.multiple_of(j * TK, TK), TK), :])

    def values(j):
        return vb_ref[pl.ds(pl.multiple_of(j * TK, TK), TK), :]

    _attend_tiles(nk, scores, values, s0_sc, s1_sc, m_sc, l_sc, acc_sc)
    _softmax_step(masked_scores(keyt_sc[...], kbt_ref[...]), vbt_ref[...], m_sc, l_sc, acc_sc)
    o = acc_sc[...] / l_sc[...]
    for h in range(N_HEADS_B):
        o_ref[:, h * LANES:(h + 1) * LANES] = o[h * TQ:(h + 1) * TQ].astype(o_ref.dtype)


def _dsa(qi, kiwi, qb, kir, kb, vb, kirt, kbt, vbt, qchunk, kchunk, korder, topk, Tq, TQ, TK):
    B = qi.shape[0]
    L = Lmain = kir.shape[1]
    assert Tq % TQ == 0 and Lmain % TK == 0 and kchunk.shape[0] == Lmain + TAIL
    nq, nt = Tq // TQ, Lmain // TK
    nk = jnp.asarray(_num_key_tiles(qchunk[:Tq], kchunk[:Lmain], TQ, TK))
    qc = jnp.asarray(qchunk[:Tq].reshape(Tq, 1).astype(np.int32))
    main = lambda a: jnp.asarray(a[:Lmain].reshape(nt, 1, TK).astype(np.int32))
    tail = lambda a: jnp.asarray(a[Lmain:].reshape(1, TAIL).astype(np.int32))
    R = N_HEADS_B * TQ
    kern = functools.partial(_dsa_kernel, TQ=TQ, TK=TK, topk=topk)
    first = lambda *_: 0
    qrow = lambda w: pl.BlockSpec((None, TQ, w), lambda b, i, nk: (b, i, 0))
    krow = lambda w: pl.BlockSpec((None, L, w), lambda b, i, nk: (b, 0, 0),
                                  pipeline_mode=pl.Buffered(1))
    cm = pl.BlockSpec((nt, 1, TK), lambda b, i, nk: (0, 0, 0))
    ct = pl.BlockSpec((1, TAIL), lambda b, i, nk: (0, 0))
    grid_spec = pltpu.PrefetchScalarGridSpec(
        num_scalar_prefetch=1,
        grid=(B, nq),
        in_specs=[qrow(QI_W), qrow(LANES), qrow(QB_EXP_W), krow(QI_W), krow(KB_W), krow(KB_W),
                  _tail_spec(kirt, QI_W, first), _tail_spec(kbt, KB_W, first), _tail_spec(vbt, KB_W, first),
                  pl.BlockSpec((TQ, 1), lambda b, i, nk: (i, 0)), cm, ct, cm, ct],
        out_specs=qrow(QB_EXP_W),
        scratch_shapes=[pltpu.VMEM((nt, TQ, TK), jnp.int32), pltpu.VMEM((TQ, TAIL), jnp.int32),
                        pltpu.VMEM((nt, TQ, TK), jnp.int16), pltpu.VMEM((nt, TQ, TK), jnp.int16),
                        pltpu.VMEM((TQ, TAIL), jnp.int16), pltpu.VMEM((TQ, TAIL), jnp.int16),
                        pltpu.VMEM((R, TK), _F32), pltpu.VMEM((R, TK), _F32),
                        pltpu.VMEM((R, LANES), _F32), pltpu.VMEM((R, LANES), _F32),
                        pltpu.VMEM((R, LANES), _F32)],
    )
    return pl.pallas_call(
        kern, grid_spec=grid_spec,
        out_shape=jax.ShapeDtypeStruct((B, Tq, QB_EXP_W), _MXU),
        compiler_params=_cparams(("parallel", "parallel")),
        name="dsa",
    )(nk, qi, kiwi, qb, kir, kb, vb, kirt, kbt, vbt, qc,
      main(kchunk), tail(kchunk), main(korder), tail(korder))


def _post_kernel(x_ref, oa_ref, ob_ref, ga_ref, gb_ref, wpa_ref, wpb_ref, wo_ref, g_ref,
                 wr_ref, br_ref, x1_o, h2_o, route_o):
    pa = jnp.dot(oa_ref[...], wpa_ref[...], preferred_element_type=_F32)
    pb = jnp.dot(ob_ref[...], wpb_ref[...], preferred_element_type=_F32)
    merged = ga_ref[...] * pa + gb_ref[...] * pb
    x1 = x_ref[...] + jnp.dot(merged.astype(wo_ref.dtype), wo_ref[...], preferred_element_type=_F32)
    x1_o[...] = x1
    ms = jnp.mean(x1 * x1, axis=-1, keepdims=True)
    h2 = (x1 * lax.rsqrt(ms + RMS_EPS) * g_ref[...]).astype(h2_o.dtype)
    h2_o[...] = h2
    cur = jnp.dot(h2, wr_ref[...], preferred_element_type=_F32) + br_ref[...]
    lane = lax.broadcasted_iota(jnp.int32, cur.shape, 1).astype(_F32)
    vals, idxs = [], []
    for _ in range(TOP_K_EXPERTS):
        m = jnp.max(cur, axis=1, keepdims=True)
        idx = jnp.min(jnp.where(cur == m, lane, float(LANES)), axis=1, keepdims=True)
        vals.append(m)
        idxs.append(idx)
        cur = jnp.where(lane == idx, -jnp.inf, cur)
    es = [jnp.exp(v - vals[0]) for v in vals]
    denom = es[0] + es[1] + es[2] + es[3]
    route = jnp.zeros(cur.shape, _F32)
    for k in range(TOP_K_EXPERTS):
        route = jnp.where(lane == float(k), idxs[k], route)
        route = jnp.where(lane == float(TOP_K_EXPERTS + k), es[k] / denom, route)
    route_o[...] = route


def _post(x, oa, ob, ga, gb, w_pa, w_pb_exp, w_o, g_ffn, w_router, b_router, T, tm):
    B, _, D = x.shape
    assert T % tm == 0
    row = lambda w: pl.BlockSpec((None, tm, w), lambda b, i: (b, i, 0))
    wr = jnp.concatenate([w_router, jnp.zeros((D, LANES - N_EXPERTS), w_router.dtype)], axis=1).astype(_MXU)
    br = jnp.concatenate([b_router.astype(_F32), jnp.full((LANES - N_EXPERTS,), _NEG, _F32)]).reshape(1, LANES)
    return pl.pallas_call(
        _post_kernel,
        grid=(B, T // tm),
        in_specs=[row(D), row(VA_W), row(QB_EXP_W), row(D), row(D),
                  _const_spec((VA_W, D)), _const_spec((QB_EXP_W, D)), _const_spec((D, D)),
                  _const_spec((1, D)), _const_spec((D, LANES)), _const_spec((1, LANES))],
        out_specs=[row(D), row(D), row(LANES)],
        out_shape=[jax.ShapeDtypeStruct((B, T, D), _F32), jax.ShapeDtypeStruct((B, T, D), _MXU),
                   jax.ShapeDtypeStruct((B, T, LANES), _F32)],
        compiler_params=_cparams(("parallel", "parallel")),
        name="post",
    )(x, oa, ob, ga, gb, w_pa.astype(_MXU), w_pb_exp, w_o.astype(_MXU), g_ffn.reshape(1, D), wr, br)


def _moe_kernel(be_ref, nu_ref, x_ref, wgu_ref, bgu_ref, wdn_ref, bdn_ref, y_ref, wgu_sc, wdn_sc):
    i = pl.program_id(0)
    used = i < nu_ref[0]
    new_expert = (i == 0) | (be_ref[i] != be_ref[jnp.maximum(i - 1, 0)])

    @pl.when(used & new_expert)
    def _():
        wgu_sc[...] = wgu_ref[...].astype(wgu_sc.dtype)
        wdn_sc[...] = wdn_ref[...].astype(wdn_sc.dtype)

    @pl.when(used)
    def _():
        gu = jnp.dot(x_ref[...], wgu_sc[...], preferred_element_type=_F32) + bgu_ref[...]
        glu = jnp.minimum(gu[:, :D_EXPERT], SWIGLU_LIMIT)
        lin = jnp.clip(gu[:, D_EXPERT:], -SWIGLU_LIMIT, SWIGLU_LIMIT)
        act = glu * jax.nn.sigmoid(SWIGLU_ALPHA * glu) * (lin + 1.0)
        y_ref[...] = jnp.dot(act.astype(wdn_sc.dtype), wdn_sc[...],
                             preferred_element_type=_F32) + bdn_ref[...]

    @pl.when(jnp.logical_not(used))
    def _():
        y_ref[...] = jnp.zeros(y_ref.shape, y_ref.dtype)


def _moe(h2, route, w_gu, b_gu, w_dn, b_dn):
    T, D = h2.shape
    K = TOP_K_EXPERTS
    A = T * K
    NB = -(-A // MOE_BLOCK) + N_EXPERTS
    NS = NB * MOE_BLOCK
    e_flat = route[:, :K].astype(jnp.int32).reshape(-1)
    experts = jnp.arange(N_EXPERTS, dtype=jnp.int32)
    counts = jnp.sum((e_flat[:, None] == experts[None, :]).astype(jnp.int32), axis=0)
    blocks_per = (counts + MOE_BLOCK - 1) // MOE_BLOCK
    cum_blocks = jnp.cumsum(blocks_per)
    blk = jnp.arange(NB, dtype=jnp.int32)
    block_expert = jnp.minimum(jnp.sum((cum_blocks[None, :] <= blk[:, None]).astype(jnp.int32), axis=1),
                               N_EXPERTS - 1)
    n_used = cum_blocks[-1:].astype(jnp.int32)
    cum_pad = jnp.cumsum(blocks_per * MOE_BLOCK - counts)
    r = jnp.arange(NS - A, dtype=jnp.int32)
    empty_expert = jnp.sum((cum_pad[None, :] <= r[:, None]).astype(jnp.int32), axis=1)
    a = jnp.arange(A, dtype=jnp.int32)
    _, slot_tok, slot_src = lax.sort(
        (jnp.concatenate([2 * e_flat, 2 * empty_expert + 1]),
         jnp.concatenate([a // K, jnp.full((NS - A,), T, jnp.int32)]),
         jnp.concatenate([a, A + r])), num_keys=1, is_stable=True)
    x_pad = jnp.concatenate([h2, jnp.zeros((1, D), h2.dtype)], axis=0)
    x_slots = x_pad[slot_tok]
    y_slots = pl.pallas_call(
        _moe_kernel,
        grid_spec=pltpu.PrefetchScalarGridSpec(
            num_scalar_prefetch=2,
            grid=(NB,),
            in_specs=[pl.BlockSpec((MOE_BLOCK, D), lambda i, be, nu: (i, 0)),
                      pl.BlockSpec((None, D, 2 * D_EXPERT), lambda i, be, nu: (be[i], 0, 0)),
                      pl.BlockSpec((None, 1, 2 * D_EXPERT), lambda i, be, nu: (be[i], 0, 0)),
                      pl.BlockSpec((None, D_EXPERT, D), lambda i, be, nu: (be[i], 0, 0)),
                      pl.BlockSpec((None, 1, D), lambda i, be, nu: (be[i], 0, 0))],
            out_specs=pl.BlockSpec((MOE_BLOCK, D), lambda i, be, nu: (i, 0)),
            scratch_shapes=[pltpu.VMEM((D, 2 * D_EXPERT), _MXU), pltpu.VMEM((D_EXPERT, D), _MXU)],
        ),
        out_shape=jax.ShapeDtypeStruct((NB * MOE_BLOCK, D), _F32),
        compiler_params=_cparams(("arbitrary",)),
        name="moe",
    )(block_expert, n_used, x_slots, w_gu, b_gu.reshape(N_EXPERTS, 1, -1).astype(_F32),
      w_dn, b_dn.reshape(N_EXPERTS, 1, -1).astype(_F32))
    _, dest = lax.sort((slot_src, jnp.arange(NS, dtype=jnp.int32)), num_keys=1)
    dest = dest[:A].reshape(T, K)
    return [y_slots[dest[:, k]] for k in range(K)]


def _final_kernel(x_ref, y0_ref, y1_ref, y2_ref, y3_ref, route_ref, g_ref, o_ref):
    K = TOP_K_EXPERTS
    x = x_ref[...]
    for k, y_ref in enumerate((y0_ref, y1_ref, y2_ref, y3_ref)):
        x = x + route_ref[:, K + k:K + k + 1] * y_ref[...]
    ms = jnp.mean(x * x, axis=-1, keepdims=True)
    o_ref[...] = x * lax.rsqrt(ms + RMS_EPS) * g_ref[...]


def _final(x1, ys, route, g_final, tm):
    T, D = x1.shape
    assert T % tm == 0
    row = pl.BlockSpec((tm, D), lambda i: (i, 0))
    return pl.pallas_call(
        _final_kernel, grid=(T // tm,),
        in_specs=[row] * 5 + [pl.BlockSpec((tm, LANES), lambda i: (i, 0)), _const_spec((1, D))],
        out_specs=row,
        out_shape=jax.ShapeDtypeStruct((T, D), _F32),
        compiler_params=_cparams(("parallel",)),
        name="final",
    )(x1, *ys, route, g_final.reshape(1, D))


def _group(x_rows, proj, keys, tails, qchunk, kchunk, korder, topk, Tq, tiles, weights, lam_init):
    (lam_vecs, subln_w, w_pa, w_pb_exp, w_o, g_ffn, w_router, b_router,
     w_gu, b_gu, w_dn, b_dn, g_final) = weights
    B = x_rows.shape[0]
    oa = _diff_attn(proj["qa"], keys["ka"], keys["va"], tails["ka"], tails["va"], qchunk, kchunk,
                    lam_vecs, subln_w, lam_init, Tq, tiles["tq_a"], tiles["tk"])
    ob = _dsa(proj["qi"], proj["wi"], proj["qb"], keys["kir"], keys["kb"], keys["vb"],
              tails["kir"], tails["kb"], tails["vb"], qchunk, kchunk, korder, topk, Tq,
              tiles["tq_b"], tiles["tk"])
    x1, h2, route = _post(x_rows, oa, ob, proj["ga"], proj["gb"], w_pa, w_pb_exp, w_o, g_ffn,
                          w_router, b_router, Tq, tiles["tm_post"])
    T = B * Tq
    route = route.reshape(T, LANES)
    ys = _moe(h2.reshape(T, D_MODEL), route, w_gu, b_gu, w_dn, b_dn)
    y = _final(x1.reshape(T, D_MODEL), ys, route, g_final, tiles["tm_post"])
    return y.reshape(B, Tq, D_MODEL)


_KEY_OPERANDS = ("ka", "va", "kb", "vb", "kir")


def kernel(x_prompt, x_sample, cache_a_k, cache_a_v, cache_b_k, cache_b_v, cache_idx_k,
           meta_tokens, g_mix, w_in, lambda_q1, lambda_k1, lambda_q2, lambda_k2, subln_w,
           w_pa, w_pb, w_o, g_ffn, w_router, b_router, w_gu, b_gu, w_dn, b_dn, g_final):
    B, S, D = x_prompt.shape
    Bs, S_dec, _ = x_sample.shape
    P = cache_a_k.shape[2]
    depth = g_mix.shape[0]
    assert depth == 1
    l = 0
    lam_init = 0.8 - 0.6 * math.exp(-0.3 * l)
    w_lay = _layout_w_in(w_in[l])
    w_pb_exp = _layout_w_pb(w_pb[l])
    lam_vecs = jnp.stack([lambda_q1[l], lambda_k1[l], lambda_q2[l], lambda_k2[l]]).astype(_F32)
    weights = (lam_vecs, subln_w[l], w_pa[l], w_pb_exp, w_o[l], g_ffn[l], w_router[l], b_router[l],
               w_gu[l], b_gu[l], w_dn[l], b_dn[l], g_final)

    r = np.arange(S + TAIL)
    chunk_p = np.where(r < S, r // CHUNK, np.where(r < S + N_META, -1, _BIG_CHUNK)).astype(np.int64)
    order_p = np.where(r < S, r + N_META, r - S).astype(np.int32)
    tiles_p = dict(tq_a=min(512, S), tq_b=min(128, S), tk=min(512, S), tm_post=min(512, S))
    pp = _project(x_prompt, jnp.asarray(N_META + np.arange(S, dtype=np.int32)), g_mix[l], w_lay,
                  tiles_p["tm_post"], cache_rows=(N_META + S, N_META))
    meta_tile = jnp.concatenate([meta_tokens.astype(x_prompt.dtype),
                                 jnp.zeros((TAIL - N_META, D), x_prompt.dtype)])[None]
    pos_m = np.where(np.arange(TAIL) < N_META, np.arange(TAIL), 0).astype(np.int32)
    pm = _project(meta_tile, jnp.asarray(pos_m), g_mix[l], w_lay, TAIL)
    y_prompt = _group(x_prompt, pp, pp, pm, chunk_p[:S], chunk_p, order_p, min(TOPK_MAX, S // 4), S,
                      tiles_p, weights, lam_init)

    def rows_p(name, shape, width=None):
        meta_rows = jnp.broadcast_to(pm[name][:, :N_META], (B, N_META, pm[name].shape[-1]))
        a = pp[name].at[:, :N_META].set(meta_rows)
        return a[:, :, :width].reshape((1, B, N_META + S) + shape)

    pos_s = (P + np.arange(S_dec)).astype(np.int32)
    ps = _project(x_sample, jnp.asarray(pos_s), g_mix[l], w_lay, S_dec)
    r = np.arange(P + TAIL)
    chunk_s = np.where(r < P + S_dec, r // CHUNK, _BIG_CHUNK).astype(np.int64)
    qchunk_s = (pos_s // CHUNK).astype(np.int64)
    cached = dict(ka=cache_a_k[l], va=cache_a_v[l], kb=cache_b_k[l], vb=cache_b_v[l],
                  kir=jnp.tile(cache_idx_k[l], (1, 1, N_IDX_HEADS)))
    keys_s = {n: cached[n].reshape(Bs, P, -1).astype(_MXU) for n in _KEY_OPERANDS}
    tails_s = {n: jnp.concatenate([ps[n], jnp.zeros((Bs, TAIL - S_dec, ps[n].shape[-1]), _MXU)], axis=1)
               for n in _KEY_OPERANDS}
    tiles_s = dict(tq_a=S_dec, tq_b=S_dec, tk=min(512, P), tm_post=S_dec)
    y_sample = _group(x_sample, ps, keys_s, tails_s, qchunk_s, chunk_s, r.astype(np.int32),
                      min(TOPK_MAX, (P + S_dec) // 4), S_dec, tiles_s, weights, lam_init)

    def rows_s(name, shape, width=None):
        return ps[name][:, :, :width].reshape((1, Bs, S_dec) + shape)

    ha, hb = (N_HEADS_A, 2 * HEAD_DIM_A), (N_KV_B, HEAD_DIM_B)
    return (y_prompt, y_sample,
            rows_p("ka_f", ha), rows_p("va_f", ha), rows_p("kb_f", hb), rows_p("vb_f", hb),
            rows_p("kiwi", (IDX_DIM,), IDX_DIM),
            rows_s("ka_f", ha), rows_s("va_f", ha), rows_s("kb_f", hb), rows_s("vb_f", hb),
            rows_s("kiwi", (IDX_DIM,), IDX_DIM))
```

```python
import functools
import math

import numpy as np
import jax
import jax.numpy as jnp
from jax import lax
from jax.experimental import pallas as pl
from jax.experimental.pallas import tpu as pltpu

D_MODEL = 1024
CHUNK = 64
N_META = 16
ROPE_THETA = 500000.0
RMS_EPS = 1e-6
N_HEADS_A = 4
HEAD_DIM_A = 64
N_HEADS_B = 8
N_KV_B = 2
HEAD_DIM_B = 64
N_IDX_HEADS = 8
IDX_DIM = 32
TOPK_MAX = 256
N_EXPERTS = 32
TOP_K_EXPERTS = 4
D_EXPERT = 1024
SWIGLU_LIMIT = 7.0
SWIGLU_ALPHA = 1.702
MOE_BLOCK = 512

QA_W = N_HEADS_A * 2 * HEAD_DIM_A
VA_W = QA_W
QB_W = N_HEADS_B * HEAD_DIM_B
KB_W = N_KV_B * HEAD_DIM_B
QI_W = N_IDX_HEADS * IDX_DIM
SPLITS = (QA_W, QA_W, VA_W, QB_W, KB_W, KB_W, QI_W, IDX_DIM, N_IDX_HEADS, D_MODEL, D_MODEL)

LANES = 128
TAIL = LANES
QB_EXP_W = N_HEADS_B * LANES

_SEC = {}
_off = 0
for _name, _w in (("qa", QA_W), ("ka", QA_W), ("va", VA_W), ("qb", QB_EXP_W), ("kb", KB_W),
                  ("vb", KB_W), ("qi", QI_W), ("kiwi", LANES), ("kir", QI_W),
                  ("ga", D_MODEL), ("gb", D_MODEL)):
    _SEC[_name] = (_off, _off + _w)
    _off += _w
W_LAYOUT = _off

_MXU = jnp.bfloat16
_F32 = jnp.float32
_NEG = -1e30
_BIG_CHUNK = 2 ** 30
_INT_MIN = -2 ** 31
_LOG2E = math.log2(math.e)
_HALF = 2 ** 15
_KEY_NEG_INF = int(np.int32(np.uint32(0xFF800000) ^ np.uint32(0x7FFFFFFF)))
_VMEM_LIMIT = 56 * 1024 * 1024


def _cparams(sem):
    return pltpu.CompilerParams(dimension_semantics=sem, vmem_limit_bytes=_VMEM_LIMIT)


def _const_spec(shape):
    nd = len(shape)
    return pl.BlockSpec(shape, lambda *_: (0,) * nd)


def _nt_dot(a, b):
    return lax.dot_general(a, b, (((1,), (1,)), ((), ())), preferred_element_type=_F32)


def _sortable(x):
    b = lax.bitcast_convert_type(x, jnp.int32)
    return b ^ ((b >> 31) & 0x7FFFFFFF)


def _rope_block(z, cos, sin_lo, sin_hi, half):
    return (z * cos + pltpu.roll(z, LANES - half, 1) * sin_lo + pltpu.roll(z, half, 1) * sin_hi)


def _proj_kernel(x_ref, g_ref, w_ref, tab_ref, *refs):
    (qa_o, kaf_o, kab_o, vaf_o, vab_o, qb_o, kbf_o, kbb_o, vbf_o, vbb_o,
     qi_o, kiwi_o, kir_o, ga_o, gb_o, wi_o) = refs[-16:]
    kaf_o, vaf_o, kbf_o, vbf_o, kiwi_o = (o.at[0] if len(o.shape) == 3 else o
                                          for o in (kaf_o, vaf_o, kbf_o, vbf_o, kiwi_o))
    x = x_ref[...]
    ms = jnp.mean(x * x, axis=-1, keepdims=True)
    h = (x * lax.rsqrt(ms + RMS_EPS) * g_ref[...]).astype(w_ref.dtype)

    def sec(name):
        a, b = _SEC[name]
        return jnp.dot(h, w_ref[:, a:b], preferred_element_type=_F32)

    t64 = tuple(tab_ref[:, k * LANES:(k + 1) * LANES] for k in (0, 1, 2))
    t32 = tuple(tab_ref[:, k * LANES:(k + 1) * LANES] for k in (3, 4, 5))
    tki = tuple(tab_ref[:, k * LANES:(k + 1) * LANES] for k in (6, 7, 8))

    def roped(z, tab, half):
        return [_rope_block(z[:, c * LANES:(c + 1) * LANES], *tab, half)
                for c in range(z.shape[1] // LANES)]

    scale_a = HEAD_DIM_A ** -0.5 * _LOG2E
    scale_b = HEAD_DIM_B ** -0.5 * _LOG2E
    for c, blk in enumerate(roped(sec("qa"), t64, 8)):
        qa_o[:, c * LANES:(c + 1) * LANES] = (blk * scale_a).astype(qa_o.dtype)
    for c, blk in enumerate(roped(sec("ka"), t64, 8)):
        kaf_o[:, c * LANES:(c + 1) * LANES] = blk
        kab_o[:, c * LANES:(c + 1) * LANES] = blk.astype(kab_o.dtype)
    va = sec("va")
    vaf_o[...] = va
    vab_o[...] = va.astype(vab_o.dtype)
    for c, blk in enumerate(roped(sec("qb"), t64, 8)):
        qb_o[:, c * LANES:(c + 1) * LANES] = (blk * scale_b).astype(qb_o.dtype)
    kb = roped(sec("kb"), t64, 8)[0]
    kbf_o[...] = kb
    kbb_o[...] = kb.astype(kbb_o.dtype)
    vb = sec("vb")
    vbf_o[...] = vb
    vbb_o[...] = vb.astype(vbb_o.dtype)
    for c, blk in enumerate(roped(sec("qi"), t32, 4)):
        qi_o[:, c * LANES:(c + 1) * LANES] = blk.astype(qi_o.dtype)
    kiwi = roped(sec("kiwi"), tki, 4)[0]
    kiwi_o[...] = kiwi
    wi_o[...] = kiwi
    for c, blk in enumerate(roped(sec("kir"), t32, 4)):
        kir_o[:, c * LANES:(c + 1) * LANES] = blk.astype(kir_o.dtype)
    ga_o[...] = jax.nn.sigmoid(sec("ga"))
    gb_o[...] = jax.nn.sigmoid(sec("gb"))


def _rope_tables(pos):
    def cs(d):
        rd = d // 4
        half = rd // 2
        inv_freq = ROPE_THETA ** (-jnp.arange(half, dtype=_F32) * 2.0 / rd)
        ang = pos.astype(_F32)[:, None] * inv_freq[None, :]
        return jnp.cos(ang), jnp.sin(ang), half

    T = pos.shape[0]

    def pattern(d, width):
        c, s, half = cs(d)
        one = jnp.ones((T, width - 2 * half), _F32)
        zero = lambda n: jnp.zeros((T, n), _F32)
        cos = jnp.concatenate([c, c, one], axis=1)
        lo = jnp.concatenate([-s, zero(width - half)], axis=1)
        hi = jnp.concatenate([zero(half), s, zero(width - 2 * half)], axis=1)
        return cos, lo, hi

    tabs = []
    for d, width in ((64, 64), (32, 32), (32, LANES)):
        tabs += [jnp.tile(t, (1, LANES // width)) for t in pattern(d, width)]
    return jnp.concatenate(tabs, axis=1)


def _layout_w_in(w_in):
    offs = np.cumsum((0,) + SPLITS)
    qa, ka, va, qb, kb, vb, qi, ki, wi, ga, gb = (w_in[:, offs[i]:offs[i + 1]] for i in range(11))
    D = w_in.shape[0]
    z64 = jnp.zeros((D, HEAD_DIM_B), w_in.dtype)
    per_kv = N_HEADS_B // N_KV_B
    qb_exp = []
    for hq in range(N_HEADS_B):
        col = qb[:, hq * HEAD_DIM_B:(hq + 1) * HEAD_DIM_B]
        qb_exp += [col, z64] if hq // per_kv == 0 else [z64, col]
    kiwi = jnp.concatenate([ki, wi, jnp.zeros((D, LANES - IDX_DIM - N_IDX_HEADS), w_in.dtype)], axis=1)
    kir = jnp.tile(ki, (1, N_IDX_HEADS))
    w = jnp.concatenate([qa, ka, va] + qb_exp + [kb, vb, qi, kiwi, kir, ga, gb], axis=1)
    assert w.shape[1] == W_LAYOUT
    return w.astype(_MXU)


def _layout_w_pb(w_pb):
    D = w_pb.shape[1]
    z64 = jnp.zeros((HEAD_DIM_B, D), w_pb.dtype)
    per_kv = N_HEADS_B // N_KV_B
    rows = []
    for hq in range(N_HEADS_B):
        r = w_pb[hq * HEAD_DIM_B:(hq + 1) * HEAD_DIM_B]
        rows += [r, z64] if hq // per_kv == 0 else [z64, r]
    return jnp.concatenate(rows, axis=0).astype(_MXU)


def _project(x, pos, g_mix, w_lay, tm, head_rows=None):
    B, T, D = x.shape
    assert T % tm == 0
    off = next(iter(head_rows.values())).shape[0] if head_rows else 0
    total = off + T
    assert off % 8 == 0
    tab = _rope_tables(pos)
    row = lambda w: pl.BlockSpec((None, tm, w), lambda b, i: (b, i, 0))
    shifted = lambda w: pl.BlockSpec((pl.Element(1), pl.Element(tm), pl.Element(w)),
                                     lambda b, i: (b, pl.multiple_of(off + i * tm, 8), 0))
    outs = [("qa", QA_W, _MXU), ("ka_f", QA_W, _F32), ("ka", QA_W, _MXU), ("va_f", VA_W, _F32),
            ("va", VA_W, _MXU), ("qb", QB_EXP_W, _MXU), ("kb_f", KB_W, _F32), ("kb", KB_W, _MXU),
            ("vb_f", KB_W, _F32), ("vb", KB_W, _MXU), ("qi", QI_W, _MXU), ("kiwi", LANES, _F32),
            ("kir", QI_W, _MXU), ("ga", D_MODEL, _F32), ("gb", D_MODEL, _F32), ("wi", LANES, _F32)]
    is_cache = lambda n: n.endswith("_f") or n == "kiwi"
    n_in = 4
    heads, aliases = [], {}
    for k, (n, w, _) in enumerate(outs):
        if head_rows and is_cache(n):
            rows = jnp.broadcast_to(head_rows[n][None], (B, off, w))
            heads.append(jnp.concatenate([rows, jnp.zeros((B, T, w), _F32)], axis=1))
            aliases[n_in + len(heads) - 1] = k
    res = pl.pallas_call(
        _proj_kernel,
        grid=(B, T // tm),
        in_specs=[row(D), _const_spec((1, D)),
                  pl.BlockSpec((D, W_LAYOUT), lambda b, i: (0, 0), pipeline_mode=pl.Buffered(1)),
                  pl.BlockSpec((tm, 9 * LANES), lambda b, i: (i, 0))]
                 + [pl.BlockSpec(memory_space=pl.ANY)] * len(heads),
        out_specs=[shifted(w) if (is_cache(n) and off) else row(w) for n, w, _ in outs],
        out_shape=[jax.ShapeDtypeStruct((B, total if is_cache(n) else T, w), dt) for n, w, dt in outs],
        input_output_aliases=aliases,
        compiler_params=_cparams(("parallel", "parallel")),
        name="proj",
    )(x, g_mix.reshape(1, D), w_lay, tab, *heads)
    return {n: r for (n, _, _), r in zip(outs, res)}


def _tile_lanes(v, n):
    return v if n == 1 else jnp.concatenate([v] * n, axis=1)


def _softmax_step(s, v, m_sc, l_sc, acc_sc):
    m_prev = m_sc[...]
    m_new = jnp.maximum(m_prev, jnp.max(s, axis=1, keepdims=True))
    alpha = jnp.exp2(m_prev - m_new)
    p = jnp.exp2(s - _tile_lanes(m_new, s.shape[1] // LANES))
    l_sc[...] = alpha * l_sc[...] + jnp.sum(p, axis=1, keepdims=True)
    acc_sc[...] = alpha * acc_sc[...] + jnp.dot(p.astype(v.dtype), v, preferred_element_type=_F32)
    m_sc[...] = m_new


def _attend_tiles(n, scores, values, s0_sc, s1_sc, m_sc, l_sc, acc_sc):
    s0_sc[...] = scores(0)

    def pair(p, carry):
        j = 2 * p
        s1_sc[...] = scores(j + 1)
        _softmax_step(s0_sc[...], values(j), m_sc, l_sc, acc_sc)
        s0_sc[...] = scores(jnp.minimum(j + 2, n - 1))
        _softmax_step(s1_sc[...], values(j + 1), m_sc, l_sc, acc_sc)
        return carry

    lax.fori_loop(0, n // 2, pair, 0)

    @pl.when(n % 2 == 1)
    def _():
        _softmax_step(s0_sc[...], values(n - 1), m_sc, l_sc, acc_sc)


def _num_full_tiles(qchunk, kchunk_main, TQ, TK):
    last = kchunk_main.reshape(-1, TK).max(axis=1)
    qmin = qchunk.reshape(-1, TQ).min(axis=1)
    full = last[None, :] <= qmin[:, None]
    return np.cumprod(full, axis=1).sum(axis=1).astype(np.int32)


def _num_key_tiles(qchunk, kchunk_main, TQ, TK):
    first = kchunk_main.reshape(-1, TK)[:, 0]
    qmax = qchunk.reshape(-1, TQ).max(axis=1)
    return (first[None, :] <= qmax[:, None]).sum(axis=1).astype(np.int32)


def _diff_attn_kernel(nk_ref, nf_ref, q_ref, k_ref, v_ref, kt_ref, vt_ref, qc_ref, kcm_ref, kct_ref,
                      lam_ref, sub_ref, o_ref, s0_sc, s1_sc, m_sc, l_sc, acc_sc, *, TQ, TK, lam_init):
    i = pl.program_id(2)
    q = q_ref[...]
    lane = lax.broadcasted_iota(jnp.int32, q.shape, 1)
    zero = jnp.zeros_like(q)
    qs = jnp.concatenate([jnp.where(lane < HEAD_DIM_A, q, zero),
                          jnp.where(lane >= HEAD_DIM_A, q, zero)], axis=0)
    qc = qc_ref[...]
    qc2 = jnp.concatenate([qc, qc], axis=0)
    m_sc[...] = jnp.full(m_sc.shape, _NEG, _F32)
    l_sc[...] = jnp.zeros(l_sc.shape, _F32)
    acc_sc[...] = jnp.zeros(acc_sc.shape, _F32)

    def keys(j):
        return k_ref[pl.ds(pl.multiple_of(j * TK, TK), TK), :]

    def values(j):
        return v_ref[pl.ds(pl.multiple_of(j * TK, TK), TK), :]

    def masked_scores(k, kc):
        return jnp.where(kc <= qc2, _nt_dot(qs, k), _NEG)

    _attend_tiles(nf_ref[i], lambda j: _nt_dot(qs, keys(j)), values,
                  s0_sc, s1_sc, m_sc, l_sc, acc_sc)

    def diagonal(j, carry):
        _softmax_step(masked_scores(keys(j), kcm_ref[j]), values(j), m_sc, l_sc, acc_sc)
        return carry

    lax.fori_loop(nf_ref[i], nk_ref[i], diagonal, 0)
    _softmax_step(masked_scores(kt_ref[...], kct_ref[...]), vt_ref[...], m_sc, l_sc, acc_sc)

    lam = (jnp.exp(jnp.sum(lam_ref[0:1, :] * lam_ref[1:2, :], axis=1, keepdims=True))
           - jnp.exp(jnp.sum(lam_ref[2:3, :] * lam_ref[3:4, :], axis=1, keepdims=True))
           + lam_init)
    o = acc_sc[...] / l_sc[...]
    d = o[:TQ] - lam * o[TQ:]
    ms = jnp.mean(d * d, axis=-1, keepdims=True)
    o_ref[...] = ((d * lax.rsqrt(ms + RMS_EPS) * sub_ref[...]) * (1.0 - lam_init)).astype(o_ref.dtype)


def _tail_spec(tail, width, index):
    per_batch = tail.shape[0] > 1
    return pl.BlockSpec((None, TAIL, width), lambda b, *rest: (b if per_batch else 0, 0, index(*rest)))


def _diff_attn(q, k, v, kt, vt, qchunk, kchunk, lam_vecs, subln_w, lam_init, Tq, TQ, TK):
    B = q.shape[0]
    Lmain = k.shape[1]
    L = Lmain
    assert Tq % TQ == 0 and Lmain % TK == 0 and kchunk.shape[0] == Lmain + TAIL
    nq, nt = Tq // TQ, Lmain // TK
    nk = jnp.asarray(_num_key_tiles(qchunk[:Tq], kchunk[:Lmain], TQ, TK))
    nf = jnp.asarray(_num_full_tiles(qchunk[:Tq], kchunk[:Lmain], TQ, TK))
    qc = jnp.asarray(qchunk[:Tq].reshape(Tq, 1).astype(np.int32))
    kcm = jnp.asarray(kchunk[:Lmain].reshape(nt, 1, TK).astype(np.int32))
    kct = jnp.asarray(kchunk[Lmain:].reshape(1, TAIL).astype(np.int32))
    R = 2 * TQ
    kern = functools.partial(_diff_attn_kernel, TQ=TQ, TK=TK, lam_init=lam_init)
    head = lambda h, i, *_: h
    grid_spec = pltpu.PrefetchScalarGridSpec(
        num_scalar_prefetch=2,
        grid=(B, N_HEADS_A, nq),
        in_specs=[pl.BlockSpec((None, TQ, LANES), lambda b, h, i, *_: (b, i, h)),
                  pl.BlockSpec((None, L, LANES), lambda b, h, i, *_: (b, 0, h)),
                  pl.BlockSpec((None, L, LANES), lambda b, h, i, *_: (b, 0, h)),
                  _tail_spec(kt, LANES, head), _tail_spec(vt, LANES, head),
                  pl.BlockSpec((TQ, 1), lambda b, h, i, *_: (i, 0)),
                  pl.BlockSpec((nt, 1, TK), lambda b, h, i, *_: (0, 0, 0)),
                  pl.BlockSpec((1, TAIL), lambda b, h, i, *_: (0, 0)),
                  pl.BlockSpec((4, HEAD_DIM_A), lambda b, h, i, *_: (0, 0)),
                  pl.BlockSpec((1, LANES), lambda b, h, i, *_: (0, 0))],
        out_specs=pl.BlockSpec((None, TQ, LANES), lambda b, h, i, *_: (b, i, h)),
        scratch_shapes=[pltpu.VMEM((R, TK), _F32)] * 2 + [pltpu.VMEM((R, LANES), _F32)] * 3,
    )
    return pl.pallas_call(
        kern, grid_spec=grid_spec,
        out_shape=jax.ShapeDtypeStruct((B, Tq, VA_W), _MXU),
        compiler_params=_cparams(("parallel", "parallel", "parallel")),
        name="diff_attn",
    )(nk, nf, q, k, v, kt, vt, qc, kcm, kct, lam_vecs, subln_w.reshape(1, LANES))


def _dsa_kernel(nk_ref, qi_ref, kiwi_ref, qb_ref, kir_ref, kb_ref, vb_ref, kirt_ref, kbt_ref, vbt_ref,
                qc_ref, kcm_ref, kct_ref, kom_ref, kot_ref, o_ref,
                key_sc, keyt_sc, hi_sc, lo_sc, hit_sc, lot_sc, s0_sc, s1_sc, m_sc, l_sc, acc_sc,
                *, TQ, TK, topk):
    i = pl.program_id(1)
    nk = nk_ref[i]
    H = N_IDX_HEADS
    nblk = TK // LANES

    qi = qi_ref[...]
    head = lax.broadcasted_iota(jnp.int32, qi.shape, 1) // IDX_DIM
    zero = jnp.zeros_like(qi)
    qis = jnp.concatenate([jnp.where(head == h, qi, zero) for h in range(H)], axis=0)
    wi = kiwi_ref[:, IDX_DIM:IDX_DIM + H] * (H ** -0.5) * (IDX_DIM ** -0.5)
    wi_b = [jnp.broadcast_to(wi[:, h:h + 1], (TQ, LANES)) for h in range(H)]
    qc = qc_ref[...]

    def score_keys(lg, kc):
        n = lg.shape[1] // LANES
        sc = _tile_lanes(wi_b[0], n) * jnp.maximum(lg[0:TQ], 0.0)
        for h in range(1, H):
            sc = sc + _tile_lanes(wi_b[h], n) * jnp.maximum(lg[h * TQ:(h + 1) * TQ], 0.0)
        return _sortable(jnp.where(kc <= qc, sc, -jnp.inf))

    def split16(key):
        return (key >> 16).astype(jnp.int16), ((key & 0xFFFF) - _HALF).astype(jnp.int16)

    def fill(j, carry):
        lg = _nt_dot(qis, kir_ref[pl.ds(pl.multiple_of(j * TK, TK), TK), :])
        key = score_keys(lg, kcm_ref[j])
        key_sc[j] = key
        hi_sc[j], lo_sc[j] = split16(key)
        return carry

    lax.fori_loop(0, nk, fill, 0)
    key_t = score_keys(_nt_dot(qis, kirt_ref[...]), kct_ref[...])
    keyt_sc[...] = key_t
    hit_sc[...], lot_sc[...] = split16(key_t)

    def count(ind_main, ind_tail, dtype):
        def body(j, c):
            ind = ind_main(j)
            for b in range(nblk):
                c = c + ind[:, b * LANES:(b + 1) * LANES]
            return c
        c = lax.fori_loop(0, nk, body, jnp.zeros((TQ, LANES), dtype)) + ind_tail()
        return jnp.sum(c.astype(_F32), axis=1, keepdims=True)

    one16, zero16 = jnp.int16(1), jnp.int16(0)

    def lanes16(v):
        return jnp.broadcast_to(v, (TQ, LANES)).astype(jnp.int16)

    def count16(main_sc, tail_sc, t16, strict=False):
        tm = _tile_lanes(t16, nblk)
        cmp = (lambda a, b: a > b) if strict else (lambda a, b: a >= b)
        return count(lambda j: jnp.where(cmp(main_sc[j], tm), one16, zero16),
                     lambda: jnp.where(cmp(tail_sc[...], t16), one16, zero16), jnp.int16)

    def kth_half(main_sc, tail_sc, target):
        def step(it, t_u):
            cand_u = t_u | jnp.left_shift(jnp.int32(1), 15 - it)
            cnt = count16(main_sc, tail_sc, lanes16(cand_u - _HALF))
            return jnp.where(cnt >= target, cand_u, t_u)
        return lax.fori_loop(0, 16, step, jnp.zeros((TQ, 1), jnp.int32))

    kf = float(topk)
    t_hi = kth_half(hi_sc, hit_sc, kf) - _HALF
    t_hi16 = lanes16(t_hi)
    n_above = count16(hi_sc, hit_sc, t_hi16, strict=True)
    t_hi16m = _tile_lanes(t_hi16, nblk)
    low16 = jnp.int16(-_HALF)

    def bucket(j, carry):
        lo_sc[j] = jnp.where(hi_sc[j] == t_hi16m, lo_sc[j], low16)
        return carry

    lax.fori_loop(0, nk, bucket, 0)
    lot_sc[...] = jnp.where(hit_sc[...] == t_hi16, lot_sc[...], low16)
    t_lo = kth_half(lo_sc, lot_sc, kf - n_above)
    n_ge = n_above + count16(lo_sc, lot_sc, lanes16(t_lo - _HALF))
    kth = (t_hi << 16) | t_lo
    thr1 = jnp.maximum(kth, _KEY_NEG_INF + 1)
    thr = jnp.broadcast_to(thr1, (TQ, LANES))
    thr_m = _tile_lanes(thr, nblk)
    excess = jnp.broadcast_to((n_ge > kf) & (kth > _KEY_NEG_INF), (TQ, LANES))

    @pl.when(jnp.max(jnp.where(excess, 1.0, 0.0)) > 0.0)
    def _():
        def count32(pred_main, pred_tail):
            return count(lambda j: jnp.where(pred_main(j), 1.0, 0.0),
                         lambda: jnp.where(pred_tail(), 1.0, 0.0), _F32)

        need = kf - count32(lambda j: key_sc[j] > thr_m, lambda: keyt_sc[...] > thr)
        excess_m = _tile_lanes(excess, nblk)

        def tied_before(J):
            Jm = _tile_lanes(J, nblk)
            return count32(lambda j: (key_sc[j] == thr_m) & (kom_ref[j] < Jm),
                           lambda: (keyt_sc[...] == thr) & (kot_ref[...] < J))

        def search(it, J):
            cand = J | jnp.left_shift(jnp.int32(1), 13 - it)
            return jnp.where(tied_before(cand) < need, cand, J)

        J = lax.fori_loop(0, 14, search, jnp.zeros((TQ, LANES), jnp.int32))
        Jm = _tile_lanes(J, nblk)

        def drop(j, carry):
            kj = key_sc[j]
            key_sc[j] = jnp.where(excess_m & (kj == thr_m) & (kom_ref[j] > Jm), _KEY_NEG_INF, kj)
            return carry

        lax.fori_loop(0, nk, drop, 0)
        kt = keyt_sc[...]
        keyt_sc[...] = jnp.where(excess & (kt == thr) & (kot_ref[...] > J), _KEY_NEG_INF, kt)

    qs = jnp.concatenate([qb_ref[:, h * LANES:(h + 1) * LANES] for h in range(N_HEADS_B)], axis=0)
    m_sc[...] = jnp.full(m_sc.shape, _NEG, _F32)
    l_sc[...] = jnp.zeros(l_sc.shape, _F32)
    acc_sc[...] = jnp.zeros(acc_sc.shape, _F32)

    def masked_scores(keys, kb):
        bias = jnp.where(keys >= _tile_lanes(thr, keys.shape[1] // LANES), 0.0, _NEG)
        return _nt_dot(qs, kb) + jnp.concatenate([bias] * N_HEADS_B, axis=0)

    def scores(j):
        return masked_scores(key_sc[j], kb_ref[pl.ds(pl.multiple_of(j * TK, TK), TK), :])

    def values(j):
        return vb_ref[pl.ds(pl.multiple_of(j * TK, TK), TK), :]

    _attend_tiles(nk, scores, values, s0_sc, s1_sc, m_sc, l_sc, acc_sc)
    _softmax_step(masked_scores(keyt_sc[...], kbt_ref[...]), vbt_ref[...], m_sc, l_sc, acc_sc)
    o = acc_sc[...] / l_sc[...]
    for h in range(N_HEADS_B):
        o_ref[:, h * LANES:(h + 1) * LANES] = o[h * TQ:(h + 1) * TQ].astype(o_ref.dtype)


def _dsa(qi, kiwi, qb, kir, kb, vb, kirt, kbt, vbt, qchunk, kchunk, korder, topk, Tq, TQ, TK):
    B = qi.shape[0]
    L = Lmain = kir.shape[1]
    assert Tq % TQ == 0 and Lmain % TK == 0 and kchunk.shape[0] == Lmain + TAIL
    nq, nt = Tq // TQ, Lmain // TK
    nk = jnp.asarray(_num_key_tiles(qchunk[:Tq], kchunk[:Lmain], TQ, TK))
    qc = jnp.asarray(qchunk[:Tq].reshape(Tq, 1).astype(np.int32))
    main = lambda a: jnp.asarray(a[:Lmain].reshape(nt, 1, TK).astype(np.int32))
    tail = lambda a: jnp.asarray(a[Lmain:].reshape(1, TAIL).astype(np.int32))
    R = N_HEADS_B * TQ
    kern = functools.partial(_dsa_kernel, TQ=TQ, TK=TK, topk=topk)
    first = lambda *_: 0
    qrow = lambda w: pl.BlockSpec((None, TQ, w), lambda b, i, nk: (b, i, 0))
    krow = lambda w: pl.BlockSpec((None, L, w), lambda b, i, nk: (b, 0, 0),
                                  pipeline_mode=pl.Buffered(1))
    cm = pl.BlockSpec((nt, 1, TK), lambda b, i, nk: (0, 0, 0))
    ct = pl.BlockSpec((1, TAIL), lambda b, i, nk: (0, 0))
    grid_spec = pltpu.PrefetchScalarGridSpec(
        num_scalar_prefetch=1,
        grid=(B, nq),
        in_specs=[qrow(QI_W), qrow(LANES), qrow(QB_EXP_W), krow(QI_W), krow(KB_W), krow(KB_W),
                  _tail_spec(kirt, QI_W, first), _tail_spec(kbt, KB_W, first), _tail_spec(vbt, KB_W, first),
                  pl.BlockSpec((TQ, 1), lambda b, i, nk: (i, 0)), cm, ct, cm, ct],
        out_specs=qrow(QB_EXP_W),
        scratch_shapes=[pltpu.VMEM((nt, TQ, TK), jnp.int32), pltpu.VMEM((TQ, TAIL), jnp.int32),
                        pltpu.VMEM((nt, TQ, TK), jnp.int16), pltpu.VMEM((nt, TQ, TK), jnp.int16),
                        pltpu.VMEM((TQ, TAIL), jnp.int16), pltpu.VMEM((TQ, TAIL), jnp.int16),
                        pltpu.VMEM((R, TK), _F32), pltpu.VMEM((R, TK), _F32),
                        pltpu.VMEM((R, LANES), _F32), pltpu.VMEM((R, LANES), _F32),
                        pltpu.VMEM((R, LANES), _F32)],
    )
    return pl.pallas_call(
        kern, grid_spec=grid_spec,
        out_shape=jax.ShapeDtypeStruct((B, Tq, QB_EXP_W), _MXU),
        compiler_params=_cparams(("parallel", "parallel")),
        name="dsa",
    )(nk, qi, kiwi, qb, kir, kb, vb, kirt, kbt, vbt, qc,
      main(kchunk), tail(kchunk), main(korder), tail(korder))


def _post_kernel(x_ref, oa_ref, ob_ref, ga_ref, gb_ref, wpa_ref, wpb_ref, wo_ref, g_ref,
                 wr_ref, br_ref, x1_o, h2_o, route_o):
    pa = jnp.dot(oa_ref[...], wpa_ref[...], preferred_element_type=_F32)
    pb = jnp.dot(ob_ref[...], wpb_ref[...], preferred_element_type=_F32)
    merged = ga_ref[...] * pa + gb_ref[...] * pb
    x1 = x_ref[...] + jnp.dot(merged.astype(wo_ref.dtype), wo_ref[...], preferred_element_type=_F32)
    x1_o[...] = x1
    ms = jnp.mean(x1 * x1, axis=-1, keepdims=True)
    h2 = (x1 * lax.rsqrt(ms + RMS_EPS) * g_ref[...]).astype(h2_o.dtype)
    h2_o[...] = h2
    cur = jnp.dot(h2, wr_ref[...], preferred_element_type=_F32) + br_ref[...]
    lane = lax.broadcasted_iota(jnp.int32, cur.shape, 1).astype(_F32)
    vals, idxs = [], []
    for _ in range(TOP_K_EXPERTS):
        m = jnp.max(cur, axis=1, keepdims=True)
        idx = jnp.min(jnp.where(cur == m, lane, float(LANES)), axis=1, keepdims=True)
        vals.append(m)
        idxs.append(idx)
        cur = jnp.where(lane == idx, -jnp.inf, cur)
    es = [jnp.exp(v - vals[0]) for v in vals]
    denom = es[0] + es[1] + es[2] + es[3]
    route = jnp.zeros(cur.shape, _F32)
    for k in range(TOP_K_EXPERTS):
        route = jnp.where(lane == float(k), idxs[k], route)
        route = jnp.where(lane == float(TOP_K_EXPERTS + k), es[k] / denom, route)
    route_o[...] = route


def _post(x, oa, ob, ga, gb, w_pa, w_pb_exp, w_o, g_ffn, w_router, b_router, T, tm):
    B, _, D = x.shape
    assert T % tm == 0
    row = lambda w: pl.BlockSpec((None, tm, w), lambda b, i: (b, i, 0))
    wr = jnp.concatenate([w_router, jnp.zeros((D, LANES - N_EXPERTS), w_router.dtype)], axis=1).astype(_MXU)
    br = jnp.concatenate([b_router.astype(_F32), jnp.full((LANES - N_EXPERTS,), _NEG, _F32)]).reshape(1, LANES)
    return pl.pallas_call(
        _post_kernel,
        grid=(B, T // tm),
        in_specs=[row(D), row(VA_W), row(QB_EXP_W), row(D), row(D),
                  _const_spec((VA_W, D)), _const_spec((QB_EXP_W, D)), _const_spec((D, D)),
                  _const_spec((1, D)), _const_spec((D, LANES)), _const_spec((1, LANES))],
        out_specs=[row(D), row(D), row(LANES)],
        out_shape=[jax.ShapeDtypeStruct((B, T, D), _F32), jax.ShapeDtypeStruct((B, T, D), _MXU),
                   jax.ShapeDtypeStruct((B, T, LANES), _F32)],
        compiler_params=_cparams(("parallel", "parallel")),
        name="post",
    )(x, oa, ob, ga, gb, w_pa.astype(_MXU), w_pb_exp, w_o.astype(_MXU), g_ffn.reshape(1, D), wr, br)


def _moe_kernel(be_ref, nu_ref, x_ref, wgu_ref, bgu_ref, wdn_ref, bdn_ref, y_ref, wgu_sc, wdn_sc):
    i = pl.program_id(0)
    used = i < nu_ref[0]
    new_expert = (i == 0) | (be_ref[i] != be_ref[jnp.maximum(i - 1, 0)])

    @pl.when(used & new_expert)
    def _():
        wgu_sc[...] = wgu_ref[...].astype(wgu_sc.dtype)
        wdn_sc[...] = wdn_ref[...].astype(wdn_sc.dtype)

    @pl.when(used)
    def _():
        gu = jnp.dot(x_ref[...], wgu_sc[...], preferred_element_type=_F32) + bgu_ref[...]
        glu = jnp.minimum(gu[:, :D_EXPERT], SWIGLU_LIMIT)
        lin = jnp.clip(gu[:, D_EXPERT:], -SWIGLU_LIMIT, SWIGLU_LIMIT)
        act = glu * jax.nn.sigmoid(SWIGLU_ALPHA * glu) * (lin + 1.0)
        y_ref[...] = jnp.dot(act.astype(wdn_sc.dtype), wdn_sc[...],
                             preferred_element_type=_F32) + bdn_ref[...]

    @pl.when(jnp.logical_not(used))
    def _():
        y_ref[...] = jnp.zeros(y_ref.shape, y_ref.dtype)


def _moe(h2, route, w_gu, b_gu, w_dn, b_dn):
    T, D = h2.shape
    K = TOP_K_EXPERTS
    A = T * K
    NB = -(-A // MOE_BLOCK) + N_EXPERTS
    NS = NB * MOE_BLOCK
    e_flat = route[:, :K].astype(jnp.int32).reshape(-1)
    experts = jnp.arange(N_EXPERTS, dtype=jnp.int32)
    counts = jnp.sum((e_flat[:, None] == experts[None, :]).astype(jnp.int32), axis=0)
    blocks_per = (counts + MOE_BLOCK - 1) // MOE_BLOCK
    cum_blocks = jnp.cumsum(blocks_per)
    blk = jnp.arange(NB, dtype=jnp.int32)
    block_expert = jnp.minimum(jnp.sum((cum_blocks[None, :] <= blk[:, None]).astype(jnp.int32), axis=1),
                               N_EXPERTS - 1)
    n_used = cum_blocks[-1:].astype(jnp.int32)
    cum_pad = jnp.cumsum(blocks_per * MOE_BLOCK - counts)
    r = jnp.arange(NS - A, dtype=jnp.int32)
    empty_expert = jnp.sum((cum_pad[None, :] <= r[:, None]).astype(jnp.int32), axis=1)
    a = jnp.arange(A, dtype=jnp.int32)
    _, slot_tok, slot_src = lax.sort(
        (jnp.concatenate([2 * e_flat, 2 * empty_expert + 1]),
         jnp.concatenate([a // K, jnp.full((NS - A,), T, jnp.int32)]),
         jnp.concatenate([a, A + r])), num_keys=1, is_stable=True)
    x_pad = jnp.concatenate([h2, jnp.zeros((1, D), h2.dtype)], axis=0)
    x_slots = x_pad[slot_tok]
    y_slots = pl.pallas_call(
        _moe_kernel,
        grid_spec=pltpu.PrefetchScalarGridSpec(
            num_scalar_prefetch=2,
            grid=(NB,),
            in_specs=[pl.BlockSpec((MOE_BLOCK, D), lambda i, be, nu: (i, 0)),
                      pl.BlockSpec((None, D, 2 * D_EXPERT), lambda i, be, nu: (be[i], 0, 0)),
                      pl.BlockSpec((None, 1, 2 * D_EXPERT), lambda i, be, nu: (be[i], 0, 0)),
                      pl.BlockSpec((None, D_EXPERT, D), lambda i, be, nu: (be[i], 0, 0)),
                      pl.BlockSpec((None, 1, D), lambda i, be, nu: (be[i], 0, 0))],
            out_specs=pl.BlockSpec((MOE_BLOCK, D), lambda i, be, nu: (i, 0)),
            scratch_shapes=[pltpu.VMEM((D, 2 * D_EXPERT), _MXU), pltpu.VMEM((D_EXPERT, D), _MXU)],
        ),
        out_shape=jax.ShapeDtypeStruct((NB * MOE_BLOCK, D), _F32),
        compiler_params=_cparams(("arbitrary",)),
        name="moe",
    )(block_expert, n_used, x_slots, w_gu, b_gu.reshape(N_EXPERTS, 1, -1).astype(_F32),
      w_dn, b_dn.reshape(N_EXPERTS, 1, -1).astype(_F32))
    _, dest = lax.sort((slot_src, jnp.arange(NS, dtype=jnp.int32)), num_keys=1)
    dest = dest[:A].reshape(T, K)
    return [y_slots[dest[:, k]] for k in range(K)]


def _final_kernel(x_ref, y0_ref, y1_ref, y2_ref, y3_ref, route_ref, g_ref, o_ref):
    K = TOP_K_EXPERTS
    x = x_ref[...]
    for k, y_ref in enumerate((y0_ref, y1_ref, y2_ref, y3_ref)):
        x = x + route_ref[:, K + k:K + k + 1] * y_ref[...]
    ms = jnp.mean(x * x, axis=-1, keepdims=True)
    o_ref[...] = x * lax.rsqrt(ms + RMS_EPS) * g_ref[...]


def _final(x1, ys, route, g_final, tm):
    T, D = x1.shape
    assert T % tm == 0
    row = pl.BlockSpec((tm, D), lambda i: (i, 0))
    return pl.pallas_call(
        _final_kernel, grid=(T // tm,),
        in_specs=[row] * 5 + [pl.BlockSpec((tm, LANES), lambda i: (i, 0)), _const_spec((1, D))],
        out_specs=row,
        out_shape=jax.ShapeDtypeStruct((T, D), _F32),
        compiler_params=_cparams(("parallel",)),
        name="final",
    )(x1, *ys, route, g_final.reshape(1, D))


def _group(x_rows, proj, keys, tails, qchunk, kchunk, korder, topk, Tq, tiles, weights, lam_init):
    (lam_vecs, subln_w, w_pa, w_pb_exp, w_o, g_ffn, w_router, b_router,
     w_gu, b_gu, w_dn, b_dn, g_final) = weights
    B = x_rows.shape[0]
    oa = _diff_attn(proj["qa"], keys["ka"], keys["va"], tails["ka"], tails["va"], qchunk, kchunk,
                    lam_vecs, subln_w, lam_init, Tq, tiles["tq_a"], tiles["tk"])
    ob = _dsa(proj["qi"], proj["wi"], proj["qb"], keys["kir"], keys["kb"], keys["vb"],
              tails["kir"], tails["kb"], tails["vb"], qchunk, kchunk, korder, topk, Tq,
              tiles["tq_b"], tiles["tk"])
    x1, h2, route = _post(x_rows, oa, ob, proj["ga"], proj["gb"], w_pa, w_pb_exp, w_o, g_ffn,
                          w_router, b_router, Tq, tiles["tm_post"])
    T = B * Tq
    route = route.reshape(T, LANES)
    ys = _moe(h2.reshape(T, D_MODEL), route, w_gu, b_gu, w_dn, b_dn)
    y = _final(x1.reshape(T, D_MODEL), ys, route, g_final, tiles["tm_post"])
    return y.reshape(B, Tq, D_MODEL)


_KEY_OPERANDS = ("ka", "va", "kb", "vb", "kir")


def kernel(x_prompt, x_sample, cache_a_k, cache_a_v, cache_b_k, cache_b_v, cache_idx_k,
           meta_tokens, g_mix, w_in, lambda_q1, lambda_k1, lambda_q2, lambda_k2, subln_w,
           w_pa, w_pb, w_o, g_ffn, w_router, b_router, w_gu, b_gu, w_dn, b_dn, g_final):
    B, S, D = x_prompt.shape
    Bs, S_dec, _ = x_sample.shape
    P = cache_a_k.shape[2]
    depth = g_mix.shape[0]
    assert depth == 1
    l = 0
    lam_init = 0.8 - 0.6 * math.exp(-0.3 * l)
    w_lay = _layout_w_in(w_in[l])
    w_pb_exp = _layout_w_pb(w_pb[l])
    lam_vecs = jnp.stack([lambda_q1[l], lambda_k1[l], lambda_q2[l], lambda_k2[l]]).astype(_F32)
    weights = (lam_vecs, subln_w[l], w_pa[l], w_pb_exp, w_o[l], g_ffn[l], w_router[l], b_router[l],
               w_gu[l], b_gu[l], w_dn[l], b_dn[l], g_final)

    r = np.arange(S + TAIL)
    chunk_p = np.where(r < S, r // CHUNK, np.where(r < S + N_META, -1, _BIG_CHUNK)).astype(np.int64)
    order_p = np.where(r < S, r + N_META, r - S).astype(np.int32)
    tiles_p = dict(tq_a=min(512, S), tq_b=min(128, S), tk=min(512, S), tm_post=min(512, S))
    meta_tile = jnp.concatenate([meta_tokens.astype(x_prompt.dtype),
                                 jnp.zeros((TAIL - N_META, D), x_prompt.dtype)])[None]
    pos_m = np.where(np.arange(TAIL) < N_META, np.arange(TAIL), 0).astype(np.int32)
    pm = _project(meta_tile, jnp.asarray(pos_m), g_mix[l], w_lay, TAIL)
    meta_rows = {n: pm[n][0, :N_META] for n in ("ka_f", "va_f", "kb_f", "vb_f", "kiwi")}
    pp = _project(x_prompt, jnp.asarray(N_META + np.arange(S, dtype=np.int32)), g_mix[l], w_lay,
                  tiles_p["tm_post"], head_rows=meta_rows)
    y_prompt = _group(x_prompt, pp, pp, pm, chunk_p[:S], chunk_p, order_p, min(TOPK_MAX, S // 4), S,
                      tiles_p, weights, lam_init)

    def rows_p(name, shape, width=None):
        return pp[name][:, :, :width].reshape((1, B, N_META + S) + shape)

    pos_s = (P + np.arange(S_dec)).astype(np.int32)
    ps = _project(x_sample, jnp.asarray(pos_s), g_mix[l], w_lay, S_dec)
    r = np.arange(P + TAIL)
    chunk_s = np.where(r < P + S_dec, r // CHUNK, _BIG_CHUNK).astype(np.int64)
    qchunk_s = (pos_s // CHUNK).astype(np.int64)
    cached = dict(ka=cache_a_k[l], va=cache_a_v[l], kb=cache_b_k[l], vb=cache_b_v[l],
                  kir=jnp.tile(cache_idx_k[l], (1, 1, N_IDX_HEADS)))
    keys_s = {n: cached[n].reshape(Bs, P, -1).astype(_MXU) for n in _KEY_OPERANDS}
    tails_s = {n: jnp.concatenate([ps[n], jnp.zeros((Bs, TAIL - S_dec, ps[n].shape[-1]), _MXU)], axis=1)
               for n in _KEY_OPERANDS}
    tiles_s = dict(tq_a=S_dec, tq_b=S_dec, tk=min(512, P), tm_post=S_dec)
    y_sample = _group(x_sample, ps, keys_s, tails_s, qchunk_s, chunk_s, r.astype(np.int32),
                      min(TOPK_MAX, (P + S_dec) // 4), S_dec, tiles_s, weights, lam_init)

    def rows_s(name, shape, width=None):
        return ps[name][:, :, :width].reshape((1, Bs, S_dec) + shape)

    ha, hb = (N_HEADS_A, 2 * HEAD_DIM_A), (N_KV_B, HEAD_DIM_B)
    return (y_prompt, y_sample,
            rows_p("ka_f", ha), rows_p("va_f", ha), rows_p("kb_f", hb), rows_p("vb_f", hb),
            rows_p("kiwi", (IDX_DIM,), IDX_DIM),
            rows_s("ka_f", ha), rows_s("va_f", ha), rows_s("kb_f", hb), rows_s("vb_f", hb),
            rows_s("kiwi", (IDX_DIM,), IDX_DIM))
```

```python
import functools
import math

import numpy as np
import jax
import jax.numpy as jnp
from jax import lax
from jax.experimental import pallas as pl
from jax.experimental.pallas import tpu as pltpu

D_MODEL = 1024
CHUNK = 64
N_META = 16
ROPE_THETA = 500000.0
RMS_EPS = 1e-6
N_HEADS_A = 4
HEAD_DIM_A = 64
N_HEADS_B = 8
N_KV_B = 2
HEAD_DIM_B = 64
N_IDX_HEADS = 8
IDX_DIM = 32
TOPK_MAX = 256
N_EXPERTS = 32
TOP_K_EXPERTS = 4
D_EXPERT = 1024
SWIGLU_LIMIT = 7.0
SWIGLU_ALPHA = 1.702
MOE_BLOCK = 512

QA_W = N_HEADS_A * 2 * HEAD_DIM_A
VA_W = QA_W
QB_W = N_HEADS_B * HEAD_DIM_B
KB_W = N_KV_B * HEAD_DIM_B
QI_W = N_IDX_HEADS * IDX_DIM
SPLITS = (QA_W, QA_W, VA_W, QB_W, KB_W, KB_W, QI_W, IDX_DIM, N_IDX_HEADS, D_MODEL, D_MODEL)

LANES = 128
TAIL = LANES
QB_EXP_W = N_HEADS_B * LANES

_SEC = {}
_off = 0
for _name, _w in (("qa", QA_W), ("ka", QA_W), ("va", VA_W), ("qb", QB_EXP_W), ("kb", KB_W),
                  ("vb", KB_W), ("qi", QI_W), ("kiwi", LANES), ("kir", QI_W),
                  ("ga", D_MODEL), ("gb", D_MODEL)):
    _SEC[_name] = (_off, _off + _w)
    _off += _w
W_LAYOUT = _off

_MXU = jnp.bfloat16
_F32 = jnp.float32
_NEG = -1e30
_BIG_CHUNK = 2 ** 30
_INT_MIN = -2 ** 31
_LOG2E = math.log2(math.e)
_HALF = 2 ** 15
_KEY_NEG_INF = int(np.int32(np.uint32(0xFF800000) ^ np.uint32(0x7FFFFFFF)))
_VMEM_LIMIT = 56 * 1024 * 1024


def _cparams(sem):
    return pltpu.CompilerParams(dimension_semantics=sem, vmem_limit_bytes=_VMEM_LIMIT)


def _const_spec(shape):
    nd = len(shape)
    return pl.BlockSpec(shape, lambda *_: (0,) * nd)


def _nt_dot(a, b):
    return lax.dot_general(a, b, (((1,), (1,)), ((), ())), preferred_element_type=_F32)


def _sortable(x):
    b = lax.bitcast_convert_type(x, jnp.int32)
    return b ^ ((b >> 31) & 0x7FFFFFFF)


def _rope_block(z, cos, sin_lo, sin_hi, half):
    return (z * cos + pltpu.roll(z, LANES - half, 1) * sin_lo + pltpu.roll(z, half, 1) * sin_hi)


def _proj_kernel(x_ref, g_ref, w_ref, tab_ref, *refs):
    (qa_o, kaf_o, kab_o, vaf_o, vab_o, qb_o, kbf_o, kbb_o, vbf_o, vbb_o,
     qi_o, kiwi_o, kir_o, ga_o, gb_o, wi_o) = refs[-16:]
    kaf_o, vaf_o, kbf_o, vbf_o, kiwi_o = (o.at[0] if len(o.shape) == 3 else o
                                          for o in (kaf_o, vaf_o, kbf_o, vbf_o, kiwi_o))
    x = x_ref[...]
    ms = jnp.mean(x * x, axis=-1, keepdims=True)
    h = (x * lax.rsqrt(ms + RMS_EPS) * g_ref[...]).astype(w_ref.dtype)

    def sec(name):
        a, b = _SEC[name]
        return jnp.dot(h, w_ref[:, a:b], preferred_element_type=_F32)

    t64 = tuple(tab_ref[:, k * LANES:(k + 1) * LANES] for k in (0, 1, 2))
    t32 = tuple(tab_ref[:, k * LANES:(k + 1) * LANES] for k in (3, 4, 5))
    tki = tuple(tab_ref[:, k * LANES:(k + 1) * LANES] for k in (6, 7, 8))

    def roped(z, tab, half):
        return [_rope_block(z[:, c * LANES:(c + 1) * LANES], *tab, half)
                for c in range(z.shape[1] // LANES)]

    scale_a = HEAD_DIM_A ** -0.5 * _LOG2E
    scale_b = HEAD_DIM_B ** -0.5 * _LOG2E
    for c, blk in enumerate(roped(sec("qa"), t64, 8)):
        qa_o[:, c * LANES:(c + 1) * LANES] = (blk * scale_a).astype(qa_o.dtype)
    for c, blk in enumerate(roped(sec("ka"), t64, 8)):
        kaf_o[:, c * LANES:(c + 1) * LANES] = blk
        kab_o[:, c * LANES:(c + 1) * LANES] = blk.astype(kab_o.dtype)
    va = sec("va")
    vaf_o[...] = va
    vab_o[...] = va.astype(vab_o.dtype)
    for c, blk in enumerate(roped(sec("qb"), t64, 8)):
        qb_o[:, c * LANES:(c + 1) * LANES] = (blk * scale_b).astype(qb_o.dtype)
    kb = roped(sec("kb"), t64, 8)[0]
    kbf_o[...] = kb
    kbb_o[...] = kb.astype(kbb_o.dtype)
    vb = sec("vb")
    vbf_o[...] = vb
    vbb_o[...] = vb.astype(vbb_o.dtype)
    for c, blk in enumerate(roped(sec("qi"), t32, 4)):
        qi_o[:, c * LANES:(c + 1) * LANES] = blk.astype(qi_o.dtype)
    kiwi = roped(sec("kiwi"), tki, 4)[0]
    kiwi_o[...] = kiwi
    wi_o[...] = kiwi
    for c, blk in enumerate(roped(sec("kir"), t32, 4)):
        kir_o[:, c * LANES:(c + 1) * LANES] = blk.astype(kir_o.dtype)
    ga_o[...] = jax.nn.sigmoid(sec("ga"))
    gb_o[...] = jax.nn.sigmoid(sec("gb"))


def _rope_tables(pos):
    def cs(d):
        rd = d // 4
        half = rd // 2
        inv_freq = ROPE_THETA ** (-jnp.arange(half, dtype=_F32) * 2.0 / rd)
        ang = pos.astype(_F32)[:, None] * inv_freq[None, :]
        return jnp.cos(ang), jnp.sin(ang), half

    T = pos.shape[0]

    def pattern(d, width):
        c, s, half = cs(d)
        one = jnp.ones((T, width - 2 * half), _F32)
        zero = lambda n: jnp.zeros((T, n), _F32)
        cos = jnp.concatenate([c, c, one], axis=1)
        lo = jnp.concatenate([-s, zero(width - half)], axis=1)
        hi = jnp.concatenate([zero(half), s, zero(width - 2 * half)], axis=1)
        return cos, lo, hi

    tabs = []
    for d, width in ((64, 64), (32, 32), (32, LANES)):
        tabs += [jnp.tile(t, (1, LANES // width)) for t in pattern(d, width)]
    return jnp.concatenate(tabs, axis=1)


def _layout_w_in(w_in):
    offs = np.cumsum((0,) + SPLITS)
    qa, ka, va, qb, kb, vb, qi, ki, wi, ga, gb = (w_in[:, offs[i]:offs[i + 1]] for i in range(11))
    D = w_in.shape[0]
    z64 = jnp.zeros((D, HEAD_DIM_B), w_in.dtype)
    per_kv = N_HEADS_B // N_KV_B
    qb_exp = []
    for hq in range(N_HEADS_B):
        col = qb[:, hq * HEAD_DIM_B:(hq + 1) * HEAD_DIM_B]
        qb_exp += [col, z64] if hq // per_kv == 0 else [z64, col]
    kiwi = jnp.concatenate([ki, wi, jnp.zeros((D, LANES - IDX_DIM - N_IDX_HEADS), w_in.dtype)], axis=1)
    kir = jnp.tile(ki, (1, N_IDX_HEADS))
    w = jnp.concatenate([qa, ka, va] + qb_exp + [kb, vb, qi, kiwi, kir, ga, gb], axis=1)
    assert w.shape[1] == W_LAYOUT
    return w.astype(_MXU)


def _layout_w_pb(w_pb):
    D = w_pb.shape[1]
    z64 = jnp.zeros((HEAD_DIM_B, D), w_pb.dtype)
    per_kv = N_HEADS_B // N_KV_B
    rows = []
    for hq in range(N_HEADS_B):
        r = w_pb[hq * HEAD_DIM_B:(hq + 1) * HEAD_DIM_B]
        rows += [r, z64] if hq // per_kv == 0 else [z64, r]
    return jnp.concatenate(rows, axis=0).astype(_MXU)


def _project(x, pos, g_mix, w_lay, tm, head_rows=None):
    B, T, D = x.shape
    assert T % tm == 0
    off = next(iter(head_rows.values())).shape[0] if head_rows else 0
    total = off + T
    assert off % 8 == 0
    tab = _rope_tables(pos)
    row = lambda w: pl.BlockSpec((None, tm, w), lambda b, i: (b, i, 0))
    shifted = lambda w: pl.BlockSpec((pl.Element(1), pl.Element(tm), pl.Element(w)),
                                     lambda b, i: (b, pl.multiple_of(off + i * tm, 8), 0))
    outs = [("qa", QA_W, _MXU), ("ka_f", QA_W, _F32), ("ka", QA_W, _MXU), ("va_f", VA_W, _F32),
            ("va", VA_W, _MXU), ("qb", QB_EXP_W, _MXU), ("kb_f", KB_W, _F32), ("kb", KB_W, _MXU),
            ("vb_f", KB_W, _F32), ("vb", KB_W, _MXU), ("qi", QI_W, _MXU), ("kiwi", LANES, _F32),
            ("kir", QI_W, _MXU), ("ga", D_MODEL, _F32), ("gb", D_MODEL, _F32), ("wi", LANES, _F32)]
    is_cache = lambda n: n.endswith("_f") or n == "kiwi"
    n_in = 4
    heads, aliases = [], {}
    for k, (n, w, _) in enumerate(outs):
        if head_rows and is_cache(n):
            rows = jnp.broadcast_to(head_rows[n][None], (B, off, w))
            heads.append(jnp.concatenate([rows, jnp.zeros((B, T, w), _F32)], axis=1))
            aliases[n_in + len(heads) - 1] = k
    res = pl.pallas_call(
        _proj_kernel,
        grid=(B, T // tm),
        in_specs=[row(D), _const_spec((1, D)),
                  pl.BlockSpec((D, W_LAYOUT), lambda b, i: (0, 0), pipeline_mode=pl.Buffered(1)),
                  pl.BlockSpec((tm, 9 * LANES), lambda b, i: (i, 0))]
                 + [pl.BlockSpec(memory_space=pl.ANY)] * len(heads),
        out_specs=[shifted(w) if (is_cache(n) and off) else row(w) for n, w, _ in outs],
        out_shape=[jax.ShapeDtypeStruct((B, total if is_cache(n) else T, w), dt) for n, w, dt in outs],
        input_output_aliases=aliases,
        compiler_params=_cparams(("parallel", "parallel")),
        name="proj",
    )(x, g_mix.reshape(1, D), w_lay, tab, *heads)
    return {n: r for (n, _, _), r in zip(outs, res)}


def _tile_lanes(v, n):
    return v if n == 1 else jnp.concatenate([v] * n, axis=1)


def _softmax_step(s, v, m_sc, l_sc, acc_sc):
    m_prev = m_sc[...]
    m_new = jnp.maximum(m_prev, jnp.max(s, axis=1, keepdims=True))
    alpha = jnp.exp2(m_prev - m_new)
    p = jnp.exp2(s - _tile_lanes(m_new, s.shape[1] // LANES))
    l_sc[...] = alpha * l_sc[...] + jnp.sum(p, axis=1, keepdims=True)
    acc_sc[...] = alpha * acc_sc[...] + jnp.dot(p.astype(v.dtype), v, preferred_element_type=_F32)
    m_sc[...] = m_new


def _attend_tiles(n, scores, values, s0_sc, s1_sc, m_sc, l_sc, acc_sc):
    s0_sc[...] = scores(0)

    def pair(p, carry):
        j = 2 * p
        s1_sc[...] = scores(j + 1)
        _softmax_step(s0_sc[...], values(j), m_sc, l_sc, acc_sc)
        s0_sc[...] = scores(jnp.minimum(j + 2, n - 1))
        _softmax_step(s1_sc[...], values(j + 1), m_sc, l_sc, acc_sc)
        return carry

    lax.fori_loop(0, n // 2, pair, 0)

    @pl.when(n % 2 == 1)
    def _():
        _softmax_step(s0_sc[...], values(n - 1), m_sc, l_sc, acc_sc)


def _num_full_tiles(qchunk, kchunk_main, TQ, TK):
    last = kchunk_main.reshape(-1, TK).max(axis=1)
    qmin = qchunk.reshape(-1, TQ).min(axis=1)
    full = last[None, :] <= qmin[:, None]
    return np.cumprod(full, axis=1).sum(axis=1).astype(np.int32)


def _num_key_tiles(qchunk, kchunk_main, TQ, TK):
    first = kchunk_main.reshape(-1, TK)[:, 0]
    qmax = qchunk.reshape(-1, TQ).max(axis=1)
    return (first[None, :] <= qmax[:, None]).sum(axis=1).astype(np.int32)


def _diff_attn_kernel(nk_ref, nf_ref, q_ref, k_ref, v_ref, kt_ref, vt_ref, qc_ref, kcm_ref, kct_ref,
                      lam_ref, sub_ref, o_ref, s0_sc, s1_sc, m_sc, l_sc, acc_sc, *, TQ, TK, lam_init):
    i = pl.program_id(2)
    q = q_ref[...]
    lane = lax.broadcasted_iota(jnp.int32, q.shape, 1)
    zero = jnp.zeros_like(q)
    qs = jnp.concatenate([jnp.where(lane < HEAD_DIM_A, q, zero),
                          jnp.where(lane >= HEAD_DIM_A, q, zero)], axis=0)
    qc = qc_ref[...]
    qc2 = jnp.concatenate([qc, qc], axis=0)
    m_sc[...] = jnp.full(m_sc.shape, _NEG, _F32)
    l_sc[...] = jnp.zeros(l_sc.shape, _F32)
    acc_sc[...] = jnp.zeros(acc_sc.shape, _F32)

    def keys(j):
        return k_ref[pl.ds(pl.multiple_of(j * TK, TK), TK), :]

    def values(j):
        return v_ref[pl.ds(pl.multiple_of(j * TK, TK), TK), :]

    def masked_scores(k, kc):
        return jnp.where(kc <= qc2, _nt_dot(qs, k), _NEG)

    _attend_tiles(nf_ref[i], lambda j: _nt_dot(qs, keys(j)), values,
                  s0_sc, s1_sc, m_sc, l_sc, acc_sc)

    def diagonal(j, carry):
        _softmax_step(masked_scores(keys(j), kcm_ref[j]), values(j), m_sc, l_sc, acc_sc)
        return carry

    lax.fori_loop(nf_ref[i], nk_ref[i], diagonal, 0)
    _softmax_step(masked_scores(kt_ref[...], kct_ref[...]), vt_ref[...], m_sc, l_sc, acc_sc)

    lam = (jnp.exp(jnp.sum(lam_ref[0:1, :] * lam_ref[1:2, :], axis=1, keepdims=True))
           - jnp.exp(jnp.sum(lam_ref[2:3, :] * lam_ref[3:4, :], axis=1, keepdims=True))
           + lam_init)
    o = acc_sc[...] / l_sc[...]
    d = o[:TQ] - lam * o[TQ:]
    ms = jnp.mean(d * d, axis=-1, keepdims=True)
    o_ref[...] = ((d * lax.rsqrt(ms + RMS_EPS) * sub_ref[...]) * (1.0 - lam_init)).astype(o_ref.dtype)


def _tail_spec(tail, width, index):
    per_batch = tail.shape[0] > 1
    return pl.BlockSpec((None, TAIL, width), lambda b, *rest: (b if per_batch else 0, 0, index(*rest)))


def _diff_attn(q, k, v, kt, vt, qchunk, kchunk, lam_vecs, subln_w, lam_init, Tq, TQ, TK):
    B = q.shape[0]
    Lmain = k.shape[1]
    L = Lmain
    assert Tq % TQ == 0 and Lmain % TK == 0 and kchunk.shape[0] == Lmain + TAIL
    nq, nt = Tq // TQ, Lmain // TK
    nk = jnp.asarray(_num_key_tiles(qchunk[:Tq], kchunk[:Lmain], TQ, TK))
    nf = jnp.asarray(_num_full_tiles(qchunk[:Tq], kchunk[:Lmain], TQ, TK))
    qc = jnp.asarray(qchunk[:Tq].reshape(Tq, 1).astype(np.int32))
    kcm = jnp.asarray(kchunk[:Lmain].reshape(nt, 1, TK).astype(np.int32))
    kct = jnp.asarray(kchunk[Lmain:].reshape(1, TAIL).astype(np.int32))
    R = 2 * TQ
    kern = functools.partial(_diff_attn_kernel, TQ=TQ, TK=TK, lam_init=lam_init)
    head = lambda h, i, *_: h
    grid_spec = pltpu.PrefetchScalarGridSpec(
        num_scalar_prefetch=2,
        grid=(B, N_HEADS_A, nq),
        in_specs=[pl.BlockSpec((None, TQ, LANES), lambda b, h, i, *_: (b, i, h)),
                  pl.BlockSpec((None, L, LANES), lambda b, h, i, *_: (b, 0, h)),
                  pl.BlockSpec((None, L, LANES), lambda b, h, i, *_: (b, 0, h)),
                  _tail_spec(kt, LANES, head), _tail_spec(vt, LANES, head),
                  pl.BlockSpec((TQ, 1), lambda b, h, i, *_: (i, 0)),
                  pl.BlockSpec((nt, 1, TK), lambda b, h, i, *_: (0, 0, 0)),
                  pl.BlockSpec((1, TAIL), lambda b, h, i, *_: (0, 0)),
                  pl.BlockSpec((4, HEAD_DIM_A), lambda b, h, i, *_: (0, 0)),
                  pl.BlockSpec((1, LANES), lambda b, h, i, *_: (0, 0))],
        out_specs=pl.BlockSpec((None, TQ, LANES), lambda b, h, i, *_: (b, i, h)),
        scratch_shapes=[pltpu.VMEM((R, TK), _F32)] * 2 + [pltpu.VMEM((R, LANES), _F32)] * 3,
    )
    return pl.pallas_call(
        kern, grid_spec=grid_spec,
        out_shape=jax.ShapeDtypeStruct((B, Tq, VA_W), _MXU),
        compiler_params=_cparams(("parallel", "parallel", "parallel")),
        name="diff_attn",
    )(nk, nf, q, k, v, kt, vt, qc, kcm, kct, lam_vecs, subln_w.reshape(1, LANES))


def _dsa_kernel(nk_ref, qi_ref, kiwi_ref, qb_ref, kir_ref, kb_ref, vb_ref, kirt_ref, kbt_ref, vbt_ref,
                qc_ref, kcm_ref, kct_ref, kom_ref, kot_ref, o_ref,
                key_sc, keyt_sc, hi_sc, lo_sc, hit_sc, lot_sc, s0_sc, s1_sc, m_sc, l_sc, acc_sc,
                *, TQ, TK, topk):
    i = pl.program_id(1)
    nk = nk_ref[i]
    H = N_IDX_HEADS
    nblk = TK // LANES

    qi = qi_ref[...]
    head = lax.broadcasted_iota(jnp.int32, qi.shape, 1) // IDX_DIM
    zero = jnp.zeros_like(qi)
    qis = jnp.concatenate([jnp.where(head == h, qi, zero) for h in range(H)], axis=0)
    wi = kiwi_ref[:, IDX_DIM:IDX_DIM + H] * (H ** -0.5) * (IDX_DIM ** -0.5)
    wi_b = [jnp.broadcast_to(wi[:, h:h + 1], (TQ, LANES)) for h in range(H)]
    qc = qc_ref[...]

    def score_keys(lg, kc):
        n = lg.shape[1] // LANES
        sc = _tile_lanes(wi_b[0], n) * jnp.maximum(lg[0:TQ], 0.0)
        for h in range(1, H):
            sc = sc + _tile_lanes(wi_b[h], n) * jnp.maximum(lg[h * TQ:(h + 1) * TQ], 0.0)
        return _sortable(jnp.where(kc <= qc, sc, -jnp.inf))

    def split16(key):
        return (key >> 16).astype(jnp.int16), ((key & 0xFFFF) - _HALF).astype(jnp.int16)

    def fill(j, carry):
        lg = _nt_dot(qis, kir_ref[pl.ds(pl.multiple_of(j * TK, TK), TK), :])
        key = score_keys(lg, kcm_ref[j])
        key_sc[j] = key
        hi_sc[j], lo_sc[j] = split16(key)
        return carry

    lax.fori_loop(0, nk, fill, 0)
    key_t = score_keys(_nt_dot(qis, kirt_ref[...]), kct_ref[...])
    keyt_sc[...] = key_t
    hit_sc[...], lot_sc[...] = split16(key_t)

    def count(ind_main, ind_tail, dtype):
        def body(j, c):
            ind = ind_main(j)
            for b in range(nblk):
                c = c + ind[:, b * LANES:(b + 1) * LANES]
            return c
        c = lax.fori_loop(0, nk, body, jnp.zeros((TQ, LANES), dtype)) + ind_tail()
        return jnp.sum(c.astype(_F32), axis=1, keepdims=True)

    one16, zero16 = jnp.int16(1), jnp.int16(0)

    def lanes16(v):
        return jnp.broadcast_to(v, (TQ, LANES)).astype(jnp.int16)

    def count16(main_sc, tail_sc, t16, strict=False):
        tm = _tile_lanes(t16, nblk)
        cmp = (lambda a, b: a > b) if strict else (lambda a, b: a >= b)
        return count(lambda j: jnp.where(cmp(main_sc[j], tm), one16, zero16),
                     lambda: jnp.where(cmp(tail_sc[...], t16), one16, zero16), jnp.int16)

    def kth_half(main_sc, tail_sc, target):
        def step(it, t_u):
            cand_u = t_u | jnp.left_shift(jnp.int32(1), 15 - it)
            cnt = count16(main_sc, tail_sc, lanes16(cand_u - _HALF))
            return jnp.where(cnt >= target, cand_u, t_u)
        return lax.fori_loop(0, 16, step, jnp.zeros((TQ, 1), jnp.int32))

    kf = float(topk)
    t_hi = kth_half(hi_sc, hit_sc, kf) - _HALF
    t_hi16 = lanes16(t_hi)
    n_above = count16(hi_sc, hit_sc, t_hi16, strict=True)
    t_hi16m = _tile_lanes(t_hi16, nblk)
    low16 = jnp.int16(-_HALF)

    def bucket(j, carry):
        lo_sc[j] = jnp.where(hi_sc[j] == t_hi16m, lo_sc[j], low16)
        return carry

    lax.fori_loop(0, nk, bucket, 0)
    lot_sc[...] = jnp.where(hit_sc[...] == t_hi16, lot_sc[...], low16)
    t_lo = kth_half(lo_sc, lot_sc, kf - n_above)
    n_ge = n_above + count16(lo_sc, lot_sc, lanes16(t_lo - _HALF))
    kth = (t_hi << 16) | t_lo
    thr1 = jnp.maximum(kth, _KEY_NEG_INF + 1)
    thr = jnp.broadcast_to(thr1, (TQ, LANES))
    thr_m = _tile_lanes(thr, nblk)
    excess = jnp.broadcast_to((n_ge > kf) & (kth > _KEY_NEG_INF), (TQ, LANES))

    @pl.when(jnp.max(jnp.where(excess, 1.0, 0.0)) > 0.0)
    def _():
        def count32(pred_main, pred_tail):
            return count(lambda j: jnp.where(pred_main(j), 1.0, 0.0),
                         lambda: jnp.where(pred_tail(), 1.0, 0.0), _F32)

        need = kf - count32(lambda j: key_sc[j] > thr_m, lambda: keyt_sc[...] > thr)
        excess_m = _tile_lanes(excess, nblk)

        def tied_before(J):
            Jm = _tile_lanes(J, nblk)
            return count32(lambda j: (key_sc[j] == thr_m) & (kom_ref[j] < Jm),
                           lambda: (keyt_sc[...] == thr) & (kot_ref[...] < J))

        def search(it, J):
            cand = J | jnp.left_shift(jnp.int32(1), 13 - it)
            return jnp.where(tied_before(cand) < need, cand, J)

        J = lax.fori_loop(0, 14, search, jnp.zeros((TQ, LANES), jnp.int32))
        Jm = _tile_lanes(J, nblk)

        def drop(j, carry):
            kj = key_sc[j]
            key_sc[j] = jnp.where(excess_m & (kj == thr_m) & (kom_ref[j] > Jm), _KEY_NEG_INF, kj)
            return carry

        lax.fori_loop(0, nk, drop, 0)
        kt = keyt_sc[...]
        keyt_sc[...] = jnp.where(excess & (kt == thr) & (kot_ref[...] > J), _KEY_NEG_INF, kt)

    qs = jnp.concatenate([qb_ref[:, h * LANES:(h + 1) * LANES] for h in range(N_HEADS_B)], axis=0)
    m_sc[...] = jnp.full(m_sc.shape, _NEG, _F32)
    l_sc[...] = jnp.zeros(l_sc.shape, _F32)
    acc_sc[...] = jnp.zeros(acc_sc.shape, _F32)

    def masked_scores(keys, kb):
        bias = jnp.where(keys >= _tile_lanes(thr, keys.shape[1] // LANES), 0.0, _NEG)
        return _nt_dot(qs, kb) + jnp.concatenate([bias] * N_HEADS_B, axis=0)

    def scores(j):
        return masked_scores(key_sc[j], kb_ref[pl.ds(pl.multiple_of(j * TK, TK), TK), :])

    def values(j):
        return vb_ref[pl.ds(pl.multiple_of(j * TK, TK), TK), :]

    _attend_tiles(nk, scores, values, s0_sc, s1_sc, m_sc, l_sc, acc_sc)
    _softmax_step(masked_scores(keyt_sc[...], kbt_ref[...]), vbt_ref[...], m_sc, l_sc, acc_sc)
    o = acc_sc[...] / l_sc[...]
    for h in range(N_HEADS_B):
        o_ref[:, h * LANES:(h + 1) * LANES] = o[h * TQ:(h + 1) * TQ].astype(o_ref.dtype)


def _dsa(qi, kiwi, qb, kir, kb, vb, kirt, kbt, vbt, qchunk, kchunk, korder, topk, Tq, TQ, TK):
    B = qi.shape[0]
    L = Lmain = kir.shape[1]
    assert Tq % TQ == 0 and Lmain % TK == 0 and kchunk.shape[0] == Lmain + TAIL
    nq, nt = Tq // TQ, Lmain // TK
    nk = jnp.asarray(_num_key_tiles(qchunk[:Tq], kchunk[:Lmain], TQ, TK))
    qc = jnp.asarray(qchunk[:Tq].reshape(Tq, 1).astype(np.int32))
    main = lambda a: jnp.asarray(a[:Lmain].reshape(nt, 1, TK).astype(np.int32))
    tail = lambda a: jnp.asarray(a[Lmain:].reshape(1, TAIL).astype(np.int32))
    R = N_HEADS_B * TQ
    kern = functools.partial(_dsa_kernel, TQ=TQ, TK=TK, topk=topk)
    first = lambda *_: 0
    qrow = lambda w: pl.BlockSpec((None, TQ, w), lambda b, i, nk: (b, i, 0))
    krow = lambda w: pl.BlockSpec((None, L, w), lambda b, i, nk: (b, 0, 0),
                                  pipeline_mode=pl.Buffered(1))
    cm = pl.BlockSpec((nt, 1, TK), lambda b, i, nk: (0, 0, 0))
    ct = pl.BlockSpec((1, TAIL), lambda b, i, nk: (0, 0))
    grid_spec = pltpu.PrefetchScalarGridSpec(
        num_scalar_prefetch=1,
        grid=(B, nq),
        in_specs=[qrow(QI_W), qrow(LANES), qrow(QB_EXP_W), krow(QI_W), krow(KB_W), krow(KB_W),
                  _tail_spec(kirt, QI_W, first), _tail_spec(kbt, KB_W, first), _tail_spec(vbt, KB_W, first),
                  pl.BlockSpec((TQ, 1), lambda b, i, nk: (i, 0)), cm, ct, cm, ct],
        out_specs=qrow(QB_EXP_W),
        scratch_shapes=[pltpu.VMEM((nt, TQ, TK), jnp.int32), pltpu.VMEM((TQ, TAIL), jnp.int32),
                        pltpu.VMEM((nt, TQ, TK), jnp.int16), pltpu.VMEM((nt, TQ, TK), jnp.int16),
                        pltpu.VMEM((TQ, TAIL), jnp.int16), pltpu.VMEM((TQ, TAIL), jnp.int16),
                        pltpu.VMEM((R, TK), _F32), pltpu.VMEM((R, TK), _F32),
                        pltpu.VMEM((R, LANES), _F32), pltpu.VMEM((R, LANES), _F32),
                        pltpu.VMEM((R, LANES), _F32)],
    )
    return pl.pallas_call(
        kern, grid_spec=grid_spec,
        out_shape=jax.ShapeDtypeStruct((B, Tq, QB_EXP_W), _MXU),
        compiler_params=_cparams(("parallel", "parallel")),
        name="dsa",
    )(nk, qi, kiwi, qb, kir, kb, vb, kirt, kbt, vbt, qc,
      main(kchunk), tail(kchunk), main(korder), tail(korder))


def _post_kernel(x_ref, oa_ref, ob_ref, ga_ref, gb_ref, wpa_ref, wpb_ref, wo_ref, g_ref,
                 wr_ref, br_ref, x1_o, h2_o, route_o):
    pa = jnp.dot(oa_ref[...], wpa_ref[...], preferred_element_type=_F32)
    pb = jnp.dot(ob_ref[...], wpb_ref[...], preferred_element_type=_F32)
    merged = ga_ref[...] * pa + gb_ref[...] * pb
    x1 = x_ref[...] + jnp.dot(merged.astype(wo_ref.dtype), wo_ref[...], preferred_element_type=_F32)
    x1_o[...] = x1
    ms = jnp.mean(x1 * x1, axis=-1, keepdims=True)
    h2 = (x1 * lax.rsqrt(ms + RMS_EPS) * g_ref[...]).astype(h2_o.dtype)
    h2_o[...] = h2
    cur = jnp.dot(h2, wr_ref[...], preferred_element_type=_F32) + br_ref[...]
    lane = lax.broadcasted_iota(jnp.int32, cur.shape, 1).astype(_F32)
    vals, idxs = [], []
    for _ in range(TOP_K_EXPERTS):
        m = jnp.max(cur, axis=1, keepdims=True)
        idx = jnp.min(jnp.where(cur == m, lane, float(LANES)), axis=1, keepdims=True)
        vals.append(m)
        idxs.append(idx)
        cur = jnp.where(lane == idx, -jnp.inf, cur)
    es = [jnp.exp(v - vals[0]) for v in vals]
    denom = es[0] + es[1] + es[2] + es[3]
    route = jnp.zeros(cur.shape, _F32)
    for k in range(TOP_K_EXPERTS):
        route = jnp.where(lane == float(k), idxs[k], route)
        route = jnp.where(lane == float(TOP_K_EXPERTS + k), es[k] / denom, route)
    route_o[...] = route


def _post(x, oa, ob, ga, gb, w_pa, w_pb_exp, w_o, g_ffn, w_router, b_router, T, tm):
    B, _, D = x.shape
    assert T % tm == 0
    row = lambda w: pl.BlockSpec((None, tm, w), lambda b, i: (b, i, 0))
    wr = jnp.concatenate([w_router, jnp.zeros((D, LANES - N_EXPERTS), w_router.dtype)], axis=1).astype(_MXU)
    br = jnp.concatenate([b_router.astype(_F32), jnp.full((LANES - N_EXPERTS,), _NEG, _F32)]).reshape(1, LANES)
    return pl.pallas_call(
        _post_kernel,
        grid=(B, T // tm),
        in_specs=[row(D), row(VA_W), row(QB_EXP_W), row(D), row(D),
                  _const_spec((VA_W, D)), _const_spec((QB_EXP_W, D)), _const_spec((D, D)),
                  _const_spec((1, D)), _const_spec((D, LANES)), _const_spec((1, LANES))],
        out_specs=[row(D), row(D), row(LANES)],
        out_shape=[jax.ShapeDtypeStruct((B, T, D), _F32), jax.ShapeDtypeStruct((B, T, D), _MXU),
                   jax.ShapeDtypeStruct((B, T, LANES), _F32)],
        compiler_params=_cparams(("parallel", "parallel")),
        name="post",
    )(x, oa, ob, ga, gb, w_pa.astype(_MXU), w_pb_exp, w_o.astype(_MXU), g_ffn.reshape(1, D), wr, br)


def _moe_kernel(be_ref, nu_ref, x_ref, wgu_ref, bgu_ref, wdn_ref, bdn_ref, y_ref, wgu_sc, wdn_sc):
    i = pl.program_id(0)
    used = i < nu_ref[0]
    new_expert = (i == 0) | (be_ref[i] != be_ref[jnp.maximum(i - 1, 0)])

    @pl.when(used & new_expert)
    def _():
        wgu_sc[...] = wgu_ref[...].astype(wgu_sc.dtype)
        wdn_sc[...] = wdn_ref[...].astype(wdn_sc.dtype)

    @pl.when(used)
    def _():
        gu = jnp.dot(x_ref[...], wgu_sc[...], preferred_element_type=_F32) + bgu_ref[...]
        glu = jnp.minimum(gu[:, :D_EXPERT], SWIGLU_LIMIT)
        lin = jnp.clip(gu[:, D_EXPERT:], -SWIGLU_LIMIT, SWIGLU_LIMIT)
        act = glu * jax.nn.sigmoid(SWIGLU_ALPHA * glu) * (lin + 1.0)
        y_ref[...] = jnp.dot(act.astype(wdn_sc.dtype), wdn_sc[...],
                             preferred_element_type=_F32) + bdn_ref[...]

    @pl.when(jnp.logical_not(used))
    def _():
        y_ref[...] = jnp.zeros(y_ref.shape, y_ref.dtype)


def _moe(h2, route, w_gu, b_gu, w_dn, b_dn):
    T, D = h2.shape
    K = TOP_K_EXPERTS
    A = T * K
    NB = -(-A // MOE_BLOCK) + N_EXPERTS
    NS = NB * MOE_BLOCK
    e_flat = route[:, :K].astype(jnp.int32).reshape(-1)
    experts = jnp.arange(N_EXPERTS, dtype=jnp.int32)
    counts = jnp.sum((e_flat[:, None] == experts[None, :]).astype(jnp.int32), axis=0)
    blocks_per = (counts + MOE_BLOCK - 1) // MOE_BLOCK
    cum_blocks = jnp.cumsum(blocks_per)
    blk = jnp.arange(NB, dtype=jnp.int32)
    block_expert = jnp.minimum(jnp.sum((cum_blocks[None, :] <= blk[:, None]).astype(jnp.int32), axis=1),
                               N_EXPERTS - 1)
    n_used = cum_blocks[-1:].astype(jnp.int32)
    cum_pad = jnp.cumsum(blocks_per * MOE_BLOCK - counts)
    r = jnp.arange(NS - A, dtype=jnp.int32)
    empty_expert = jnp.sum((cum_pad[None, :] <= r[:, None]).astype(jnp.int32), axis=1)
    src_bits = (NS - 1).bit_length()
    assert (2 * N_EXPERTS + 2) << src_bits < 2 ** 31
    group = jnp.concatenate([2 * e_flat, 2 * empty_expert + 1])
    slot_src = lax.sort((group << src_bits) | jnp.arange(NS, dtype=jnp.int32)) & ((1 << src_bits) - 1)
    slot_tok = jnp.where(slot_src < A, slot_src // K, (slot_src - A) % T)
    x_slots = h2[slot_tok]
    y_slots = pl.pallas_call(
        _moe_kernel,
        grid_spec=pltpu.PrefetchScalarGridSpec(
            num_scalar_prefetch=2,
            grid=(NB,),
            in_specs=[pl.BlockSpec((MOE_BLOCK, D), lambda i, be, nu: (i, 0)),
                      pl.BlockSpec((None, D, 2 * D_EXPERT), lambda i, be, nu: (be[i], 0, 0)),
                      pl.BlockSpec((None, 1, 2 * D_EXPERT), lambda i, be, nu: (be[i], 0, 0)),
                      pl.BlockSpec((None, D_EXPERT, D), lambda i, be, nu: (be[i], 0, 0)),
                      pl.BlockSpec((None, 1, D), lambda i, be, nu: (be[i], 0, 0))],
            out_specs=pl.BlockSpec((MOE_BLOCK, D), lambda i, be, nu: (i, 0)),
            scratch_shapes=[pltpu.VMEM((D, 2 * D_EXPERT), _MXU), pltpu.VMEM((D_EXPERT, D), _MXU)],
        ),
        out_shape=jax.ShapeDtypeStruct((NB * MOE_BLOCK, D), _F32),
        compiler_params=_cparams(("arbitrary",)),
        name="moe",
    )(block_expert, n_used, x_slots, w_gu, b_gu.reshape(N_EXPERTS, 1, -1).astype(_F32),
      w_dn, b_dn.reshape(N_EXPERTS, 1, -1).astype(_F32))
    _, dest = lax.sort((slot_src, jnp.arange(NS, dtype=jnp.int32)), num_keys=1)
    dest = dest[:A].reshape(T, K)
    return [y_slots[dest[:, k]] for k in range(K)]


def _final_kernel(x_ref, y0_ref, y1_ref, y2_ref, y3_ref, route_ref, g_ref, o_ref):
    K = TOP_K_EXPERTS
    x = x_ref[...]
    for k, y_ref in enumerate((y0_ref, y1_ref, y2_ref, y3_ref)):
        x = x + route_ref[:, K + k:K + k + 1] * y_ref[...]
    ms = jnp.mean(x * x, axis=-1, keepdims=True)
    o_ref[...] = x * lax.rsqrt(ms + RMS_EPS) * g_ref[...]


def _final(x1, ys, route, g_final, tm):
    T, D = x1.shape
    assert T % tm == 0
    row = pl.BlockSpec((tm, D), lambda i: (i, 0))
    return pl.pallas_call(
        _final_kernel, grid=(T // tm,),
        in_specs=[row] * 5 + [pl.BlockSpec((tm, LANES), lambda i: (i, 0)), _const_spec((1, D))],
        out_specs=row,
        out_shape=jax.ShapeDtypeStruct((T, D), _F32),
        compiler_params=_cparams(("parallel",)),
        name="final",
    )(x1, *ys, route, g_final.reshape(1, D))


def _group(x_rows, proj, keys, tails, qchunk, kchunk, korder, topk, Tq, tiles, weights, lam_init):
    (lam_vecs, subln_w, w_pa, w_pb_exp, w_o, g_ffn, w_router, b_router,
     w_gu, b_gu, w_dn, b_dn, g_final) = weights
    B = x_rows.shape[0]
    oa = _diff_attn(proj["qa"], keys["ka"], keys["va"], tails["ka"], tails["va"], qchunk, kchunk,
                    lam_vecs, subln_w, lam_init, Tq, tiles["tq_a"], tiles["tk"])
    ob = _dsa(proj["qi"], proj["wi"], proj["qb"], keys["kir"], keys["kb"], keys["vb"],
              tails["kir"], tails["kb"], tails["vb"], qchunk, kchunk, korder, topk, Tq,
              tiles["tq_b"], tiles["tk"])
    x1, h2, route = _post(x_rows, oa, ob, proj["ga"], proj["gb"], w_pa, w_pb_exp, w_o, g_ffn,
                          w_router, b_router, Tq, tiles["tm_post"])
    T = B * Tq
    route = route.reshape(T, LANES)
    ys = _moe(h2.reshape(T, D_MODEL), route, w_gu, b_gu, w_dn, b_dn)
    y = _final(x1.reshape(T, D_MODEL), ys, route, g_final, tiles["tm_post"])
    return y.reshape(B, Tq, D_MODEL)


_KEY_OPERANDS = ("ka", "va", "kb", "vb", "kir")


def kernel(x_prompt, x_sample, cache_a_k, cache_a_v, cache_b_k, cache_b_v, cache_idx_k,
           meta_tokens, g_mix, w_in, lambda_q1, lambda_k1, lambda_q2, lambda_k2, subln_w,
           w_pa, w_pb, w_o, g_ffn, w_router, b_router, w_gu, b_gu, w_dn, b_dn, g_final):
    B, S, D = x_prompt.shape
    Bs, S_dec, _ = x_sample.shape
    P = cache_a_k.shape[2]
    depth = g_mix.shape[0]
    assert depth == 1
    l = 0
    lam_init = 0.8 - 0.6 * math.exp(-0.3 * l)
    w_lay = _layout_w_in(w_in[l])
    w_pb_exp = _layout_w_pb(w_pb[l])
    lam_vecs = jnp.stack([lambda_q1[l], lambda_k1[l], lambda_q2[l], lambda_k2[l]]).astype(_F32)
    weights = (lam_vecs, subln_w[l], w_pa[l], w_pb_exp, w_o[l], g_ffn[l], w_router[l], b_router[l],
               w_gu[l], b_gu[l], w_dn[l], b_dn[l], g_final)

    r = np.arange(S + TAIL)
    chunk_p = np.where(r < S, r // CHUNK, np.where(r < S + N_META, -1, _BIG_CHUNK)).astype(np.int64)
    order_p = np.where(r < S, r + N_META, r - S).astype(np.int32)
    tiles_p = dict(tq_a=min(512, S), tq_b=min(128, S), tk=min(512, S), tm_post=min(512, S))
    meta_tile = jnp.concatenate([meta_tokens.astype(x_prompt.dtype),
                                 jnp.zeros((TAIL - N_META, D), x_prompt.dtype)])[None]
    pos_m = np.where(np.arange(TAIL) < N_META, np.arange(TAIL), 0).astype(np.int32)
    pm = _project(meta_tile, jnp.asarray(pos_m), g_mix[l], w_lay, TAIL)
    meta_rows = {n: pm[n][0, :N_META] for n in ("ka_f", "va_f", "kb_f", "vb_f", "kiwi")}
    pp = _project(x_prompt, jnp.asarray(N_META + np.arange(S, dtype=np.int32)), g_mix[l], w_lay,
                  tiles_p["tm_post"], head_rows=meta_rows)
    y_prompt = _group(x_prompt, pp, pp, pm, chunk_p[:S], chunk_p, order_p, min(TOPK_MAX, S // 4), S,
                      tiles_p, weights, lam_init)

    def rows_p(name, shape, width=None):
        return pp[name][:, :, :width].reshape((1, B, N_META + S) + shape)

    pos_s = (P + np.arange(S_dec)).astype(np.int32)
    ps = _project(x_sample, jnp.asarray(pos_s), g_mix[l], w_lay, S_dec)
    r = np.arange(P + TAIL)
    chunk_s = np.where(r < P + S_dec, r // CHUNK, _BIG_CHUNK).astype(np.int64)
    qchunk_s = (pos_s // CHUNK).astype(np.int64)
    cached = dict(ka=cache_a_k[l], va=cache_a_v[l], kb=cache_b_k[l], vb=cache_b_v[l],
                  kir=jnp.tile(cache_idx_k[l], (1, 1, N_IDX_HEADS)))
    keys_s = {n: cached[n].reshape(Bs, P, -1).astype(_MXU) for n in _KEY_OPERANDS}
    tails_s = {n: jnp.concatenate([ps[n], jnp.zeros((Bs, TAIL - S_dec, ps[n].shape[-1]), _MXU)], axis=1)
               for n in _KEY_OPERANDS}
    tiles_s = dict(tq_a=S_dec, tq_b=S_dec, tk=min(512, P), tm_post=S_dec)
    y_sample = _group(x_sample, ps, keys_s, tails_s, qchunk_s, chunk_s, r.astype(np.int32),
                      min(TOPK_MAX, (P + S_dec) // 4), S_dec, tiles_s, weights, lam_init)

    def rows_s(name, shape, width=None):
        return ps[name][:, :, :width].reshape((1, Bs, S_dec) + shape)

    ha, hb = (N_HEADS_A, 2 * HEAD_DIM_A), (N_KV_B, HEAD_DIM_B)
    return (y_prompt, y_sample,
            rows_p("ka_f", ha), rows_p("va_f", ha), rows_p("kb_f", hb), rows_p("vb_f", hb),
            rows_p("kiwi", (IDX_DIM,), IDX_DIM),
            rows_s("ka_f", ha), rows_s("va_f", ha), rows_s("kb_f", hb), rows_s("vb_f", hb),
            rows_s("kiwi", (IDX_DIM,), IDX_DIM))
```

```python
import functools
import math

import numpy as np
import jax
import jax.numpy as jnp
from jax import lax
from jax.experimental import pallas as pl
from jax.experimental.pallas import tpu as pltpu

D_MODEL = 1024
CHUNK = 64
N_META = 16
ROPE_THETA = 500000.0
RMS_EPS = 1e-6
N_HEADS_A = 4
HEAD_DIM_A = 64
N_HEADS_B = 8
N_KV_B = 2
HEAD_DIM_B = 64
N_IDX_HEADS = 8
IDX_DIM = 32
TOPK_MAX = 256
N_EXPERTS = 32
TOP_K_EXPERTS = 4
D_EXPERT = 1024
SWIGLU_LIMIT = 7.0
SWIGLU_ALPHA = 1.702
MOE_BLOCK = 512

QA_W = N_HEADS_A * 2 * HEAD_DIM_A
VA_W = QA_W
QB_W = N_HEADS_B * HEAD_DIM_B
KB_W = N_KV_B * HEAD_DIM_B
QI_W = N_IDX_HEADS * IDX_DIM
SPLITS = (QA_W, QA_W, VA_W, QB_W, KB_W, KB_W, QI_W, IDX_DIM, N_IDX_HEADS, D_MODEL, D_MODEL)

LANES = 128
TAIL = LANES
QB_EXP_W = N_HEADS_B * LANES

_SEC = {}
_off = 0
for _name, _w in (("qa", QA_W), ("ka", QA_W), ("va", VA_W), ("qb", QB_EXP_W), ("kb", KB_W),
                  ("vb", KB_W), ("qi", QI_W), ("kiwi", LANES), ("kir", QI_W),
                  ("ga", D_MODEL), ("gb", D_MODEL)):
    _SEC[_name] = (_off, _off + _w)
    _off += _w
W_LAYOUT = _off

_MXU = jnp.bfloat16
_F32 = jnp.float32
_NEG = -1e30
_BIG_CHUNK = 2 ** 30
_INT_MIN = -2 ** 31
_SUB16 = 16
_LOG2E = math.log2(math.e)
_HALF = 2 ** 15
_KEY_NEG_INF = int(np.int32(np.uint32(0xFF800000) ^ np.uint32(0x7FFFFFFF)))
_VMEM_LIMIT = 56 * 1024 * 1024


def _cparams(sem):
    return pltpu.CompilerParams(dimension_semantics=sem, vmem_limit_bytes=_VMEM_LIMIT)


def _const_spec(shape):
    nd = len(shape)
    return pl.BlockSpec(shape, lambda *_: (0,) * nd)


def _nt_dot(a, b):
    return lax.dot_general(a, b, (((1,), (1,)), ((), ())), preferred_element_type=_F32)


def _sortable(x):
    b = lax.bitcast_convert_type(x, jnp.int32)
    return b ^ ((b >> 31) & 0x7FFFFFFF)


def _rope_block(z, cos, sin_lo, sin_hi, half):
    return (z * cos + pltpu.roll(z, LANES - half, 1) * sin_lo + pltpu.roll(z, half, 1) * sin_hi)


def _proj_kernel(x_ref, g_ref, w_ref, tab_ref, *refs):
    (qa_o, kaf_o, kab_o, vaf_o, vab_o, qb_o, kbf_o, kbb_o, vbf_o, vbb_o,
     qi_o, kiwi_o, kir_o, ga_o, gb_o, wi_o) = refs[-16:]
    kaf_o, vaf_o, kbf_o, vbf_o, kiwi_o = (o.at[0] if len(o.shape) == 3 else o
                                          for o in (kaf_o, vaf_o, kbf_o, vbf_o, kiwi_o))
    x = x_ref[...]
    ms = jnp.mean(x * x, axis=-1, keepdims=True)
    h = (x * lax.rsqrt(ms + RMS_EPS) * g_ref[...]).astype(w_ref.dtype)

    def sec(name):
        a, b = _SEC[name]
        return jnp.dot(h, w_ref[:, a:b], preferred_element_type=_F32)

    t64 = tuple(tab_ref[:, k * LANES:(k + 1) * LANES] for k in (0, 1, 2))
    t32 = tuple(tab_ref[:, k * LANES:(k + 1) * LANES] for k in (3, 4, 5))
    tki = tuple(tab_ref[:, k * LANES:(k + 1) * LANES] for k in (6, 7, 8))

    def roped(z, tab, half):
        return [_rope_block(z[:, c * LANES:(c + 1) * LANES], *tab, half)
                for c in range(z.shape[1] // LANES)]

    scale_a = HEAD_DIM_A ** -0.5 * _LOG2E
    scale_b = HEAD_DIM_B ** -0.5 * _LOG2E
    for c, blk in enumerate(roped(sec("qa"), t64, 8)):
        qa_o[:, c * LANES:(c + 1) * LANES] = (blk * scale_a).astype(qa_o.dtype)
    for c, blk in enumerate(roped(sec("ka"), t64, 8)):
        kaf_o[:, c * LANES:(c + 1) * LANES] = blk
        kab_o[:, c * LANES:(c + 1) * LANES] = blk.astype(kab_o.dtype)
    va = sec("va")
    vaf_o[...] = va
    vab_o[...] = va.astype(vab_o.dtype)
    for c, blk in enumerate(roped(sec("qb"), t64, 8)):
        qb_o[:, c * LANES:(c + 1) * LANES] = (blk * scale_b).astype(qb_o.dtype)
    kb = roped(sec("kb"), t64, 8)[0]
    kbf_o[...] = kb
    kbb_o[...] = kb.astype(kbb_o.dtype)
    vb = sec("vb")
    vbf_o[...] = vb
    vbb_o[...] = vb.astype(vbb_o.dtype)
    for c, blk in enumerate(roped(sec("qi"), t32, 4)):
        qi_o[:, c * LANES:(c + 1) * LANES] = blk.astype(qi_o.dtype)
    kiwi = roped(sec("kiwi"), tki, 4)[0]
    kiwi_o[...] = kiwi
    wi_o[...] = kiwi
    for c, blk in enumerate(roped(sec("kir"), t32, 4)):
        kir_o[:, c * LANES:(c + 1) * LANES] = blk.astype(kir_o.dtype)
    ga_o[...] = jax.nn.sigmoid(sec("ga"))
    gb_o[...] = jax.nn.sigmoid(sec("gb"))


def _rope_tables(pos):
    def cs(d):
        rd = d // 4
        half = rd // 2
        inv_freq = ROPE_THETA ** (-jnp.arange(half, dtype=_F32) * 2.0 / rd)
        ang = pos.astype(_F32)[:, None] * inv_freq[None, :]
        return jnp.cos(ang), jnp.sin(ang), half

    T = pos.shape[0]

    def pattern(d, width):
        c, s, half = cs(d)
        one = jnp.ones((T, width - 2 * half), _F32)
        zero = lambda n: jnp.zeros((T, n), _F32)
        cos = jnp.concatenate([c, c, one], axis=1)
        lo = jnp.concatenate([-s, zero(width - half)], axis=1)
        hi = jnp.concatenate([zero(half), s, zero(width - 2 * half)], axis=1)
        return cos, lo, hi

    tabs = []
    for d, width in ((64, 64), (32, 32), (32, LANES)):
        tabs += [jnp.tile(t, (1, LANES // width)) for t in pattern(d, width)]
    return jnp.concatenate(tabs, axis=1)


def _layout_w_in(w_in):
    offs = np.cumsum((0,) + SPLITS)
    qa, ka, va, qb, kb, vb, qi, ki, wi, ga, gb = (w_in[:, offs[i]:offs[i + 1]] for i in range(11))
    D = w_in.shape[0]
    z64 = jnp.zeros((D, HEAD_DIM_B), w_in.dtype)
    per_kv = N_HEADS_B // N_KV_B
    qb_exp = []
    for hq in range(N_HEADS_B):
        col = qb[:, hq * HEAD_DIM_B:(hq + 1) * HEAD_DIM_B]
        qb_exp += [col, z64] if hq // per_kv == 0 else [z64, col]
    kiwi = jnp.concatenate([ki, wi, jnp.zeros((D, LANES - IDX_DIM - N_IDX_HEADS), w_in.dtype)], axis=1)
    kir = jnp.tile(ki, (1, N_IDX_HEADS))
    w = jnp.concatenate([qa, ka, va] + qb_exp + [kb, vb, qi, kiwi, kir, ga, gb], axis=1)
    assert w.shape[1] == W_LAYOUT
    return w.astype(_MXU)


def _layout_w_pb(w_pb):
    D = w_pb.shape[1]
    z64 = jnp.zeros((HEAD_DIM_B, D), w_pb.dtype)
    per_kv = N_HEADS_B // N_KV_B
    rows = []
    for hq in range(N_HEADS_B):
        r = w_pb[hq * HEAD_DIM_B:(hq + 1) * HEAD_DIM_B]
        rows += [r, z64] if hq // per_kv == 0 else [z64, r]
    return jnp.concatenate(rows, axis=0).astype(_MXU)


def _project(x, pos, g_mix, w_lay, tm, head_rows=None):
    B, T, D = x.shape
    assert T % tm == 0
    off = next(iter(head_rows.values())).shape[0] if head_rows else 0
    total = off + T
    assert off % 8 == 0
    tab = _rope_tables(pos)
    row = lambda w: pl.BlockSpec((None, tm, w), lambda b, i: (b, i, 0))
    shifted = lambda w: pl.BlockSpec((pl.Element(1), pl.Element(tm), pl.Element(w)),
                                     lambda b, i: (b, pl.multiple_of(off + i * tm, 8), 0))
    outs = [("qa", QA_W, _MXU), ("ka_f", QA_W, _F32), ("ka", QA_W, _MXU), ("va_f", VA_W, _F32),
            ("va", VA_W, _MXU), ("qb", QB_EXP_W, _MXU), ("kb_f", KB_W, _F32), ("kb", KB_W, _MXU),
            ("vb_f", KB_W, _F32), ("vb", KB_W, _MXU), ("qi", QI_W, _MXU), ("kiwi", LANES, _F32),
            ("kir", QI_W, _MXU), ("ga", D_MODEL, _F32), ("gb", D_MODEL, _F32), ("wi", LANES, _F32)]
    is_cache = lambda n: n.endswith("_f") or n == "kiwi"
    n_in = 4
    heads, aliases = [], {}
    for k, (n, w, _) in enumerate(outs):
        if head_rows and is_cache(n):
            rows = jnp.broadcast_to(head_rows[n][None], (B, off, w))
            heads.append(jnp.concatenate([rows, jnp.zeros((B, T, w), _F32)], axis=1))
            aliases[n_in + len(heads) - 1] = k
    res = pl.pallas_call(
        _proj_kernel,
        grid=(B, T // tm),
        in_specs=[row(D), _const_spec((1, D)),
                  pl.BlockSpec((D, W_LAYOUT), lambda b, i: (0, 0), pipeline_mode=pl.Buffered(1)),
                  pl.BlockSpec((tm, 9 * LANES), lambda b, i: (i, 0))]
                 + [pl.BlockSpec(memory_space=pl.ANY)] * len(heads),
        out_specs=[shifted(w) if (is_cache(n) and off) else row(w) for n, w, _ in outs],
        out_shape=[jax.ShapeDtypeStruct((B, total if is_cache(n) else T, w), dt) for n, w, dt in outs],
        input_output_aliases=aliases,
        compiler_params=_cparams(("parallel", "parallel")),
        name="proj",
    )(x, g_mix.reshape(1, D), w_lay, tab, *heads)
    return {n: r for (n, _, _), r in zip(outs, res)}


def _tile_lanes(v, n):
    return v if n == 1 else jnp.concatenate([v] * n, axis=1)


def _softmax_step(s, v, m_sc, l_sc, acc_sc):
    m_prev = m_sc[...]
    m_new = jnp.maximum(m_prev, jnp.max(s, axis=1, keepdims=True))
    alpha = jnp.exp2(m_prev - m_new)
    p = jnp.exp2(s - _tile_lanes(m_new, s.shape[1] // LANES))
    l_sc[...] = alpha * l_sc[...] + jnp.sum(p, axis=1, keepdims=True)
    acc_sc[...] = alpha * acc_sc[...] + jnp.dot(p.astype(v.dtype), v, preferred_element_type=_F32)
    m_sc[...] = m_new


def _attend_tiles(n, scores, values, s0_sc, s1_sc, m_sc, l_sc, acc_sc):
    s0_sc[...] = scores(0)

    def pair(p, carry):
        j = 2 * p
        s1_sc[...] = scores(j + 1)
        _softmax_step(s0_sc[...], values(j), m_sc, l_sc, acc_sc)
        s0_sc[...] = scores(jnp.minimum(j + 2, n - 1))
        _softmax_step(s1_sc[...], values(j + 1), m_sc, l_sc, acc_sc)
        return carry

    lax.fori_loop(0, n // 2, pair, 0)

    @pl.when(n % 2 == 1)
    def _():
        _softmax_step(s0_sc[...], values(n - 1), m_sc, l_sc, acc_sc)


def _num_full_tiles(qchunk, kchunk_main, TQ, TK):
    last = kchunk_main.reshape(-1, TK).max(axis=1)
    qmin = qchunk.reshape(-1, TQ).min(axis=1)
    full = last[None, :] <= qmin[:, None]
    return np.cumprod(full, axis=1).sum(axis=1).astype(np.int32)


def _num_key_tiles(qchunk, kchunk_main, TQ, TK):
    first = kchunk_main.reshape(-1, TK)[:, 0]
    qmax = qchunk.reshape(-1, TQ).max(axis=1)
    return (first[None, :] <= qmax[:, None]).sum(axis=1).astype(np.int32)


def _diff_attn_kernel(nk_ref, nf_ref, q_ref, k_ref, v_ref, kt_ref, vt_ref, qc_ref, kcm_ref, kct_ref,
                      lam_ref, sub_ref, o_ref, s0_sc, s1_sc, m_sc, l_sc, acc_sc, *, TQ, TK, lam_init):
    i = pl.program_id(2)
    q = q_ref[...]
    lane = lax.broadcasted_iota(jnp.int32, q.shape, 1)
    zero = jnp.zeros_like(q)
    qs = jnp.concatenate([jnp.where(lane < HEAD_DIM_A, q, zero),
                          jnp.where(lane >= HEAD_DIM_A, q, zero)], axis=0)
    qc = qc_ref[...]
    qc2 = jnp.concatenate([qc, qc], axis=0)
    m_sc[...] = jnp.full(m_sc.shape, _NEG, _F32)
    l_sc[...] = jnp.zeros(l_sc.shape, _F32)
    acc_sc[...] = jnp.zeros(acc_sc.shape, _F32)

    def keys(j):
        return k_ref[pl.ds(pl.multiple_of(j * TK, TK), TK), :]

    def values(j):
        return v_ref[pl.ds(pl.multiple_of(j * TK, TK), TK), :]

    def masked_scores(k, kc):
        return jnp.where(kc <= qc2, _nt_dot(qs, k), _NEG)

    _attend_tiles(nf_ref[i], lambda j: _nt_dot(qs, keys(j)), values,
                  s0_sc, s1_sc, m_sc, l_sc, acc_sc)

    def diagonal(j, carry):
        _softmax_step(masked_scores(keys(j), kcm_ref[j]), values(j), m_sc, l_sc, acc_sc)
        return carry

    lax.fori_loop(nf_ref[i], nk_ref[i], diagonal, 0)
    _softmax_step(masked_scores(kt_ref[...], kct_ref[...]), vt_ref[...], m_sc, l_sc, acc_sc)

    lam = (jnp.exp(jnp.sum(lam_ref[0:1, :] * lam_ref[1:2, :], axis=1, keepdims=True))
           - jnp.exp(jnp.sum(lam_ref[2:3, :] * lam_ref[3:4, :], axis=1, keepdims=True))
           + lam_init)
    o = acc_sc[...] / l_sc[...]
    d = o[:TQ] - lam * o[TQ:]
    ms = jnp.mean(d * d, axis=-1, keepdims=True)
    o_ref[...] = ((d * lax.rsqrt(ms + RMS_EPS) * sub_ref[...]) * (1.0 - lam_init)).astype(o_ref.dtype)


def _tail_spec(tail, width, index):
    per_batch = tail.shape[0] > 1
    return pl.BlockSpec((None, TAIL, width), lambda b, *rest: (b if per_batch else 0, 0, index(*rest)))


def _diff_attn(q, k, v, kt, vt, qchunk, kchunk, lam_vecs, subln_w, lam_init, Tq, TQ, TK):
    B = q.shape[0]
    Lmain = k.shape[1]
    L = Lmain
    assert Tq % TQ == 0 and Lmain % TK == 0 and kchunk.shape[0] == Lmain + TAIL
    nq, nt = Tq // TQ, Lmain // TK
    nk = jnp.asarray(_num_key_tiles(qchunk[:Tq], kchunk[:Lmain], TQ, TK))
    nf = jnp.asarray(_num_full_tiles(qchunk[:Tq], kchunk[:Lmain], TQ, TK))
    qc = jnp.asarray(qchunk[:Tq].reshape(Tq, 1).astype(np.int32))
    kcm = jnp.asarray(kchunk[:Lmain].reshape(nt, 1, TK).astype(np.int32))
    kct = jnp.asarray(kchunk[Lmain:].reshape(1, TAIL).astype(np.int32))
    R = 2 * TQ
    kern = functools.partial(_diff_attn_kernel, TQ=TQ, TK=TK, lam_init=lam_init)
    head = lambda h, i, *_: h
    grid_spec = pltpu.PrefetchScalarGridSpec(
        num_scalar_prefetch=2,
        grid=(B, N_HEADS_A, nq),
        in_specs=[pl.BlockSpec((None, TQ, LANES), lambda b, h, i, *_: (b, i, h)),
                  pl.BlockSpec((None, L, LANES), lambda b, h, i, *_: (b, 0, h)),
                  pl.BlockSpec((None, L, LANES), lambda b, h, i, *_: (b, 0, h)),
                  _tail_spec(kt, LANES, head), _tail_spec(vt, LANES, head),
                  pl.BlockSpec((TQ, 1), lambda b, h, i, *_: (i, 0)),
                  pl.BlockSpec((nt, 1, TK), lambda b, h, i, *_: (0, 0, 0)),
                  pl.BlockSpec((1, TAIL), lambda b, h, i, *_: (0, 0)),
                  pl.BlockSpec((4, HEAD_DIM_A), lambda b, h, i, *_: (0, 0)),
                  pl.BlockSpec((1, LANES), lambda b, h, i, *_: (0, 0))],
        out_specs=pl.BlockSpec((None, TQ, LANES), lambda b, h, i, *_: (b, i, h)),
        scratch_shapes=[pltpu.VMEM((R, TK), _F32)] * 2 + [pltpu.VMEM((R, LANES), _F32)] * 3,
    )
    return pl.pallas_call(
        kern, grid_spec=grid_spec,
        out_shape=jax.ShapeDtypeStruct((B, Tq, VA_W), _MXU),
        compiler_params=_cparams(("parallel", "parallel", "parallel")),
        name="diff_attn",
    )(nk, nf, q, k, v, kt, vt, qc, kcm, kct, lam_vecs, subln_w.reshape(1, LANES))


def _dsa_kernel(nk_ref, qi_ref, kiwi_ref, qb_ref, kir_ref, kb_ref, vb_ref, kirt_ref, kbt_ref, vbt_ref,
                qc_ref, kcm_ref, kct_ref, kom_ref, kot_ref, o_ref,
                key_sc, keyt_sc, hi_sc, lo_sc, hit_sc, lot_sc, s0_sc, s1_sc, m_sc, l_sc, acc_sc,
                *, TQ, TK, topk):
    i = pl.program_id(1)
    nk = nk_ref[i]
    H = N_IDX_HEADS
    nblk = TK // LANES

    qi = qi_ref[...]
    head = lax.broadcasted_iota(jnp.int32, qi.shape, 1) // IDX_DIM
    zero = jnp.zeros_like(qi)
    qis = jnp.concatenate([jnp.where(head == h, qi, zero) for h in range(H)], axis=0)
    wi = kiwi_ref[:, IDX_DIM:IDX_DIM + H] * (H ** -0.5) * (IDX_DIM ** -0.5)
    wi_b = [jnp.broadcast_to(wi[:, h:h + 1], (TQ, LANES)) for h in range(H)]
    qc = qc_ref[...]

    def score_keys(lg, kc):
        n = lg.shape[1] // LANES
        sc = _tile_lanes(wi_b[0], n) * jnp.maximum(lg[0:TQ], 0.0)
        for h in range(1, H):
            sc = sc + _tile_lanes(wi_b[h], n) * jnp.maximum(lg[h * TQ:(h + 1) * TQ], 0.0)
        return _sortable(jnp.where(kc <= qc, sc, -jnp.inf))

    def split16(key):
        return (key >> 16).astype(jnp.int16), ((key & 0xFFFF) - _HALF).astype(jnp.int16)

    def halves_by_key(key):
        hi, lo = split16(key.T)
        return hi.reshape(-1, _SUB16, TQ), lo.reshape(-1, _SUB16, TQ)

    def fill(j, carry):
        lg = _nt_dot(qis, kir_ref[pl.ds(pl.multiple_of(j * TK, TK), TK), :])
        key = score_keys(lg, kcm_ref[j])
        key_sc[j] = key
        hi_sc[j], lo_sc[j] = halves_by_key(key)
        return carry

    lax.fori_loop(0, nk, fill, 0)
    key_t = score_keys(_nt_dot(qis, kirt_ref[...]), kct_ref[...])
    keyt_sc[...] = key_t
    hit_sc[...], lot_sc[...] = halves_by_key(key_t)

    one16, zero16 = jnp.int16(1), jnp.int16(0)

    def lanes16(v):
        return jnp.broadcast_to(v, (_SUB16, TQ)).astype(jnp.int16)[None]

    def hits(pred):
        ind = jnp.where(pred, one16, zero16)
        parts = [ind[i] for i in range(ind.shape[0])]
        while len(parts) > 1:
            parts = [a + b for a, b in zip(parts[::2], parts[1::2])] + parts[len(parts) & ~1:]
        return parts[0]

    def count16(pred_main, pred_tail):
        def body(j, c):
            return c + hits(pred_main(j))
        c = lax.fori_loop(0, nk, body, jnp.zeros((_SUB16, TQ), jnp.int16)) + hits(pred_tail())
        return jnp.sum(c.astype(jnp.int32), axis=0, keepdims=True)

    def count_ge(main_sc, tail_sc, t16):
        return count16(lambda j: main_sc[j] >= t16, lambda: tail_sc[...] >= t16)

    def kth_half(main_sc, tail_sc, target):
        def step(it, t_u):
            cand_u = t_u | jnp.left_shift(jnp.int32(1), 15 - it)
            cnt = count_ge(main_sc, tail_sc, lanes16(cand_u - _HALF))
            return jnp.where(cnt >= target, cand_u, t_u)
        return lax.fori_loop(0, 16, step, jnp.zeros((1, TQ), jnp.int32))

    t_hi = kth_half(hi_sc, hit_sc, topk) - _HALF
    t_hi16 = lanes16(t_hi)
    n_above = count16(lambda j: hi_sc[j] > t_hi16, lambda: hit_sc[...] > t_hi16)
    low16 = jnp.int16(-_HALF)

    def bucket(j, c):
        member = hi_sc[j] == t_hi16
        lo_sc[j] = jnp.where(member, lo_sc[j], low16)
        return c + hits(member)

    in_bucket = lax.fori_loop(0, nk, bucket, jnp.zeros((_SUB16, TQ), jnp.int16))
    member_t = hit_sc[...] == t_hi16
    lot_sc[...] = jnp.where(member_t, lot_sc[...], low16)
    in_bucket = in_bucket + hits(member_t)
    n_bucket = jnp.sum(in_bucket.astype(jnp.int32), axis=0, keepdims=True)
    t_lo = kth_half(lo_sc, lot_sc, topk - n_above)
    n_ge = n_above + jnp.where(t_lo == 0, n_bucket, count_ge(lo_sc, lot_sc, lanes16(t_lo - _HALF)))
    kth = (t_hi << 16) | t_lo

    def per_row(v):
        return jnp.broadcast_to(v, (LANES, TQ)).T

    thr = per_row(jnp.maximum(kth, _KEY_NEG_INF + 1))
    thr_m = _tile_lanes(thr, nblk)
    tied = ((n_ge > topk) & (kth > _KEY_NEG_INF)).astype(jnp.int32)
    excess = per_row(tied) > 0
    kf = float(topk)

    def count(ind_main, ind_tail, dtype):
        def body(j, c):
            ind = ind_main(j)
            for b in range(nblk):
                c = c + ind[:, b * LANES:(b + 1) * LANES]
            return c
        c = lax.fori_loop(0, nk, body, jnp.zeros((TQ, LANES), dtype)) + ind_tail()
        return jnp.sum(c.astype(_F32), axis=1, keepdims=True)

    @pl.when(jnp.max(jnp.where(excess, 1.0, 0.0)) > 0.0)
    def _():
        def count32(pred_main, pred_tail):
            return count(lambda j: jnp.where(pred_main(j), 1.0, 0.0),
                         lambda: jnp.where(pred_tail(), 1.0, 0.0), _F32)

        need = kf - count32(lambda j: key_sc[j] > thr_m, lambda: keyt_sc[...] > thr)
        excess_m = _tile_lanes(excess, nblk)

        def tied_before(J):
            Jm = _tile_lanes(J, nblk)
            return count32(lambda j: (key_sc[j] == thr_m) & (kom_ref[j] < Jm),
                           lambda: (keyt_sc[...] == thr) & (kot_ref[...] < J))

        def search(it, J):
            cand = J | jnp.left_shift(jnp.int32(1), 13 - it)
            return jnp.where(tied_before(cand) < need, cand, J)

        J = lax.fori_loop(0, 14, search, jnp.zeros((TQ, LANES), jnp.int32))
        Jm = _tile_lanes(J, nblk)

        def drop(j, carry):
            kj = key_sc[j]
            key_sc[j] = jnp.where(excess_m & (kj == thr_m) & (kom_ref[j] > Jm), _KEY_NEG_INF, kj)
            return carry

        lax.fori_loop(0, nk, drop, 0)
        kt = keyt_sc[...]
        keyt_sc[...] = jnp.where(excess & (kt == thr) & (kot_ref[...] > J), _KEY_NEG_INF, kt)

    qs = jnp.concatenate([qb_ref[:, h * LANES:(h + 1) * LANES] for h in range(N_HEADS_B)], axis=0)
    m_sc[...] = jnp.full(m_sc.shape, _NEG, _F32)
    l_sc[...] = jnp.zeros(l_sc.shape, _F32)
    acc_sc[...] = jnp.zeros(acc_sc.shape, _F32)

    def masked_scores(keys, kb):
        bias = jnp.where(keys >= _tile_lanes(thr, keys.shape[1] // LANES), 0.0, _NEG)
        return _nt_dot(qs, kb) + jnp.concatenate([bias] * N_HEADS_B, axis=0)

    def scores(j):
        return masked_scores(key_sc[j], kb_ref[pl.ds(pl.multiple_of(j * TK, TK), TK), :])

    def values(j):
        return vb_ref[pl.ds(pl.multiple_of(j * TK, TK), TK), :]

    _attend_tiles(nk, scores, values, s0_sc, s1_sc, m_sc, l_sc, acc_sc)
    _softmax_step(masked_scores(keyt_sc[...], kbt_ref[...]), vbt_ref[...], m_sc, l_sc, acc_sc)
    o = acc_sc[...] / l_sc[...]
    for h in range(N_HEADS_B):
        o_ref[:, h * LANES:(h + 1) * LANES] = o[h * TQ:(h + 1) * TQ].astype(o_ref.dtype)


def _dsa(qi, kiwi, qb, kir, kb, vb, kirt, kbt, vbt, qchunk, kchunk, korder, topk, Tq, TQ, TK):
    if Tq < LANES:
        pad = lambda a: jnp.pad(a[:, :Tq], ((0, 0), (0, LANES - Tq), (0, 0)))
        qchunk = np.concatenate([qchunk[:Tq], np.full((LANES - Tq,), qchunk[Tq - 1])])
        out = _dsa(pad(qi), pad(kiwi), pad(qb), kir, kb, vb, kirt, kbt, vbt, qchunk, kchunk, korder,
                   topk, LANES, LANES, TK)
        return out[:, :Tq]
    B = qi.shape[0]
    L = Lmain = kir.shape[1]
    assert Tq % TQ == 0 and Lmain % TK == 0 and kchunk.shape[0] == Lmain + TAIL
    nq, nt = Tq // TQ, Lmain // TK
    nk = jnp.asarray(_num_key_tiles(qchunk[:Tq], kchunk[:Lmain], TQ, TK))
    qc = jnp.asarray(qchunk[:Tq].reshape(Tq, 1).astype(np.int32))
    main = lambda a: jnp.asarray(a[:Lmain].reshape(nt, 1, TK).astype(np.int32))
    tail = lambda a: jnp.asarray(a[Lmain:].reshape(1, TAIL).astype(np.int32))
    R = N_HEADS_B * TQ
    kern = functools.partial(_dsa_kernel, TQ=TQ, TK=TK, topk=topk)
    first = lambda *_: 0
    qrow = lambda w: pl.BlockSpec((None, TQ, w), lambda b, i, nk: (b, i, 0))
    krow = lambda w: pl.BlockSpec((None, L, w), lambda b, i, nk: (b, 0, 0),
                                  pipeline_mode=pl.Buffered(1))
    cm = pl.BlockSpec((nt, 1, TK), lambda b, i, nk: (0, 0, 0))
    ct = pl.BlockSpec((1, TAIL), lambda b, i, nk: (0, 0))
    grid_spec = pltpu.PrefetchScalarGridSpec(
        num_scalar_prefetch=1,
        grid=(B, nq),
        in_specs=[qrow(QI_W), qrow(LANES), qrow(QB_EXP_W), krow(QI_W), krow(KB_W), krow(KB_W),
                  _tail_spec(kirt, QI_W, first), _tail_spec(kbt, KB_W, first), _tail_spec(vbt, KB_W, first),
                  pl.BlockSpec((TQ, 1), lambda b, i, nk: (i, 0)), cm, ct, cm, ct],
        out_specs=qrow(QB_EXP_W),
        scratch_shapes=[pltpu.VMEM((nt, TQ, TK), jnp.int32), pltpu.VMEM((TQ, TAIL), jnp.int32),
                        pltpu.VMEM((nt, TK // _SUB16, _SUB16, TQ), jnp.int16),
                        pltpu.VMEM((nt, TK // _SUB16, _SUB16, TQ), jnp.int16),
                        pltpu.VMEM((TAIL // _SUB16, _SUB16, TQ), jnp.int16),
                        pltpu.VMEM((TAIL // _SUB16, _SUB16, TQ), jnp.int16),
                        pltpu.VMEM((R, TK), _F32), pltpu.VMEM((R, TK), _F32),
                        pltpu.VMEM((R, LANES), _F32), pltpu.VMEM((R, LANES), _F32),
                        pltpu.VMEM((R, LANES), _F32)],
    )
    return pl.pallas_call(
        kern, grid_spec=grid_spec,
        out_shape=jax.ShapeDtypeStruct((B, Tq, QB_EXP_W), _MXU),
        compiler_params=_cparams(("parallel", "parallel")),
        name="dsa",
    )(nk, qi, kiwi, qb, kir, kb, vb, kirt, kbt, vbt, qc,
      main(kchunk), tail(kchunk), main(korder), tail(korder))


def _post_kernel(x_ref, oa_ref, ob_ref, ga_ref, gb_ref, wpa_ref, wpb_ref, wo_ref, g_ref,
                 wr_ref, br_ref, x1_o, h2_o, route_o):
    pa = jnp.dot(oa_ref[...], wpa_ref[...], preferred_element_type=_F32)
    pb = jnp.dot(ob_ref[...], wpb_ref[...], preferred_element_type=_F32)
    merged = ga_ref[...] * pa + gb_ref[...] * pb
    x1 = x_ref[...] + jnp.dot(merged.astype(wo_ref.dtype), wo_ref[...], preferred_element_type=_F32)
    x1_o[...] = x1
    ms = jnp.mean(x1 * x1, axis=-1, keepdims=True)
    h2 = (x1 * lax.rsqrt(ms + RMS_EPS) * g_ref[...]).astype(h2_o.dtype)
    h2_o[...] = h2
    cur = jnp.dot(h2, wr_ref[...], preferred_element_type=_F32) + br_ref[...]
    lane = lax.broadcasted_iota(jnp.int32, cur.shape, 1).astype(_F32)
    vals, idxs = [], []
    for _ in range(TOP_K_EXPERTS):
        m = jnp.max(cur, axis=1, keepdims=True)
        idx = jnp.min(jnp.where(cur == m, lane, float(LANES)), axis=1, keepdims=True)
        vals.append(m)
        idxs.append(idx)
        cur = jnp.where(lane == idx, -jnp.inf, cur)
    es = [jnp.exp(v - vals[0]) for v in vals]
    denom = es[0] + es[1] + es[2] + es[3]
    route = jnp.zeros(cur.shape, _F32)
    for k in range(TOP_K_EXPERTS):
        route = jnp.where(lane == float(k), idxs[k], route)
        route = jnp.where(lane == float(TOP_K_EXPERTS + k), es[k] / denom, route)
    route_o[...] = route


def _post(x, oa, ob, ga, gb, w_pa, w_pb_exp, w_o, g_ffn, w_router, b_router, T, tm):
    B, _, D = x.shape
    assert T % tm == 0
    row = lambda w: pl.BlockSpec((None, tm, w), lambda b, i: (b, i, 0))
    wr = jnp.concatenate([w_router, jnp.zeros((D, LANES - N_EXPERTS), w_router.dtype)], axis=1).astype(_MXU)
    br = jnp.concatenate([b_router.astype(_F32), jnp.full((LANES - N_EXPERTS,), _NEG, _F32)]).reshape(1, LANES)
    return pl.pallas_call(
        _post_kernel,
        grid=(B, T // tm),
        in_specs=[row(D), row(VA_W), row(QB_EXP_W), row(D), row(D),
                  _const_spec((VA_W, D)), _const_spec((QB_EXP_W, D)), _const_spec((D, D)),
                  _const_spec((1, D)), _const_spec((D, LANES)), _const_spec((1, LANES))],
        out_specs=[row(D), row(D), row(LANES)],
        out_shape=[jax.ShapeDtypeStruct((B, T, D), _F32), jax.ShapeDtypeStruct((B, T, D), _MXU),
                   jax.ShapeDtypeStruct((B, T, LANES), _F32)],
        compiler_params=_cparams(("parallel", "parallel")),
        name="post",
    )(x, oa, ob, ga, gb, w_pa.astype(_MXU), w_pb_exp, w_o.astype(_MXU), g_ffn.reshape(1, D), wr, br)


def _moe_kernel(be_ref, nu_ref, x_ref, wgu_ref, bgu_ref, wdn_ref, bdn_ref, y_ref, wgu_sc, wdn_sc):
    i = pl.program_id(0)
    used = i < nu_ref[0]
    new_expert = (i == 0) | (be_ref[i] != be_ref[jnp.maximum(i - 1, 0)])

    @pl.when(used & new_expert)
    def _():
        wgu_sc[...] = wgu_ref[...].astype(wgu_sc.dtype)
        wdn_sc[...] = wdn_ref[...].astype(wdn_sc.dtype)

    @pl.when(used)
    def _():
        gu = jnp.dot(x_ref[...], wgu_sc[...], preferred_element_type=_F32) + bgu_ref[...]
        glu = jnp.minimum(gu[:, :D_EXPERT], SWIGLU_LIMIT)
        lin = jnp.clip(gu[:, D_EXPERT:], -SWIGLU_LIMIT, SWIGLU_LIMIT)
        act = glu * jax.nn.sigmoid(SWIGLU_ALPHA * glu) * (lin + 1.0)
        y_ref[...] = jnp.dot(act.astype(wdn_sc.dtype), wdn_sc[...],
                             preferred_element_type=_F32) + bdn_ref[...]

    @pl.when(jnp.logical_not(used))
    def _():
        y_ref[...] = jnp.zeros(y_ref.shape, y_ref.dtype)


def _moe(h2, route, w_gu, b_gu, w_dn, b_dn):
    T, D = h2.shape
    K = TOP_K_EXPERTS
    A = T * K
    NB = -(-A // MOE_BLOCK) + N_EXPERTS
    NS = NB * MOE_BLOCK
    e_flat = route[:, :K].astype(jnp.int32).reshape(-1)
    experts = jnp.arange(N_EXPERTS, dtype=jnp.int32)
    counts = jnp.sum((e_flat[:, None] == experts[None, :]).astype(jnp.int32), axis=0)
    blocks_per = (counts + MOE_BLOCK - 1) // MOE_BLOCK
    cum_blocks = jnp.cumsum(blocks_per)
    blk = jnp.arange(NB, dtype=jnp.int32)
    block_expert = jnp.minimum(jnp.sum((cum_blocks[None, :] <= blk[:, None]).astype(jnp.int32), axis=1),
                               N_EXPERTS - 1)
    n_used = cum_blocks[-1:].astype(jnp.int32)
    cum_pad = jnp.cumsum(blocks_per * MOE_BLOCK - counts)
    r = jnp.arange(NS - A, dtype=jnp.int32)
    empty_expert = jnp.sum((cum_pad[None, :] <= r[:, None]).astype(jnp.int32), axis=1)
    src_bits = (NS - 1).bit_length()
    assert (2 * N_EXPERTS + 2) << src_bits < 2 ** 31
    group = jnp.concatenate([2 * e_flat, 2 * empty_expert + 1])
    slot_src = lax.sort((group << src_bits) | jnp.arange(NS, dtype=jnp.int32)) & ((1 << src_bits) - 1)
    slot_tok = jnp.where(slot_src < A, slot_src // K, (slot_src - A) % T)
    x_slots = h2[slot_tok]
    y_slots = pl.pallas_call(
        _moe_kernel,
        grid_spec=pltpu.PrefetchScalarGridSpec(
            num_scalar_prefetch=2,
            grid=(NB,),
            in_specs=[pl.BlockSpec((MOE_BLOCK, D), lambda i, be, nu: (i, 0)),
                      pl.BlockSpec((None, D, 2 * D_EXPERT), lambda i, be, nu: (be[i], 0, 0)),
                      pl.BlockSpec((None, 1, 2 * D_EXPERT), lambda i, be, nu: (be[i], 0, 0)),
                      pl.BlockSpec((None, D_EXPERT, D), lambda i, be, nu: (be[i], 0, 0)),
                      pl.BlockSpec((None, 1, D), lambda i, be, nu: (be[i], 0, 0))],
            out_specs=pl.BlockSpec((MOE_BLOCK, D), lambda i, be, nu: (i, 0)),
            scratch_shapes=[pltpu.VMEM((D, 2 * D_EXPERT), _MXU), pltpu.VMEM((D_EXPERT, D), _MXU)],
        ),
        out_shape=jax.ShapeDtypeStruct((NB * MOE_BLOCK, D), _F32),
        compiler_params=_cparams(("arbitrary",)),
        name="moe",
    )(block_expert, n_used, x_slots, w_gu, b_gu.reshape(N_EXPERTS, 1, -1).astype(_F32),
      w_dn, b_dn.reshape(N_EXPERTS, 1, -1).astype(_F32))
    _, dest = lax.sort((slot_src, jnp.arange(NS, dtype=jnp.int32)), num_keys=1)
    dest = dest[:A].reshape(T, K)
    return [y_slots[dest[:, k]] for k in range(K)]


def _final_kernel(x_ref, y0_ref, y1_ref, y2_ref, y3_ref, route_ref, g_ref, o_ref):
    K = TOP_K_EXPERTS
    x = x_ref[...]
    for k, y_ref in enumerate((y0_ref, y1_ref, y2_ref, y3_ref)):
        x = x + route_ref[:, K + k:K + k + 1] * y_ref[...]
    ms = jnp.mean(x * x, axis=-1, keepdims=True)
    o_ref[...] = x * lax.rsqrt(ms + RMS_EPS) * g_ref[...]


def _final(x1, ys, route, g_final, tm):
    T, D = x1.shape
    assert T % tm == 0
    row = pl.BlockSpec((tm, D), lambda i: (i, 0))
    return pl.pallas_call(
        _final_kernel, grid=(T // tm,),
        in_specs=[row] * 5 + [pl.BlockSpec((tm, LANES), lambda i: (i, 0)), _const_spec((1, D))],
        out_specs=row,
        out_shape=jax.ShapeDtypeStruct((T, D), _F32),
        compiler_params=_cparams(("parallel",)),
        name="final",
    )(x1, *ys, route, g_final.reshape(1, D))


def _group(x_rows, proj, keys, tails, qchunk, kchunk, korder, topk, Tq, tiles, weights, lam_init):
    (lam_vecs, subln_w, w_pa, w_pb_exp, w_o, g_ffn, w_router, b_router,
     w_gu, b_gu, w_dn, b_dn, g_final) = weights
    B = x_rows.shape[0]
    oa = _diff_attn(proj["qa"], keys["ka"], keys["va"], tails["ka"], tails["va"], qchunk, kchunk,
                    lam_vecs, subln_w, lam_init, Tq, tiles["tq_a"], tiles["tk"])
    ob = _dsa(proj["qi"], proj["wi"], proj["qb"], keys["kir"], keys["kb"], keys["vb"],
              tails["kir"], tails["kb"], tails["vb"], qchunk, kchunk, korder, topk, Tq,
              tiles["tq_b"], tiles["tk"])
    x1, h2, route = _post(x_rows, oa, ob, proj["ga"], proj["gb"], w_pa, w_pb_exp, w_o, g_ffn,
                          w_router, b_router, Tq, tiles["tm_post"])
    T = B * Tq
    route = route.reshape(T, LANES)
    ys = _moe(h2.reshape(T, D_MODEL), route, w_gu, b_gu, w_dn, b_dn)
    y = _final(x1.reshape(T, D_MODEL), ys, route, g_final, tiles["tm_post"])
    return y.reshape(B, Tq, D_MODEL)


_KEY_OPERANDS = ("ka", "va", "kb", "vb", "kir")


def kernel(x_prompt, x_sample, cache_a_k, cache_a_v, cache_b_k, cache_b_v, cache_idx_k,
           meta_tokens, g_mix, w_in, lambda_q1, lambda_k1, lambda_q2, lambda_k2, subln_w,
           w_pa, w_pb, w_o, g_ffn, w_router, b_router, w_gu, b_gu, w_dn, b_dn, g_final):
    B, S, D = x_prompt.shape
    Bs, S_dec, _ = x_sample.shape
    P = cache_a_k.shape[2]
    depth = g_mix.shape[0]
    assert depth == 1
    l = 0
    lam_init = 0.8 - 0.6 * math.exp(-0.3 * l)
    w_lay = _layout_w_in(w_in[l])
    w_pb_exp = _layout_w_pb(w_pb[l])
    lam_vecs = jnp.stack([lambda_q1[l], lambda_k1[l], lambda_q2[l], lambda_k2[l]]).astype(_F32)
    weights = (lam_vecs, subln_w[l], w_pa[l], w_pb_exp, w_o[l], g_ffn[l], w_router[l], b_router[l],
               w_gu[l], b_gu[l], w_dn[l], b_dn[l], g_final)

    r = np.arange(S + TAIL)
    chunk_p = np.where(r < S, r // CHUNK, np.where(r < S + N_META, -1, _BIG_CHUNK)).astype(np.int64)
    order_p = np.where(r < S, r + N_META, r - S).astype(np.int32)
    tiles_p = dict(tq_a=min(512, S), tq_b=min(128, S), tk=min(512, S), tm_post=min(512, S))
    meta_tile = jnp.concatenate([meta_tokens.astype(x_prompt.dtype),
                                 jnp.zeros((TAIL - N_META, D), x_prompt.dtype)])[None]
    pos_m = np.where(np.arange(TAIL) < N_META, np.arange(TAIL), 0).astype(np.int32)
    pm = _project(meta_tile, jnp.asarray(pos_m), g_mix[l], w_lay, TAIL)
    meta_rows = {n: pm[n][0, :N_META] for n in ("ka_f", "va_f", "kb_f", "vb_f", "kiwi")}
    pp = _project(x_prompt, jnp.asarray(N_META + np.arange(S, dtype=np.int32)), g_mix[l], w_lay,
                  tiles_p["tm_post"], head_rows=meta_rows)
    y_prompt = _group(x_prompt, pp, pp, pm, chunk_p[:S], chunk_p, order_p, min(TOPK_MAX, S // 4), S,
                      tiles_p, weights, lam_init)

    def rows_p(name, shape, width=None):
        return pp[name][:, :, :width].reshape((1, B, N_META + S) + shape)

    pos_s = (P + np.arange(S_dec)).astype(np.int32)
    ps = _project(x_sample, jnp.asarray(pos_s), g_mix[l], w_lay, S_dec)
    r = np.arange(P + TAIL)
    chunk_s = np.where(r < P + S_dec, r // CHUNK, _BIG_CHUNK).astype(np.int64)
    qchunk_s = (pos_s // CHUNK).astype(np.int64)
    cached = dict(ka=cache_a_k[l], va=cache_a_v[l], kb=cache_b_k[l], vb=cache_b_v[l],
                  kir=jnp.tile(cache_idx_k[l], (1, 1, N_IDX_HEADS)))
    keys_s = {n: cached[n].reshape(Bs, P, -1).astype(_MXU) for n in _KEY_OPERANDS}
    tails_s = {n: jnp.concatenate([ps[n], jnp.zeros((Bs, TAIL - S_dec, ps[n].shape[-1]), _MXU)], axis=1)
               for n in _KEY_OPERANDS}
    tiles_s = dict(tq_a=S_dec, tq_b=S_dec, tk=min(512, P), tm_post=S_dec)
    y_sample = _group(x_sample, ps, keys_s, tails_s, qchunk_s, chunk_s, r.astype(np.int32),
                      min(TOPK_MAX, (P + S_dec) // 4), S_dec, tiles_s, weights, lam_init)

    def rows_s(name, shape, width=None):
        return ps[name][:, :, :width].reshape((1, Bs, S_dec) + shape)

    ha, hb = (N_HEADS_A, 2 * HEAD_DIM_A), (N_KV_B, HEAD_DIM_B)
    return (y_prompt, y_sample,
            rows_p("ka_f", ha), rows_p("va_f", ha), rows_p("kb_f", hb), rows_p("vb_f", hb),
            rows_p("kiwi", (IDX_DIM,), IDX_DIM),
            rows_s("ka_f", ha), rows_s("va_f", ha), rows_s("kb_f", hb), rows_s("vb_f", hb),
            rows_s("kiwi", (IDX_DIM,), IDX_DIM))
```

```python
import functools
import math

import numpy as np
import jax
import jax.numpy as jnp
from jax import lax
from jax.experimental import pallas as pl
from jax.experimental.pallas import tpu as pltpu

D_MODEL = 1024
CHUNK = 64
N_META = 16
ROPE_THETA = 500000.0
RMS_EPS = 1e-6
N_HEADS_A = 4
HEAD_DIM_A = 64
N_HEADS_B = 8
N_KV_B = 2
HEAD_DIM_B = 64
N_IDX_HEADS = 8
IDX_DIM = 32
TOPK_MAX = 256
N_EXPERTS = 32
TOP_K_EXPERTS = 4
D_EXPERT = 1024
SWIGLU_LIMIT = 7.0
SWIGLU_ALPHA = 1.702
MOE_BLOCK = 512

QA_W = N_HEADS_A * 2 * HEAD_DIM_A
VA_W = QA_W
QB_W = N_HEADS_B * HEAD_DIM_B
KB_W = N_KV_B * HEAD_DIM_B
QI_W = N_IDX_HEADS * IDX_DIM
SPLITS = (QA_W, QA_W, VA_W, QB_W, KB_W, KB_W, QI_W, IDX_DIM, N_IDX_HEADS, D_MODEL, D_MODEL)

LANES = 128
TAIL = LANES
QB_EXP_W = N_HEADS_B * LANES

_SEC = {}
_off = 0
for _name, _w in (("qa", QA_W), ("ka", QA_W), ("va", VA_W), ("qb", QB_EXP_W), ("kb", KB_W),
                  ("vb", KB_W), ("qi", QI_W), ("kiwi", LANES), ("kir", QI_W),
                  ("ga", D_MODEL), ("gb", D_MODEL)):
    _SEC[_name] = (_off, _off + _w)
    _off += _w
W_LAYOUT = _off

_MXU = jnp.bfloat16
_F32 = jnp.float32
_NEG = -1e30
_BIG_CHUNK = 2 ** 30
_LOG2E = math.log2(math.e)
_HALF = 2 ** 15
_KEY_NEG_INF = int(np.int32(np.uint32(0xFF800000) ^ np.uint32(0x7FFFFFFF)))
_VMEM_LIMIT = 56 * 1024 * 1024


def _cparams(sem):
    return pltpu.CompilerParams(dimension_semantics=sem, vmem_limit_bytes=_VMEM_LIMIT)


def _const_spec(shape):
    nd = len(shape)
    return pl.BlockSpec(shape, lambda *_: (0,) * nd)


def _nt_dot(a, b):
    return lax.dot_general(a, b, (((1,), (1,)), ((), ())), preferred_element_type=_F32)


def _sortable(x):
    b = lax.bitcast_convert_type(x, jnp.int32)
    return b ^ ((b >> 31) & 0x7FFFFFFF)


def _rope_block(z, cos, sin_lo, sin_hi, half):
    return (z * cos + pltpu.roll(z, LANES - half, 1) * sin_lo + pltpu.roll(z, half, 1) * sin_hi)


def _proj_kernel(x_ref, g_ref, w_ref, tab_ref, *refs):
    (qa_o, kaf_o, kab_o, vaf_o, vab_o, qb_o, kbf_o, kbb_o, vbf_o, vbb_o,
     qi_o, kiwi_o, kir_o, ga_o, gb_o, wi_o) = refs[-16:]
    kaf_o, vaf_o, kbf_o, vbf_o, kiwi_o = (o.at[0] if len(o.shape) == 3 else o
                                          for o in (kaf_o, vaf_o, kbf_o, vbf_o, kiwi_o))
    x = x_ref[...]
    ms = jnp.mean(x * x, axis=-1, keepdims=True)
    h = (x * lax.rsqrt(ms + RMS_EPS) * g_ref[...]).astype(w_ref.dtype)

    def sec(name):
        a, b = _SEC[name]
        return jnp.dot(h, w_ref[:, a:b], preferred_element_type=_F32)

    t64 = tuple(tab_ref[:, k * LANES:(k + 1) * LANES] for k in (0, 1, 2))
    t32 = tuple(tab_ref[:, k * LANES:(k + 1) * LANES] for k in (3, 4, 5))
    tki = tuple(tab_ref[:, k * LANES:(k + 1) * LANES] for k in (6, 7, 8))

    def roped(z, tab, half):
        return [_rope_block(z[:, c * LANES:(c + 1) * LANES], *tab, half)
                for c in range(z.shape[1] // LANES)]

    scale_a = HEAD_DIM_A ** -0.5 * _LOG2E
    scale_b = HEAD_DIM_B ** -0.5 * _LOG2E
    for c, blk in enumerate(roped(sec("qa"), t64, 8)):
        qa_o[:, c * LANES:(c + 1) * LANES] = (blk * scale_a).astype(qa_o.dtype)
    for c, blk in enumerate(roped(sec("ka"), t64, 8)):
        kaf_o[:, c * LANES:(c + 1) * LANES] = blk
        kab_o[:, c * LANES:(c + 1) * LANES] = blk.astype(kab_o.dtype)
    va = sec("va")
    vaf_o[...] = va
    vab_o[...] = va.astype(vab_o.dtype)
    for c, blk in enumerate(roped(sec("qb"), t64, 8)):
        qb_o[:, c * LANES:(c + 1) * LANES] = (blk * scale_b).astype(qb_o.dtype)
    kb = roped(sec("kb"), t64, 8)[0]
    kbf_o[...] = kb
    kbb_o[...] = kb.astype(kbb_o.dtype)
    vb = sec("vb")
    vbf_o[...] = vb
    vbb_o[...] = vb.astype(vbb_o.dtype)
    for c, blk in enumerate(roped(sec("qi"), t32, 4)):
        qi_o[:, c * LANES:(c + 1) * LANES] = blk.astype(qi_o.dtype)
    kiwi = roped(sec("kiwi"), tki, 4)[0]
    kiwi_o[...] = kiwi
    wi_o[...] = kiwi
    for c, blk in enumerate(roped(sec("kir"), t32, 4)):
        kir_o[:, c * LANES:(c + 1) * LANES] = blk.astype(kir_o.dtype)
    ga_o[...] = jax.nn.sigmoid(sec("ga"))
    gb_o[...] = jax.nn.sigmoid(sec("gb"))


def _rope_tables(pos):
    def cs(d):
        rd = d // 4
        half = rd // 2
        inv_freq = ROPE_THETA ** (-jnp.arange(half, dtype=_F32) * 2.0 / rd)
        ang = pos.astype(_F32)[:, None] * inv_freq[None, :]
        return jnp.cos(ang), jnp.sin(ang), half

    T = pos.shape[0]

    def pattern(d, width):
        c, s, half = cs(d)
        one = jnp.ones((T, width - 2 * half), _F32)
        zero = lambda n: jnp.zeros((T, n), _F32)
        cos = jnp.concatenate([c, c, one], axis=1)
        lo = jnp.concatenate([-s, zero(width - half)], axis=1)
        hi = jnp.concatenate([zero(half), s, zero(width - 2 * half)], axis=1)
        return cos, lo, hi

    tabs = []
    for d, width in ((64, 64), (32, 32), (32, LANES)):
        tabs += [jnp.tile(t, (1, LANES // width)) for t in pattern(d, width)]
    return jnp.concatenate(tabs, axis=1)


def _layout_w_in(w_in):
    offs = np.cumsum((0,) + SPLITS)
    qa, ka, va, qb, kb, vb, qi, ki, wi, ga, gb = (w_in[:, offs[i]:offs[i + 1]] for i in range(11))
    D = w_in.shape[0]
    z64 = jnp.zeros((D, HEAD_DIM_B), w_in.dtype)
    per_kv = N_HEADS_B // N_KV_B
    qb_exp = []
    for hq in range(N_HEADS_B):
        col = qb[:, hq * HEAD_DIM_B:(hq + 1) * HEAD_DIM_B]
        qb_exp += [col, z64] if hq // per_kv == 0 else [z64, col]
    kiwi = jnp.concatenate([ki, wi, jnp.zeros((D, LANES - IDX_DIM - N_IDX_HEADS), w_in.dtype)], axis=1)
    kir = jnp.tile(ki, (1, N_IDX_HEADS))
    w = jnp.concatenate([qa, ka, va] + qb_exp + [kb, vb, qi, kiwi, kir, ga, gb], axis=1)
    assert w.shape[1] == W_LAYOUT
    return w.astype(_MXU)


def _layout_w_pb(w_pb):
    D = w_pb.shape[1]
    z64 = jnp.zeros((HEAD_DIM_B, D), w_pb.dtype)
    per_kv = N_HEADS_B // N_KV_B
    rows = []
    for hq in range(N_HEADS_B):
        r = w_pb[hq * HEAD_DIM_B:(hq + 1) * HEAD_DIM_B]
        rows += [r, z64] if hq // per_kv == 0 else [z64, r]
    return jnp.concatenate(rows, axis=0).astype(_MXU)


def _project(x, pos, g_mix, w_lay, tm, head_rows=None):
    B, T, D = x.shape
    assert T % tm == 0
    off = next(iter(head_rows.values())).shape[0] if head_rows else 0
    total = off + T
    assert off % 8 == 0
    tab = _rope_tables(pos)
    row = lambda w: pl.BlockSpec((None, tm, w), lambda b, i: (b, i, 0))
    shifted = lambda w: pl.BlockSpec((pl.Element(1), pl.Element(tm), pl.Element(w)),
                                     lambda b, i: (b, pl.multiple_of(off + i * tm, 8), 0))
    outs = [("qa", QA_W, _MXU), ("ka_f", QA_W, _F32), ("ka", QA_W, _MXU), ("va_f", VA_W, _F32),
            ("va", VA_W, _MXU), ("qb", QB_EXP_W, _MXU), ("kb_f", KB_W, _F32), ("kb", KB_W, _MXU),
            ("vb_f", KB_W, _F32), ("vb", KB_W, _MXU), ("qi", QI_W, _MXU), ("kiwi", LANES, _F32),
            ("kir", QI_W, _MXU), ("ga", D_MODEL, _F32), ("gb", D_MODEL, _F32), ("wi", LANES, _F32)]
    is_cache = lambda n: n.endswith("_f") or n == "kiwi"
    n_in = 4
    heads, aliases = [], {}
    for k, (n, w, _) in enumerate(outs):
        if head_rows and is_cache(n):
            rows = jnp.broadcast_to(head_rows[n][None], (B, off, w))
            heads.append(jnp.concatenate([rows, jnp.zeros((B, T, w), _F32)], axis=1))
            aliases[n_in + len(heads) - 1] = k
    res = pl.pallas_call(
        _proj_kernel,
        grid=(B, T // tm),
        in_specs=[row(D), _const_spec((1, D)),
                  pl.BlockSpec((D, W_LAYOUT), lambda b, i: (0, 0), pipeline_mode=pl.Buffered(1)),
                  pl.BlockSpec((tm, 9 * LANES), lambda b, i: (i, 0))]
                 + [pl.BlockSpec(memory_space=pl.ANY)] * len(heads),
        out_specs=[shifted(w) if (is_cache(n) and off) else row(w) for n, w, _ in outs],
        out_shape=[jax.ShapeDtypeStruct((B, total if is_cache(n) else T, w), dt) for n, w, dt in outs],
        input_output_aliases=aliases,
        compiler_params=_cparams(("parallel", "parallel")),
        name="proj",
    )(x, g_mix.reshape(1, D), w_lay, tab, *heads)
    return {n: r for (n, _, _), r in zip(outs, res)}


def _tile_lanes(v, n):
    return v if n == 1 else jnp.concatenate([v] * n, axis=1)


def _softmax_step(s, v, m_sc, l_sc, acc_sc):
    m_prev = m_sc[...]
    m_new = jnp.maximum(m_prev, jnp.max(s, axis=1, keepdims=True))
    alpha = jnp.exp2(m_prev - m_new)
    p = jnp.exp2(s - _tile_lanes(m_new, s.shape[1] // LANES))
    l_sc[...] = alpha * l_sc[...] + jnp.sum(p, axis=1, keepdims=True)
    acc_sc[...] = alpha * acc_sc[...] + jnp.dot(p.astype(v.dtype), v, preferred_element_type=_F32)
    m_sc[...] = m_new


def _attend_tiles(n, scores, values, s0_sc, s1_sc, m_sc, l_sc, acc_sc):
    s0_sc[...] = scores(0)

    def pair(p, carry):
        j = 2 * p
        s1_sc[...] = scores(j + 1)
        _softmax_step(s0_sc[...], values(j), m_sc, l_sc, acc_sc)
        s0_sc[...] = scores(jnp.minimum(j + 2, n - 1))
        _softmax_step(s1_sc[...], values(j + 1), m_sc, l_sc, acc_sc)
        return carry

    lax.fori_loop(0, n // 2, pair, 0)

    @pl.when(n % 2 == 1)
    def _():
        _softmax_step(s0_sc[...], values(n - 1), m_sc, l_sc, acc_sc)


def _num_full_tiles(qchunk, kchunk_main, TQ, TK):
    last = kchunk_main.reshape(-1, TK).max(axis=1)
    qmin = qchunk.reshape(-1, TQ).min(axis=1)
    full = last[None, :] <= qmin[:, None]
    return np.cumprod(full, axis=1).sum(axis=1).astype(np.int32)


def _num_key_tiles(qchunk, kchunk_main, TQ, TK):
    first = kchunk_main.reshape(-1, TK)[:, 0]
    qmax = qchunk.reshape(-1, TQ).max(axis=1)
    return (first[None, :] <= qmax[:, None]).sum(axis=1).astype(np.int32)


def _diff_attn_kernel(nk_ref, nf_ref, q_ref, k_ref, v_ref, kt_ref, vt_ref, qc_ref, kcm_ref, kct_ref,
                      lam_ref, sub_ref, o_ref, s0_sc, s1_sc, m_sc, l_sc, acc_sc, *, TQ, TK, lam_init):
    i = pl.program_id(2)
    q = q_ref[...]
    lane = lax.broadcasted_iota(jnp.int32, q.shape, 1)
    zero = jnp.zeros_like(q)
    qs = jnp.concatenate([jnp.where(lane < HEAD_DIM_A, q, zero),
                          jnp.where(lane >= HEAD_DIM_A, q, zero)], axis=0)
    qc = qc_ref[...]
    qc2 = jnp.concatenate([qc, qc], axis=0)
    m_sc[...] = jnp.full(m_sc.shape, _NEG, _F32)
    l_sc[...] = jnp.zeros(l_sc.shape, _F32)
    acc_sc[...] = jnp.zeros(acc_sc.shape, _F32)

    def keys(j):
        return k_ref[pl.ds(pl.multiple_of(j * TK, TK), TK), :]

    def values(j):
        return v_ref[pl.ds(pl.multiple_of(j * TK, TK), TK), :]

    def masked_scores(k, kc):
        return jnp.where(kc <= qc2, _nt_dot(qs, k), _NEG)

    _attend_tiles(nf_ref[i], lambda j: _nt_dot(qs, keys(j)), values,
                  s0_sc, s1_sc, m_sc, l_sc, acc_sc)

    def diagonal(j, carry):
        _softmax_step(masked_scores(keys(j), kcm_ref[j]), values(j), m_sc, l_sc, acc_sc)
        return carry

    lax.fori_loop(nf_ref[i], nk_ref[i], diagonal, 0)
    _softmax_step(masked_scores(kt_ref[...], kct_ref[...]), vt_ref[...], m_sc, l_sc, acc_sc)

    lam = (jnp.exp(jnp.sum(lam_ref[0:1, :] * lam_ref[1:2, :], axis=1, keepdims=True))
           - jnp.exp(jnp.sum(lam_ref[2:3, :] * lam_ref[3:4, :], axis=1, keepdims=True))
           + lam_init)
    o = acc_sc[...] / l_sc[...]
    d = o[:TQ] - lam * o[TQ:]
    ms = jnp.mean(d * d, axis=-1, keepdims=True)
    o_ref[...] = ((d * lax.rsqrt(ms + RMS_EPS) * sub_ref[...]) * (1.0 - lam_init)).astype(o_ref.dtype)


def _tail_spec(tail, width, index):
    per_batch = tail.shape[0] > 1
    return pl.BlockSpec((None, TAIL, width), lambda b, *rest: (b if per_batch else 0, 0, index(*rest)))


def _diff_attn(q, k, v, kt, vt, qchunk, kchunk, lam_vecs, subln_w, lam_init, Tq, TQ, TK):
    B = q.shape[0]
    Lmain = k.shape[1]
    L = Lmain
    assert Tq % TQ == 0 and Lmain % TK == 0 and kchunk.shape[0] == Lmain + TAIL
    nq, nt = Tq // TQ, Lmain // TK
    nk = jnp.asarray(_num_key_tiles(qchunk[:Tq], kchunk[:Lmain], TQ, TK))
    nf = jnp.asarray(_num_full_tiles(qchunk[:Tq], kchunk[:Lmain], TQ, TK))
    qc = jnp.asarray(qchunk[:Tq].reshape(Tq, 1).astype(np.int32))
    kcm = jnp.asarray(kchunk[:Lmain].reshape(nt, 1, TK).astype(np.int32))
    kct = jnp.asarray(kchunk[Lmain:].reshape(1, TAIL).astype(np.int32))
    R = 2 * TQ
    kern = functools.partial(_diff_attn_kernel, TQ=TQ, TK=TK, lam_init=lam_init)
    head = lambda h, i, *_: h
    grid_spec = pltpu.PrefetchScalarGridSpec(
        num_scalar_prefetch=2,
        grid=(B, N_HEADS_A, nq),
        in_specs=[pl.BlockSpec((None, TQ, LANES), lambda b, h, i, *_: (b, i, h)),
                  pl.BlockSpec((None, L, LANES), lambda b, h, i, *_: (b, 0, h)),
                  pl.BlockSpec((None, L, LANES), lambda b, h, i, *_: (b, 0, h)),
                  _tail_spec(kt, LANES, head), _tail_spec(vt, LANES, head),
                  pl.BlockSpec((TQ, 1), lambda b, h, i, *_: (i, 0)),
                  pl.BlockSpec((nt, 1, TK), lambda b, h, i, *_: (0, 0, 0)),
                  pl.BlockSpec((1, TAIL), lambda b, h, i, *_: (0, 0)),
                  pl.BlockSpec((4, HEAD_DIM_A), lambda b, h, i, *_: (0, 0)),
                  pl.BlockSpec((1, LANES), lambda b, h, i, *_: (0, 0))],
        out_specs=pl.BlockSpec((None, TQ, LANES), lambda b, h, i, *_: (b, i, h)),
        scratch_shapes=[pltpu.VMEM((R, TK), _F32)] * 2 + [pltpu.VMEM((R, LANES), _F32)] * 3,
    )
    return pl.pallas_call(
        kern, grid_spec=grid_spec,
        out_shape=jax.ShapeDtypeStruct((B, Tq, VA_W), _MXU),
        compiler_params=_cparams(("parallel", "parallel", "parallel")),
        name="diff_attn",
    )(nk, nf, q, k, v, kt, vt, qc, kcm, kct, lam_vecs, subln_w.reshape(1, LANES))


def _dsa_kernel(nk_ref, qi_ref, kiwi_ref, qb_ref, kir_ref, kb_ref, vb_ref, kirt_ref, kbt_ref, vbt_ref,
                qc_ref, kcm_ref, kct_ref, o_ref,
                key_sc, keyt_sc, hi_sc, lo_sc, hit_sc, lot_sc, s0_sc, s1_sc, m_sc, l_sc, acc_sc,
                *, TQ, TK, topk, tail_first):
    i = pl.program_id(1)
    nk = nk_ref[i]
    H = N_IDX_HEADS
    nblk = TK // LANES

    qi = qi_ref[...]
    head = lax.broadcasted_iota(jnp.int32, qi.shape, 1) // IDX_DIM
    zero = jnp.zeros_like(qi)
    qis = jnp.concatenate([jnp.where(head == h, qi, zero) for h in range(H)], axis=0)
    wi = kiwi_ref[:, IDX_DIM:IDX_DIM + H] * (H ** -0.5) * (IDX_DIM ** -0.5)
    wi_b = [jnp.broadcast_to(wi[:, h:h + 1], (TQ, LANES)) for h in range(H)]
    qc = qc_ref[...]

    def score_keys(lg, kc):
        n = lg.shape[1] // LANES
        sc = _tile_lanes(wi_b[0], n) * jnp.maximum(lg[0:TQ], 0.0)
        for h in range(1, H):
            sc = sc + _tile_lanes(wi_b[h], n) * jnp.maximum(lg[h * TQ:(h + 1) * TQ], 0.0)
        return _sortable(jnp.where(kc <= qc, sc, -jnp.inf))

    def split16(key):
        return (key >> 16).astype(jnp.int16), ((key & 0xFFFF) - _HALF).astype(jnp.int16)

    def fill(j, carry):
        lg = _nt_dot(qis, kir_ref[pl.ds(pl.multiple_of(j * TK, TK), TK), :])
        key = score_keys(lg, kcm_ref[j])
        key_sc[j] = key
        hi_sc[j], lo_sc[j] = split16(key)
        return carry

    lax.fori_loop(0, nk, fill, 0)
    key_t = score_keys(_nt_dot(qis, kirt_ref[...]), kct_ref[...])
    keyt_sc[...] = key_t
    hit_sc[...], lot_sc[...] = split16(key_t)

    def count(ind_main, ind_tail, dtype):
        def body(j, c):
            ind = ind_main(j)
            for b in range(nblk):
                c = c + ind[:, b * LANES:(b + 1) * LANES]
            return c
        c = lax.fori_loop(0, nk, body, jnp.zeros((TQ, LANES), dtype)) + ind_tail()
        return jnp.sum(c.astype(_F32), axis=1, keepdims=True)

    one16, zero16 = jnp.int16(1), jnp.int16(0)

    def lanes16(v):
        return jnp.broadcast_to(v, (TQ, LANES)).astype(jnp.int16)

    def count16(main_sc, tail_sc, t16, strict=False):
        tm = _tile_lanes(t16, nblk)
        cmp = (lambda a, b: a > b) if strict else (lambda a, b: a >= b)
        return count(lambda j: jnp.where(cmp(main_sc[j], tm), one16, zero16),
                     lambda: jnp.where(cmp(tail_sc[...], t16), one16, zero16), jnp.int16)

    def kth_half(main_sc, tail_sc, target):
        def step(it, t_u):
            cand_u = t_u | jnp.left_shift(jnp.int32(1), 15 - it)
            cnt = count16(main_sc, tail_sc, lanes16(cand_u - _HALF))
            return jnp.where(cnt >= target, cand_u, t_u)
        return lax.fori_loop(0, 16, step, jnp.zeros((TQ, 1), jnp.int32))

    kf = float(topk)
    t_hi = kth_half(hi_sc, hit_sc, kf) - _HALF
    t_hi16 = lanes16(t_hi)
    n_above = count16(hi_sc, hit_sc, t_hi16, strict=True)
    t_hi16m = _tile_lanes(t_hi16, nblk)
    low16 = jnp.int16(-_HALF)

    def bucket(j, carry):
        lo_sc[j] = jnp.where(hi_sc[j] == t_hi16m, lo_sc[j], low16)
        return carry

    lax.fori_loop(0, nk, bucket, 0)
    lot_sc[...] = jnp.where(hit_sc[...] == t_hi16, lot_sc[...], low16)
    t_lo = kth_half(lo_sc, lot_sc, kf - n_above)
    n_ge = n_above + count16(lo_sc, lot_sc, lanes16(t_lo - _HALF))
    kth = (t_hi << 16) | t_lo
    thr1 = jnp.maximum(kth, _KEY_NEG_INF + 1)
    thr = jnp.broadcast_to(thr1, (TQ, LANES))
    thr_m = _tile_lanes(thr, nblk)
    excess = jnp.broadcast_to((n_ge > kf) & (kth > _KEY_NEG_INF), (TQ, LANES))

    @pl.when(jnp.max(jnp.where(excess, 1.0, 0.0)) > 0.0)
    def _():
        quota = kf - count(lambda j: jnp.where(key_sc[j] > thr_m, 1.0, 0.0),
                           lambda: jnp.where(keyt_sc[...] > thr, 1.0, 0.0), _F32)
        r = lax.broadcasted_iota(jnp.int32, (LANES, LANES), 0)
        c = lax.broadcasted_iota(jnp.int32, (LANES, LANES), 1)
        upto = jnp.where(r <= c, 1.0, 0.0).astype(_MXU)
        ones = jnp.ones((LANES, LANES), _MXU)

        def keep_first(keys, seen):
            tie = keys == thr
            e = jnp.where(tie, 1.0, 0.0).astype(_MXU)
            rank = seen + jnp.dot(e, upto, preferred_element_type=_F32)
            keys = jnp.where(tie & (rank > quota), _KEY_NEG_INF, keys)
            return keys, seen + jnp.dot(e, ones, preferred_element_type=_F32)

        def tail_block(seen):
            keyt_sc[...], seen = keep_first(keyt_sc[...], seen)
            return seen

        def main_tile(j, seen):
            kj = key_sc[j]
            blocks = []
            for b in range(nblk):
                blk, seen = keep_first(kj[:, b * LANES:(b + 1) * LANES], seen)
                blocks.append(blk)
            key_sc[j] = jnp.concatenate(blocks, axis=1)
            return seen

        seen = jnp.zeros((TQ, LANES), _F32)
        if tail_first:
            seen = tail_block(seen)
        seen = lax.fori_loop(0, nk, main_tile, seen)
        if not tail_first:
            tail_block(seen)

    qs = jnp.concatenate([qb_ref[:, h * LANES:(h + 1) * LANES] for h in range(N_HEADS_B)], axis=0)
    m_sc[...] = jnp.full(m_sc.shape, _NEG, _F32)
    l_sc[...] = jnp.zeros(l_sc.shape, _F32)
    acc_sc[...] = jnp.zeros(acc_sc.shape, _F32)

    def masked_scores(keys, kb):
        bias = jnp.where(keys >= _tile_lanes(thr, keys.shape[1] // LANES), 0.0, _NEG)
        return _nt_dot(qs, kb) + jnp.concatenate([bias] * N_HEADS_B, axis=0)

    def scores(j):
        return masked_scores(key_sc[j], kb_ref[pl.ds(pl.multiple_of(j * TK, TK), TK), :])

    def values(j):
        return vb_ref[pl.ds(pl.multiple_of(j * TK, TK), TK), :]

    _attend_tiles(nk, scores, values, s0_sc, s1_sc, m_sc, l_sc, acc_sc)
    _softmax_step(masked_scores(keyt_sc[...], kbt_ref[...]), vbt_ref[...], m_sc, l_sc, acc_sc)
    o = acc_sc[...] / l_sc[...]
    for h in range(N_HEADS_B):
        o_ref[:, h * LANES:(h + 1) * LANES] = o[h * TQ:(h + 1) * TQ].astype(o_ref.dtype)


def _dsa(qi, kiwi, qb, kir, kb, vb, kirt, kbt, vbt, qchunk, kchunk, korder, topk, Tq, TQ, TK):
    B = qi.shape[0]
    L = Lmain = kir.shape[1]
    assert Tq % TQ == 0 and Lmain % TK == 0 and kchunk.shape[0] == Lmain + TAIL
    real = kchunk < _BIG_CHUNK
    order_main, order_tail = korder[:Lmain], korder[Lmain:][real[Lmain:]]
    assert np.all(np.diff(order_main) > 0) and np.all(np.diff(order_tail) > 0)
    tail_first = bool(order_tail[-1] < order_main[0])
    assert tail_first or order_tail[0] > order_main[-1]
    nq, nt = Tq // TQ, Lmain // TK
    nk = jnp.asarray(_num_key_tiles(qchunk[:Tq], kchunk[:Lmain], TQ, TK))
    qc = jnp.asarray(qchunk[:Tq].reshape(Tq, 1).astype(np.int32))
    main = lambda a: jnp.asarray(a[:Lmain].reshape(nt, 1, TK).astype(np.int32))
    tail = lambda a: jnp.asarray(a[Lmain:].reshape(1, TAIL).astype(np.int32))
    R = N_HEADS_B * TQ
    kern = functools.partial(_dsa_kernel, TQ=TQ, TK=TK, topk=topk, tail_first=tail_first)
    first = lambda *_: 0
    qrow = lambda w: pl.BlockSpec((None, TQ, w), lambda b, i, nk: (b, i, 0))
    krow = lambda w: pl.BlockSpec((None, L, w), lambda b, i, nk: (b, 0, 0),
                                  pipeline_mode=pl.Buffered(1))
    cm = pl.BlockSpec((nt, 1, TK), lambda b, i, nk: (0, 0, 0))
    ct = pl.BlockSpec((1, TAIL), lambda b, i, nk: (0, 0))
    grid_spec = pltpu.PrefetchScalarGridSpec(
        num_scalar_prefetch=1,
        grid=(B, nq),
        in_specs=[qrow(QI_W), qrow(LANES), qrow(QB_EXP_W), krow(QI_W), krow(KB_W), krow(KB_W),
                  _tail_spec(kirt, QI_W, first), _tail_spec(kbt, KB_W, first), _tail_spec(vbt, KB_W, first),
                  pl.BlockSpec((TQ, 1), lambda b, i, nk: (i, 0)), cm, ct],
        out_specs=qrow(QB_EXP_W),
        scratch_shapes=[pltpu.VMEM((nt, TQ, TK), jnp.int32), pltpu.VMEM((TQ, TAIL), jnp.int32),
                        pltpu.VMEM((nt, TQ, TK), jnp.int16), pltpu.VMEM((nt, TQ, TK), jnp.int16),
                        pltpu.VMEM((TQ, TAIL), jnp.int16), pltpu.VMEM((TQ, TAIL), jnp.int16),
                        pltpu.VMEM((R, TK), _F32), pltpu.VMEM((R, TK), _F32),
                        pltpu.VMEM((R, LANES), _F32), pltpu.VMEM((R, LANES), _F32),
                        pltpu.VMEM((R, LANES), _F32)],
    )
    return pl.pallas_call(
        kern, grid_spec=grid_spec,
        out_shape=jax.ShapeDtypeStruct((B, Tq, QB_EXP_W), _MXU),
        compiler_params=_cparams(("parallel", "parallel")),
        name="dsa",
    )(nk, qi, kiwi, qb, kir, kb, vb, kirt, kbt, vbt, qc, main(kchunk), tail(kchunk))


def _post_kernel(x_ref, oa_ref, ob_ref, ga_ref, gb_ref, wpa_ref, wpb_ref, wo_ref, g_ref,
                 wr_ref, br_ref, x1_o, h2_o, route_o):
    pa = jnp.dot(oa_ref[...], wpa_ref[...], preferred_element_type=_F32)
    pb = jnp.dot(ob_ref[...], wpb_ref[...], preferred_element_type=_F32)
    merged = ga_ref[...] * pa + gb_ref[...] * pb
    x1 = x_ref[...] + jnp.dot(merged.astype(wo_ref.dtype), wo_ref[...], preferred_element_type=_F32)
    x1_o[...] = x1
    ms = jnp.mean(x1 * x1, axis=-1, keepdims=True)
    h2 = (x1 * lax.rsqrt(ms + RMS_EPS) * g_ref[...]).astype(h2_o.dtype)
    h2_o[...] = h2
    cur = jnp.dot(h2, wr_ref[...], preferred_element_type=_F32) + br_ref[...]
    lane = lax.broadcasted_iota(jnp.int32, cur.shape, 1).astype(_F32)
    vals, idxs = [], []
    for _ in range(TOP_K_EXPERTS):
        m = jnp.max(cur, axis=1, keepdims=True)
        idx = jnp.min(jnp.where(cur == m, lane, float(LANES)), axis=1, keepdims=True)
        vals.append(m)
        idxs.append(idx)
        cur = jnp.where(lane == idx, -jnp.inf, cur)
    es = [jnp.exp(v - vals[0]) for v in vals]
    denom = es[0] + es[1] + es[2] + es[3]
    route = jnp.zeros(cur.shape, _F32)
    for k in range(TOP_K_EXPERTS):
        route = jnp.where(lane == float(k), idxs[k], route)
        route = jnp.where(lane == float(TOP_K_EXPERTS + k), es[k] / denom, route)
    route_o[...] = route


def _post(x, oa, ob, ga, gb, w_pa, w_pb_exp, w_o, g_ffn, w_router, b_router, T, tm):
    B, _, D = x.shape
    assert T % tm == 0
    row = lambda w: pl.BlockSpec((None, tm, w), lambda b, i: (b, i, 0))
    wr = jnp.concatenate([w_router, jnp.zeros((D, LANES - N_EXPERTS), w_router.dtype)], axis=1).astype(_MXU)
    br = jnp.concatenate([b_router.astype(_F32), jnp.full((LANES - N_EXPERTS,), _NEG, _F32)]).reshape(1, LANES)
    return pl.pallas_call(
        _post_kernel,
        grid=(B, T // tm),
        in_specs=[row(D), row(VA_W), row(QB_EXP_W), row(D), row(D),
                  _const_spec((VA_W, D)), _const_spec((QB_EXP_W, D)), _const_spec((D, D)),
                  _const_spec((1, D)), _const_spec((D, LANES)), _const_spec((1, LANES))],
        out_specs=[row(D), row(D), row(LANES)],
        out_shape=[jax.ShapeDtypeStruct((B, T, D), _F32), jax.ShapeDtypeStruct((B, T, D), _MXU),
                   jax.ShapeDtypeStruct((B, T, LANES), _F32)],
        compiler_params=_cparams(("parallel", "parallel")),
        name="post",
    )(x, oa, ob, ga, gb, w_pa.astype(_MXU), w_pb_exp, w_o.astype(_MXU), g_ffn.reshape(1, D), wr, br)


def _moe_kernel(be_ref, nu_ref, x_ref, wgu_ref, bgu_ref, wdn_ref, bdn_ref, y_ref, wgu_sc, wdn_sc):
    i = pl.program_id(0)
    used = i < nu_ref[0]
    new_expert = (i == 0) | (be_ref[i] != be_ref[jnp.maximum(i - 1, 0)])

    @pl.when(used & new_expert)
    def _():
        wgu_sc[...] = wgu_ref[...].astype(wgu_sc.dtype)
        wdn_sc[...] = wdn_ref[...].astype(wdn_sc.dtype)

    @pl.when(used)
    def _():
        gu = jnp.dot(x_ref[...], wgu_sc[...], preferred_element_type=_F32) + bgu_ref[...]
        glu = jnp.minimum(gu[:, :D_EXPERT], SWIGLU_LIMIT)
        lin = jnp.clip(gu[:, D_EXPERT:], -SWIGLU_LIMIT, SWIGLU_LIMIT)
        act = glu * jax.nn.sigmoid(SWIGLU_ALPHA * glu) * (lin + 1.0)
        y_ref[...] = jnp.dot(act.astype(wdn_sc.dtype), wdn_sc[...],
                             preferred_element_type=_F32) + bdn_ref[...]

    @pl.when(jnp.logical_not(used))
    def _():
        y_ref[...] = jnp.zeros(y_ref.shape, y_ref.dtype)


def _moe(h2, route, w_gu, b_gu, w_dn, b_dn):
    T, D = h2.shape
    K = TOP_K_EXPERTS
    A = T * K
    NB = -(-A // MOE_BLOCK) + N_EXPERTS
    NS = NB * MOE_BLOCK
    e_flat = route[:, :K].astype(jnp.int32).reshape(-1)
    experts = jnp.arange(N_EXPERTS, dtype=jnp.int32)
    counts = jnp.sum((e_flat[:, None] == experts[None, :]).astype(jnp.int32), axis=0)
    blocks_per = (counts + MOE_BLOCK - 1) // MOE_BLOCK
    cum_blocks = jnp.cumsum(blocks_per)
    blk = jnp.arange(NB, dtype=jnp.int32)
    block_expert = jnp.minimum(jnp.sum((cum_blocks[None, :] <= blk[:, None]).astype(jnp.int32), axis=1),
                               N_EXPERTS - 1)
    n_used = cum_blocks[-1:].astype(jnp.int32)
    cum_pad = jnp.cumsum(blocks_per * MOE_BLOCK - counts)
    r = jnp.arange(NS - A, dtype=jnp.int32)
    empty_expert = jnp.sum((cum_pad[None, :] <= r[:, None]).astype(jnp.int32), axis=1)
    src_bits = (NS - 1).bit_length()
    assert (2 * N_EXPERTS + 2) << src_bits < 2 ** 31
    group = jnp.concatenate([2 * e_flat, 2 * empty_expert + 1])
    slot_src = lax.sort((group << src_bits) | jnp.arange(NS, dtype=jnp.int32)) & ((1 << src_bits) - 1)
    slot_tok = jnp.where(slot_src < A, slot_src // K, (slot_src - A) % T)
    x_slots = h2[slot_tok]
    y_slots = pl.pallas_call(
        _moe_kernel,
        grid_spec=pltpu.PrefetchScalarGridSpec(
            num_scalar_prefetch=2,
            grid=(NB,),
            in_specs=[pl.BlockSpec((MOE_BLOCK, D), lambda i, be, nu: (i, 0)),
                      pl.BlockSpec((None, D, 2 * D_EXPERT), lambda i, be, nu: (be[i], 0, 0)),
                      pl.BlockSpec((None, 1, 2 * D_EXPERT), lambda i, be, nu: (be[i], 0, 0)),
                      pl.BlockSpec((None, D_EXPERT, D), lambda i, be, nu: (be[i], 0, 0)),
                      pl.BlockSpec((None, 1, D), lambda i, be, nu: (be[i], 0, 0))],
            out_specs=pl.BlockSpec((MOE_BLOCK, D), lambda i, be, nu: (i, 0)),
            scratch_shapes=[pltpu.VMEM((D, 2 * D_EXPERT), _MXU), pltpu.VMEM((D_EXPERT, D), _MXU)],
        ),
        out_shape=jax.ShapeDtypeStruct((NB * MOE_BLOCK, D), _F32),
        compiler_params=_cparams(("arbitrary",)),
        name="moe",
    )(block_expert, n_used, x_slots, w_gu, b_gu.reshape(N_EXPERTS, 1, -1).astype(_F32),
      w_dn, b_dn.reshape(N_EXPERTS, 1, -1).astype(_F32))
    _, dest = lax.sort((slot_src, jnp.arange(NS, dtype=jnp.int32)), num_keys=1)
    dest = dest[:A].reshape(T, K)
    return [y_slots[dest[:, k]] for k in range(K)]


def _final_kernel(x_ref, y0_ref, y1_ref, y2_ref, y3_ref, route_ref, g_ref, o_ref):
    K = TOP_K_EXPERTS
    x = x_ref[...]
    for k, y_ref in enumerate((y0_ref, y1_ref, y2_ref, y3_ref)):
        x = x + route_ref[:, K + k:K + k + 1] * y_ref[...]
    ms = jnp.mean(x * x, axis=-1, keepdims=True)
    o_ref[...] = x * lax.rsqrt(ms + RMS_EPS) * g_ref[...]


def _final(x1, ys, route, g_final, tm):
    T, D = x1.shape
    assert T % tm == 0
    row = pl.BlockSpec((tm, D), lambda i: (i, 0))
    return pl.pallas_call(
        _final_kernel, grid=(T // tm,),
        in_specs=[row] * 5 + [pl.BlockSpec((tm, LANES), lambda i: (i, 0)), _const_spec((1, D))],
        out_specs=row,
        out_shape=jax.ShapeDtypeStruct((T, D), _F32),
        compiler_params=_cparams(("parallel",)),
        name="final",
    )(x1, *ys, route, g_final.reshape(1, D))


def _group(x_rows, proj, keys, tails, qchunk, kchunk, korder, topk, Tq, tiles, weights, lam_init):
    (lam_vecs, subln_w, w_pa, w_pb_exp, w_o, g_ffn, w_router, b_router,
     w_gu, b_gu, w_dn, b_dn, g_final) = weights
    B = x_rows.shape[0]
    oa = _diff_attn(proj["qa"], keys["ka"], keys["va"], tails["ka"], tails["va"], qchunk, kchunk,
                    lam_vecs, subln_w, lam_init, Tq, tiles["tq_a"], tiles["tk"])
    ob = _dsa(proj["qi"], proj["wi"], proj["qb"], keys["kir"], keys["kb"], keys["vb"],
              tails["kir"], tails["kb"], tails["vb"], qchunk, kchunk, korder, topk, Tq,
              tiles["tq_b"], tiles["tk"])
    x1, h2, route = _post(x_rows, oa, ob, proj["ga"], proj["gb"], w_pa, w_pb_exp, w_o, g_ffn,
                          w_router, b_router, Tq, tiles["tm_post"])
    T = B * Tq
    route = route.reshape(T, LANES)
    ys = _moe(h2.reshape(T, D_MODEL), route, w_gu, b_gu, w_dn, b_dn)
    y = _final(x1.reshape(T, D_MODEL), ys, route, g_final, tiles["tm_post"])
    return y.reshape(B, Tq, D_MODEL)


_KEY_OPERANDS = ("ka", "va", "kb", "vb", "kir")


def kernel(x_prompt, x_sample, cache_a_k, cache_a_v, cache_b_k, cache_b_v, cache_idx_k,
           meta_tokens, g_mix, w_in, lambda_q1, lambda_k1, lambda_q2, lambda_k2, subln_w,
           w_pa, w_pb, w_o, g_ffn, w_router, b_router, w_gu, b_gu, w_dn, b_dn, g_final):
    B, S, D = x_prompt.shape
    Bs, S_dec, _ = x_sample.shape
    P = cache_a_k.shape[2]
    depth = g_mix.shape[0]
    assert depth == 1
    l = 0
    lam_init = 0.8 - 0.6 * math.exp(-0.3 * l)
    w_lay = _layout_w_in(w_in[l])
    w_pb_exp = _layout_w_pb(w_pb[l])
    lam_vecs = jnp.stack([lambda_q1[l], lambda_k1[l], lambda_q2[l], lambda_k2[l]]).astype(_F32)
    weights = (lam_vecs, subln_w[l], w_pa[l], w_pb_exp, w_o[l], g_ffn[l], w_router[l], b_router[l],
               w_gu[l], b_gu[l], w_dn[l], b_dn[l], g_final)

    r = np.arange(S + TAIL)
    chunk_p = np.where(r < S, r // CHUNK, np.where(r < S + N_META, -1, _BIG_CHUNK)).astype(np.int64)
    order_p = np.where(r < S, r + N_META, r - S).astype(np.int32)
    tiles_p = dict(tq_a=min(512, S), tq_b=min(128, S), tk=min(512, S), tm_post=min(512, S))
    meta_tile = jnp.concatenate([meta_tokens.astype(x_prompt.dtype),
                                 jnp.zeros((TAIL - N_META, D), x_prompt.dtype)])[None]
    pos_m = np.where(np.arange(TAIL) < N_META, np.arange(TAIL), 0).astype(np.int32)
    pm = _project(meta_tile, jnp.asarray(pos_m), g_mix[l], w_lay, TAIL)
    meta_rows = {n: pm[n][0, :N_META] for n in ("ka_f", "va_f", "kb_f", "vb_f", "kiwi")}
    pp = _project(x_prompt, jnp.asarray(N_META + np.arange(S, dtype=np.int32)), g_mix[l], w_lay,
                  tiles_p["tm_post"], head_rows=meta_rows)
    y_prompt = _group(x_prompt, pp, pp, pm, chunk_p[:S], chunk_p, order_p, min(TOPK_MAX, S // 4), S,
                      tiles_p, weights, lam_init)

    def rows_p(name, shape, width=None):
        return pp[name][:, :, :width].reshape((1, B, N_META + S) + shape)

    pos_s = (P + np.arange(S_dec)).astype(np.int32)
    ps = _project(x_sample, jnp.asarray(pos_s), g_mix[l], w_lay, S_dec)
    r = np.arange(P + TAIL)
    chunk_s = np.where(r < P + S_dec, r // CHUNK, _BIG_CHUNK).astype(np.int64)
    qchunk_s = (pos_s // CHUNK).astype(np.int64)
    cached = dict(ka=cache_a_k[l], va=cache_a_v[l], kb=cache_b_k[l], vb=cache_b_v[l],
                  kir=jnp.tile(cache_idx_k[l], (1, 1, N_IDX_HEADS)))
    keys_s = {n: cached[n].reshape(Bs, P, -1).astype(_MXU) for n in _KEY_OPERANDS}
    tails_s = {n: jnp.concatenate([ps[n], jnp.zeros((Bs, TAIL - S_dec, ps[n].shape[-1]), _MXU)], axis=1)
               for n in _KEY_OPERANDS}
    tiles_s = dict(tq_a=S_dec, tq_b=S_dec, tk=min(512, P), tm_post=S_dec)
    y_sample = _group(x_sample, ps, keys_s, tails_s, qchunk_s, chunk_s, r.astype(np.int32),
                      min(TOPK_MAX, (P + S_dec) // 4), S_dec, tiles_s, weights, lam_init)

    def rows_s(name, shape, width=None):
        return ps[name][:, :, :width].reshape((1, Bs, S_dec) + shape)

    ha, hb = (N_HEADS_A, 2 * HEAD_DIM_A), (N_KV_B, HEAD_DIM_B)
    return (y_prompt, y_sample,
            rows_p("ka_f", ha), rows_p("va_f", ha), rows_p("kb_f", hb), rows_p("vb_f", hb),
            rows_p("kiwi", (IDX_DIM,), IDX_DIM),
            rows_s("ka_f", ha), rows_s("va_f", ha), rows_s("kb_f", hb), rows_s("vb_f", hb),
            rows_s("kiwi", (IDX_DIM,), IDX_DIM))
```

```python
import functools
import math

import numpy as np
import jax
import jax.numpy as jnp
from jax import lax
from jax.experimental import pallas as pl
from jax.experimental.pallas import tpu as pltpu

D_MODEL = 1024
CHUNK = 64
N_META = 16
ROPE_THETA = 500000.0
RMS_EPS = 1e-6
N_HEADS_A = 4
HEAD_DIM_A = 64
N_HEADS_B = 8
N_KV_B = 2
HEAD_DIM_B = 64
N_IDX_HEADS = 8
IDX_DIM = 32
TOPK_MAX = 256
N_EXPERTS = 32
TOP_K_EXPERTS = 4
D_EXPERT = 1024
SWIGLU_LIMIT = 7.0
SWIGLU_ALPHA = 1.702
MOE_BLOCK = 512

QA_W = N_HEADS_A * 2 * HEAD_DIM_A
VA_W = QA_W
QB_W = N_HEADS_B * HEAD_DIM_B
KB_W = N_KV_B * HEAD_DIM_B
QI_W = N_IDX_HEADS * IDX_DIM
SPLITS = (QA_W, QA_W, VA_W, QB_W, KB_W, KB_W, QI_W, IDX_DIM, N_IDX_HEADS, D_MODEL, D_MODEL)

LANES = 128
TAIL = LANES
QB_EXP_W = N_HEADS_B * LANES

_SEC = {}
_off = 0
for _name, _w in (("qa", QA_W), ("ka", QA_W), ("va", VA_W), ("qb", QB_EXP_W), ("kb", KB_W),
                  ("vb", KB_W), ("qi", QI_W), ("kiwi", LANES), ("kir", QI_W),
                  ("ga", D_MODEL), ("gb", D_MODEL)):
    _SEC[_name] = (_off, _off + _w)
    _off += _w
W_LAYOUT = _off

_MXU = jnp.bfloat16
_F32 = jnp.float32
_NEG = -1e30
_BIG_CHUNK = 2 ** 30
_LOG2E = math.log2(math.e)
_HALF = 2 ** 15
_KEY_NEG_INF = int(np.int32(np.uint32(0xFF800000) ^ np.uint32(0x7FFFFFFF)))
_VMEM_LIMIT = 56 * 1024 * 1024


def _cparams(sem):
    return pltpu.CompilerParams(dimension_semantics=sem, vmem_limit_bytes=_VMEM_LIMIT)


def _const_spec(shape):
    nd = len(shape)
    return pl.BlockSpec(shape, lambda *_: (0,) * nd)


def _nt_dot(a, b):
    return lax.dot_general(a, b, (((1,), (1,)), ((), ())), preferred_element_type=_F32)


def _sortable(x):
    b = lax.bitcast_convert_type(x, jnp.int32)
    return b ^ ((b >> 31) & 0x7FFFFFFF)


def _rope_block(z, cos, sin_lo, sin_hi, half):
    return (z * cos + pltpu.roll(z, LANES - half, 1) * sin_lo + pltpu.roll(z, half, 1) * sin_hi)


def _proj_kernel(x_ref, g_ref, w_ref, tab_ref, *refs):
    (qa_o, kaf_o, kab_o, vaf_o, vab_o, qb_o, kbf_o, kbb_o, vbf_o, vbb_o,
     qi_o, kiwi_o, kir_o, ga_o, gb_o, wi_o) = refs[-16:]
    kaf_o, vaf_o, kbf_o, vbf_o, kiwi_o = (o.at[0] if len(o.shape) == 3 else o
                                          for o in (kaf_o, vaf_o, kbf_o, vbf_o, kiwi_o))
    x = x_ref[...]
    ms = jnp.mean(x * x, axis=-1, keepdims=True)
    h = (x * lax.rsqrt(ms + RMS_EPS) * g_ref[...]).astype(w_ref.dtype)

    def sec(name):
        a, b = _SEC[name]
        return jnp.dot(h, w_ref[:, a:b], preferred_element_type=_F32)

    t64 = tuple(tab_ref[:, k * LANES:(k + 1) * LANES] for k in (0, 1, 2))
    t32 = tuple(tab_ref[:, k * LANES:(k + 1) * LANES] for k in (3, 4, 5))
    tki = tuple(tab_ref[:, k * LANES:(k + 1) * LANES] for k in (6, 7, 8))

    def roped(z, tab, half):
        return [_rope_block(z[:, c * LANES:(c + 1) * LANES], *tab, half)
                for c in range(z.shape[1] // LANES)]

    scale_a = HEAD_DIM_A ** -0.5 * _LOG2E
    scale_b = HEAD_DIM_B ** -0.5 * _LOG2E
    for c, blk in enumerate(roped(sec("qa"), t64, 8)):
        qa_o[:, c * LANES:(c + 1) * LANES] = (blk * scale_a).astype(qa_o.dtype)
    for c, blk in enumerate(roped(sec("ka"), t64, 8)):
        kaf_o[:, c * LANES:(c + 1) * LANES] = blk
        kab_o[:, c * LANES:(c + 1) * LANES] = blk.astype(kab_o.dtype)
    va = sec("va")
    vaf_o[...] = va
    vab_o[...] = va.astype(vab_o.dtype)
    for c, blk in enumerate(roped(sec("qb"), t64, 8)):
        qb_o[:, c * LANES:(c + 1) * LANES] = (blk * scale_b).astype(qb_o.dtype)
    kb = roped(sec("kb"), t64, 8)[0]
    kbf_o[...] = kb
    kbb_o[...] = kb.astype(kbb_o.dtype)
    vb = sec("vb")
    vbf_o[...] = vb
    vbb_o[...] = vb.astype(vbb_o.dtype)
    for c, blk in enumerate(roped(sec("qi"), t32, 4)):
        qi_o[:, c * LANES:(c + 1) * LANES] = blk.astype(qi_o.dtype)
    kiwi = roped(sec("kiwi"), tki, 4)[0]
    kiwi_o[...] = kiwi
    wi_o[...] = kiwi
    for c, blk in enumerate(roped(sec("kir"), t32, 4)):
        kir_o[:, c * LANES:(c + 1) * LANES] = blk.astype(kir_o.dtype)
    ga_o[...] = jax.nn.sigmoid(sec("ga"))
    gb_o[...] = jax.nn.sigmoid(sec("gb"))


def _rope_tables(pos):
    def cs(d):
        rd = d // 4
        half = rd // 2
        inv_freq = ROPE_THETA ** (-jnp.arange(half, dtype=_F32) * 2.0 / rd)
        ang = pos.astype(_F32)[:, None] * inv_freq[None, :]
        return jnp.cos(ang), jnp.sin(ang), half

    T = pos.shape[0]

    def pattern(d, width):
        c, s, half = cs(d)
        one = jnp.ones((T, width - 2 * half), _F32)
        zero = lambda n: jnp.zeros((T, n), _F32)
        cos = jnp.concatenate([c, c, one], axis=1)
        lo = jnp.concatenate([-s, zero(width - half)], axis=1)
        hi = jnp.concatenate([zero(half), s, zero(width - 2 * half)], axis=1)
        return cos, lo, hi

    tabs = []
    for d, width in ((64, 64), (32, 32), (32, LANES)):
        tabs += [jnp.tile(t, (1, LANES // width)) for t in pattern(d, width)]
    return jnp.concatenate(tabs, axis=1)


def _layout_w_in(w_in):
    offs = np.cumsum((0,) + SPLITS)
    qa, ka, va, qb, kb, vb, qi, ki, wi, ga, gb = (w_in[:, offs[i]:offs[i + 1]] for i in range(11))
    D = w_in.shape[0]
    z64 = jnp.zeros((D, HEAD_DIM_B), w_in.dtype)
    per_kv = N_HEADS_B // N_KV_B
    qb_exp = []
    for hq in range(N_HEADS_B):
        col = qb[:, hq * HEAD_DIM_B:(hq + 1) * HEAD_DIM_B]
        qb_exp += [col, z64] if hq // per_kv == 0 else [z64, col]
    kiwi = jnp.concatenate([ki, wi, jnp.zeros((D, LANES - IDX_DIM - N_IDX_HEADS), w_in.dtype)], axis=1)
    kir = jnp.tile(ki, (1, N_IDX_HEADS))
    w = jnp.concatenate([qa, ka, va] + qb_exp + [kb, vb, qi, kiwi, kir, ga, gb], axis=1)
    assert w.shape[1] == W_LAYOUT
    return w.astype(_MXU)


def _layout_w_pb(w_pb):
    D = w_pb.shape[1]
    z64 = jnp.zeros((HEAD_DIM_B, D), w_pb.dtype)
    per_kv = N_HEADS_B // N_KV_B
    rows = []
    for hq in range(N_HEADS_B):
        r = w_pb[hq * HEAD_DIM_B:(hq + 1) * HEAD_DIM_B]
        rows += [r, z64] if hq // per_kv == 0 else [z64, r]
    return jnp.concatenate(rows, axis=0).astype(_MXU)


def _project(x, pos, g_mix, w_lay, tm, head_rows=None):
    B, T, D = x.shape
    assert T % tm == 0
    off = next(iter(head_rows.values())).shape[0] if head_rows else 0
    total = off + T
    assert off % 8 == 0
    tab = _rope_tables(pos)
    row = lambda w: pl.BlockSpec((None, tm, w), lambda b, i: (b, i, 0))
    shifted = lambda w: pl.BlockSpec((pl.Element(1), pl.Element(tm), pl.Element(w)),
                                     lambda b, i: (b, pl.multiple_of(off + i * tm, 8), 0))
    outs = [("qa", QA_W, _MXU), ("ka_f", QA_W, _F32), ("ka", QA_W, _MXU), ("va_f", VA_W, _F32),
            ("va", VA_W, _MXU), ("qb", QB_EXP_W, _MXU), ("kb_f", KB_W, _F32), ("kb", KB_W, _MXU),
            ("vb_f", KB_W, _F32), ("vb", KB_W, _MXU), ("qi", QI_W, _MXU), ("kiwi", LANES, _F32),
            ("kir", QI_W, _MXU), ("ga", D_MODEL, _F32), ("gb", D_MODEL, _F32), ("wi", LANES, _F32)]
    is_cache = lambda n: n.endswith("_f") or n == "kiwi"
    n_in = 4
    heads, aliases = [], {}
    for k, (n, w, _) in enumerate(outs):
        if head_rows and is_cache(n):
            rows = jnp.broadcast_to(head_rows[n][None], (B, off, w))
            heads.append(jnp.concatenate([rows, jnp.zeros((B, T, w), _F32)], axis=1))
            aliases[n_in + len(heads) - 1] = k
    res = pl.pallas_call(
        _proj_kernel,
        grid=(B, T // tm),
        in_specs=[row(D), _const_spec((1, D)),
                  pl.BlockSpec((D, W_LAYOUT), lambda b, i: (0, 0), pipeline_mode=pl.Buffered(1)),
                  pl.BlockSpec((tm, 9 * LANES), lambda b, i: (i, 0))]
                 + [pl.BlockSpec(memory_space=pl.ANY)] * len(heads),
        out_specs=[shifted(w) if (is_cache(n) and off) else row(w) for n, w, _ in outs],
        out_shape=[jax.ShapeDtypeStruct((B, total if is_cache(n) else T, w), dt) for n, w, dt in outs],
        input_output_aliases=aliases,
        compiler_params=_cparams(("parallel", "parallel")),
        name="proj",
    )(x, g_mix.reshape(1, D), w_lay, tab, *heads)
    return {n: r for (n, _, _), r in zip(outs, res)}


def _tile_lanes(v, n):
    return v if n == 1 else jnp.concatenate([v] * n, axis=1)


def _softmax_step(s, v, m_sc, l_sc, acc_sc):
    m_prev = m_sc[...]
    m_new = jnp.maximum(m_prev, jnp.max(s, axis=1, keepdims=True))
    alpha = jnp.exp2(m_prev - m_new)
    p = jnp.exp2(s - _tile_lanes(m_new, s.shape[1] // LANES))
    l_sc[...] = alpha * l_sc[...] + jnp.sum(p, axis=1, keepdims=True)
    acc_sc[...] = alpha * acc_sc[...] + jnp.dot(p.astype(v.dtype), v, preferred_element_type=_F32)
    m_sc[...] = m_new


def _attend_tiles(n, scores, values, s0_sc, s1_sc, m_sc, l_sc, acc_sc):
    s0_sc[...] = scores(0)

    def two(j):
        s1_sc[...] = scores(j + 1)
        _softmax_step(s0_sc[...], values(j), m_sc, l_sc, acc_sc)
        s0_sc[...] = scores(jnp.minimum(j + 2, n - 1))
        _softmax_step(s1_sc[...], values(j + 1), m_sc, l_sc, acc_sc)

    def quad(p, carry):
        two(4 * p)
        two(4 * p + 2)
        return carry

    lax.fori_loop(0, n // 4, quad, 0)
    done = (n // 4) * 4

    @pl.when(n - done >= 2)
    def _():
        two(done)

    @pl.when(n % 2 == 1)
    def _():
        _softmax_step(s0_sc[...], values(n - 1), m_sc, l_sc, acc_sc)


def _for_tiles(n, tile_fn):
    def quad(p, carry):
        for t in range(4):
            tile_fn(4 * p + t)
        return carry

    lax.fori_loop(0, n // 4, quad, 0)
    done = (n // 4) * 4

    @pl.when(n - done >= 2)
    def _():
        tile_fn(done)
        tile_fn(done + 1)

    @pl.when(n % 2 == 1)
    def _():
        tile_fn(n - 1)


def _num_full_tiles(qchunk, kchunk_main, TQ, TK):
    last = kchunk_main.reshape(-1, TK).max(axis=1)
    qmin = qchunk.reshape(-1, TQ).min(axis=1)
    full = last[None, :] <= qmin[:, None]
    return np.cumprod(full, axis=1).sum(axis=1).astype(np.int32)


def _num_key_tiles(qchunk, kchunk_main, TQ, TK):
    first = kchunk_main.reshape(-1, TK)[:, 0]
    qmax = qchunk.reshape(-1, TQ).max(axis=1)
    return (first[None, :] <= qmax[:, None]).sum(axis=1).astype(np.int32)


def _diff_attn_kernel(nk_ref, nf_ref, q_ref, k_ref, v_ref, kt_ref, vt_ref, qc_ref, kcm_ref, kct_ref,
                      lam_ref, sub_ref, o_ref, s0_sc, s1_sc, m_sc, l_sc, acc_sc, *, TQ, TK, lam_init):
    i = pl.program_id(2)
    q = q_ref[...]
    lane = lax.broadcasted_iota(jnp.int32, q.shape, 1)
    zero = jnp.zeros_like(q)
    qs = jnp.concatenate([jnp.where(lane < HEAD_DIM_A, q, zero),
                          jnp.where(lane >= HEAD_DIM_A, q, zero)], axis=0)
    qc = qc_ref[...]
    qc2 = jnp.concatenate([qc, qc], axis=0)
    m_sc[...] = jnp.full(m_sc.shape, _NEG, _F32)
    l_sc[...] = jnp.zeros(l_sc.shape, _F32)
    acc_sc[...] = jnp.zeros(acc_sc.shape, _F32)

    def keys(j):
        return k_ref[pl.ds(pl.multiple_of(j * TK, TK), TK), :]

    def values(j):
        return v_ref[pl.ds(pl.multiple_of(j * TK, TK), TK), :]

    def masked_scores(k, kc):
        return jnp.where(kc <= qc2, _nt_dot(qs, k), _NEG)

    _attend_tiles(nf_ref[i], lambda j: _nt_dot(qs, keys(j)), values,
                  s0_sc, s1_sc, m_sc, l_sc, acc_sc)

    def diagonal(j, carry):
        _softmax_step(masked_scores(keys(j), kcm_ref[j]), values(j), m_sc, l_sc, acc_sc)
        return carry

    lax.fori_loop(nf_ref[i], nk_ref[i], diagonal, 0)
    _softmax_step(masked_scores(kt_ref[...], kct_ref[...]), vt_ref[...], m_sc, l_sc, acc_sc)

    lam = (jnp.exp(jnp.sum(lam_ref[0:1, :] * lam_ref[1:2, :], axis=1, keepdims=True))
           - jnp.exp(jnp.sum(lam_ref[2:3, :] * lam_ref[3:4, :], axis=1, keepdims=True))
           + lam_init)
    o = acc_sc[...] / l_sc[...]
    d = o[:TQ] - lam * o[TQ:]
    ms = jnp.mean(d * d, axis=-1, keepdims=True)
    o_ref[...] = ((d * lax.rsqrt(ms + RMS_EPS) * sub_ref[...]) * (1.0 - lam_init)).astype(o_ref.dtype)


def _tail_spec(tail, width, index):
    per_batch = tail.shape[0] > 1
    return pl.BlockSpec((None, TAIL, width), lambda b, *rest: (b if per_batch else 0, 0, index(*rest)))


def _diff_attn(q, k, v, kt, vt, qchunk, kchunk, lam_vecs, subln_w, lam_init, Tq, TQ, TK):
    B = q.shape[0]
    Lmain = k.shape[1]
    L = Lmain
    assert Tq % TQ == 0 and Lmain % TK == 0 and kchunk.shape[0] == Lmain + TAIL
    nq, nt = Tq // TQ, Lmain // TK
    nk = jnp.asarray(_num_key_tiles(qchunk[:Tq], kchunk[:Lmain], TQ, TK))
    nf = jnp.asarray(_num_full_tiles(qchunk[:Tq], kchunk[:Lmain], TQ, TK))
    qc = jnp.asarray(qchunk[:Tq].reshape(Tq, 1).astype(np.int32))
    kcm = jnp.asarray(kchunk[:Lmain].reshape(nt, 1, TK).astype(np.int32))
    kct = jnp.asarray(kchunk[Lmain:].reshape(1, TAIL).astype(np.int32))
    R = 2 * TQ
    kern = functools.partial(_diff_attn_kernel, TQ=TQ, TK=TK, lam_init=lam_init)
    head = lambda h, i, *_: h
    grid_spec = pltpu.PrefetchScalarGridSpec(
        num_scalar_prefetch=2,
        grid=(B, N_HEADS_A, nq),
        in_specs=[pl.BlockSpec((None, TQ, LANES), lambda b, h, i, *_: (b, i, h)),
                  pl.BlockSpec((None, L, LANES), lambda b, h, i, *_: (b, 0, h)),
                  pl.BlockSpec((None, L, LANES), lambda b, h, i, *_: (b, 0, h)),
                  _tail_spec(kt, LANES, head), _tail_spec(vt, LANES, head),
                  pl.BlockSpec((TQ, 1), lambda b, h, i, *_: (i, 0)),
                  pl.BlockSpec((nt, 1, TK), lambda b, h, i, *_: (0, 0, 0)),
                  pl.BlockSpec((1, TAIL), lambda b, h, i, *_: (0, 0)),
                  pl.BlockSpec((4, HEAD_DIM_A), lambda b, h, i, *_: (0, 0)),
                  pl.BlockSpec((1, LANES), lambda b, h, i, *_: (0, 0))],
        out_specs=pl.BlockSpec((None, TQ, LANES), lambda b, h, i, *_: (b, i, h)),
        scratch_shapes=[pltpu.VMEM((R, TK), _F32)] * 2 + [pltpu.VMEM((R, LANES), _F32)] * 3,
    )
    return pl.pallas_call(
        kern, grid_spec=grid_spec,
        out_shape=jax.ShapeDtypeStruct((B, Tq, VA_W), _MXU),
        compiler_params=_cparams(("parallel", "parallel", "parallel")),
        name="diff_attn",
    )(nk, nf, q, k, v, kt, vt, qc, kcm, kct, lam_vecs, subln_w.reshape(1, LANES))


def _dsa_kernel(nk_ref, qi_ref, kiwi_ref, qb_ref, kir_ref, kb_ref, vb_ref, kirt_ref, kbt_ref, vbt_ref,
                qc_ref, kcm_ref, kct_ref, o_ref,
                key_sc, keyt_sc, hi_sc, lo_sc, hit_sc, lot_sc, s0_sc, s1_sc, m_sc, l_sc, acc_sc,
                *, TQ, TK, topk, tail_first):
    i = pl.program_id(1)
    nk = nk_ref[i]
    H = N_IDX_HEADS
    nblk = TK // LANES

    qi = qi_ref[...]
    head = lax.broadcasted_iota(jnp.int32, qi.shape, 1) // IDX_DIM
    zero = jnp.zeros_like(qi)
    qis = jnp.concatenate([jnp.where(head == h, qi, zero) for h in range(H)], axis=0)
    wi = kiwi_ref[:, IDX_DIM:IDX_DIM + H] * (H ** -0.5) * (IDX_DIM ** -0.5)
    wi_b = [jnp.broadcast_to(wi[:, h:h + 1], (TQ, LANES)) for h in range(H)]
    qc = qc_ref[...]

    def score_keys(lg, kc):
        n = lg.shape[1] // LANES
        sc = _tile_lanes(wi_b[0], n) * jnp.maximum(lg[0:TQ], 0.0)
        for h in range(1, H):
            sc = sc + _tile_lanes(wi_b[h], n) * jnp.maximum(lg[h * TQ:(h + 1) * TQ], 0.0)
        return _sortable(jnp.where(kc <= qc, sc, -jnp.inf))

    def split16(key):
        return (key >> 16).astype(jnp.int16), ((key & 0xFFFF) - _HALF).astype(jnp.int16)

    def fill(j):
        lg = _nt_dot(qis, kir_ref[pl.ds(pl.multiple_of(j * TK, TK), TK), :])
        key = score_keys(lg, kcm_ref[j])
        key_sc[j] = key
        hi_sc[j], lo_sc[j] = split16(key)

    _for_tiles(nk, fill)
    key_t = score_keys(_nt_dot(qis, kirt_ref[...]), kct_ref[...])
    keyt_sc[...] = key_t
    hit_sc[...], lot_sc[...] = split16(key_t)

    def count(ind_main, ind_tail, dtype):
        def body(j, c):
            ind = ind_main(j)
            for b in range(nblk):
                c = c + ind[:, b * LANES:(b + 1) * LANES]
            return c
        c = lax.fori_loop(0, nk, body, jnp.zeros((TQ, LANES), dtype)) + ind_tail()
        return jnp.sum(c.astype(_F32), axis=1, keepdims=True)

    one16, zero16 = jnp.int16(1), jnp.int16(0)

    def lanes16(v):
        return jnp.broadcast_to(v, (TQ, LANES)).astype(jnp.int16)

    def count16(main_sc, tail_sc, t16, strict=False):
        tm = _tile_lanes(t16, nblk)
        cmp = (lambda a, b: a > b) if strict else (lambda a, b: a >= b)
        return count(lambda j: jnp.where(cmp(main_sc[j], tm), one16, zero16),
                     lambda: jnp.where(cmp(tail_sc[...], t16), one16, zero16), jnp.int16)

    def kth_half(main_sc, tail_sc, target):
        def step(it, t_u):
            cand_u = t_u | jnp.left_shift(jnp.int32(1), 15 - it)
            cnt = count16(main_sc, tail_sc, lanes16(cand_u - _HALF))
            return jnp.where(cnt >= target, cand_u, t_u)
        return lax.fori_loop(0, 16, step, jnp.zeros((TQ, 1), jnp.int32))

    kf = float(topk)
    t_hi = kth_half(hi_sc, hit_sc, kf) - _HALF
    t_hi16 = lanes16(t_hi)
    n_above = count16(hi_sc, hit_sc, t_hi16, strict=True)
    t_hi16m = _tile_lanes(t_hi16, nblk)
    low16 = jnp.int16(-_HALF)

    def bucket(j, carry):
        lo_sc[j] = jnp.where(hi_sc[j] == t_hi16m, lo_sc[j], low16)
        return carry

    lax.fori_loop(0, nk, bucket, 0)
    lot_sc[...] = jnp.where(hit_sc[...] == t_hi16, lot_sc[...], low16)
    t_lo = kth_half(lo_sc, lot_sc, kf - n_above)
    n_ge = n_above + count16(lo_sc, lot_sc, lanes16(t_lo - _HALF))
    kth = (t_hi << 16) | t_lo
    thr1 = jnp.maximum(kth, _KEY_NEG_INF + 1)
    thr = jnp.broadcast_to(thr1, (TQ, LANES))
    thr_m = _tile_lanes(thr, nblk)
    excess = jnp.broadcast_to((n_ge > kf) & (kth > _KEY_NEG_INF), (TQ, LANES))

    @pl.when(jnp.max(jnp.where(excess, 1.0, 0.0)) > 0.0)
    def _():
        quota = kf - count(lambda j: jnp.where(key_sc[j] > thr_m, 1.0, 0.0),
                           lambda: jnp.where(keyt_sc[...] > thr, 1.0, 0.0), _F32)
        r = lax.broadcasted_iota(jnp.int32, (LANES, LANES), 0)
        c = lax.broadcasted_iota(jnp.int32, (LANES, LANES), 1)
        upto = jnp.where(r <= c, 1.0, 0.0).astype(_MXU)
        ones = jnp.ones((LANES, LANES), _MXU)

        def keep_first(keys, seen):
            tie = keys == thr
            e = jnp.where(tie, 1.0, 0.0).astype(_MXU)
            rank = seen + jnp.dot(e, upto, preferred_element_type=_F32)
            keys = jnp.where(tie & (rank > quota), _KEY_NEG_INF, keys)
            return keys, seen + jnp.dot(e, ones, preferred_element_type=_F32)

        def tail_block(seen):
            keyt_sc[...], seen = keep_first(keyt_sc[...], seen)
            return seen

        def main_tile(j, seen):
            kj = key_sc[j]
            blocks = []
            for b in range(nblk):
                blk, seen = keep_first(kj[:, b * LANES:(b + 1) * LANES], seen)
                blocks.append(blk)
            key_sc[j] = jnp.concatenate(blocks, axis=1)
            return seen

        seen = jnp.zeros((TQ, LANES), _F32)
        if tail_first:
            seen = tail_block(seen)
        seen = lax.fori_loop(0, nk, main_tile, seen)
        if not tail_first:
            tail_block(seen)

    qs = jnp.concatenate([qb_ref[:, h * LANES:(h + 1) * LANES] for h in range(N_HEADS_B)], axis=0)
    m_sc[...] = jnp.full(m_sc.shape, _NEG, _F32)
    l_sc[...] = jnp.zeros(l_sc.shape, _F32)
    acc_sc[...] = jnp.zeros(acc_sc.shape, _F32)

    def masked_scores(keys, kb):
        bias = jnp.where(keys >= _tile_lanes(thr, keys.shape[1] // LANES), 0.0, _NEG)
        return _nt_dot(qs, kb) + jnp.concatenate([bias] * N_HEADS_B, axis=0)

    def scores(j):
        return masked_scores(key_sc[j], kb_ref[pl.ds(pl.multiple_of(j * TK, TK), TK), :])

    def values(j):
        return vb_ref[pl.ds(pl.multiple_of(j * TK, TK), TK), :]

    _attend_tiles(nk, scores, values, s0_sc, s1_sc, m_sc, l_sc, acc_sc)
    _softmax_step(masked_scores(keyt_sc[...], kbt_ref[...]), vbt_ref[...], m_sc, l_sc, acc_sc)
    o = acc_sc[...] / l_sc[...]
    for h in range(N_HEADS_B):
        o_ref[:, h * LANES:(h + 1) * LANES] = o[h * TQ:(h + 1) * TQ].astype(o_ref.dtype)


def _dsa(qi, kiwi, qb, kir, kb, vb, kirt, kbt, vbt, qchunk, kchunk, korder, topk, Tq, TQ, TK):
    B = qi.shape[0]
    L = Lmain = kir.shape[1]
    assert Tq % TQ == 0 and Lmain % TK == 0 and kchunk.shape[0] == Lmain + TAIL
    real = kchunk < _BIG_CHUNK
    order_main, order_tail = korder[:Lmain], korder[Lmain:][real[Lmain:]]
    assert np.all(np.diff(order_main) > 0) and np.all(np.diff(order_tail) > 0)
    tail_first = bool(order_tail[-1] < order_main[0])
    assert tail_first or order_tail[0] > order_main[-1]
    nq, nt = Tq // TQ, Lmain // TK
    nk = jnp.asarray(_num_key_tiles(qchunk[:Tq], kchunk[:Lmain], TQ, TK))
    qc = jnp.asarray(qchunk[:Tq].reshape(Tq, 1).astype(np.int32))
    main = lambda a: jnp.asarray(a[:Lmain].reshape(nt, 1, TK).astype(np.int32))
    tail = lambda a: jnp.asarray(a[Lmain:].reshape(1, TAIL).astype(np.int32))
    R = N_HEADS_B * TQ
    kern = functools.partial(_dsa_kernel, TQ=TQ, TK=TK, topk=topk, tail_first=tail_first)
    first = lambda *_: 0
    qrow = lambda w: pl.BlockSpec((None, TQ, w), lambda b, i, nk: (b, i, 0))
    krow = lambda w: pl.BlockSpec((None, L, w), lambda b, i, nk: (b, 0, 0),
                                  pipeline_mode=pl.Buffered(1))
    cm = pl.BlockSpec((nt, 1, TK), lambda b, i, nk: (0, 0, 0))
    ct = pl.BlockSpec((1, TAIL), lambda b, i, nk: (0, 0))
    grid_spec = pltpu.PrefetchScalarGridSpec(
        num_scalar_prefetch=1,
        grid=(B, nq),
        in_specs=[qrow(QI_W), qrow(LANES), qrow(QB_EXP_W), krow(QI_W), krow(KB_W), krow(KB_W),
                  _tail_spec(kirt, QI_W, first), _tail_spec(kbt, KB_W, first), _tail_spec(vbt, KB_W, first),
                  pl.BlockSpec((TQ, 1), lambda b, i, nk: (i, 0)), cm, ct],
        out_specs=qrow(QB_EXP_W),
        scratch_shapes=[pltpu.VMEM((nt, TQ, TK), jnp.int32), pltpu.VMEM((TQ, TAIL), jnp.int32),
                        pltpu.VMEM((nt, TQ, TK), jnp.int16), pltpu.VMEM((nt, TQ, TK), jnp.int16),
                        pltpu.VMEM((TQ, TAIL), jnp.int16), pltpu.VMEM((TQ, TAIL), jnp.int16),
                        pltpu.VMEM((R, TK), _F32), pltpu.VMEM((R, TK), _F32),
                        pltpu.VMEM((R, LANES), _F32), pltpu.VMEM((R, LANES), _F32),
                        pltpu.VMEM((R, LANES), _F32)],
    )
    return pl.pallas_call(
        kern, grid_spec=grid_spec,
        out_shape=jax.ShapeDtypeStruct((B, Tq, QB_EXP_W), _MXU),
        compiler_params=_cparams(("parallel", "parallel")),
        name="dsa",
    )(nk, qi, kiwi, qb, kir, kb, vb, kirt, kbt, vbt, qc, main(kchunk), tail(kchunk))


def _post_kernel(x_ref, oa_ref, ob_ref, ga_ref, gb_ref, wpa_ref, wpb_ref, wo_ref, g_ref,
                 wr_ref, br_ref, x1_o, h2_o, route_o):
    pa = jnp.dot(oa_ref[...], wpa_ref[...], preferred_element_type=_F32)
    pb = jnp.dot(ob_ref[...], wpb_ref[...], preferred_element_type=_F32)
    merged = ga_ref[...] * pa + gb_ref[...] * pb
    x1 = x_ref[...] + jnp.dot(merged.astype(wo_ref.dtype), wo_ref[...], preferred_element_type=_F32)
    x1_o[...] = x1
    ms = jnp.mean(x1 * x1, axis=-1, keepdims=True)
    h2 = (x1 * lax.rsqrt(ms + RMS_EPS) * g_ref[...]).astype(h2_o.dtype)
    h2_o[...] = h2
    cur = jnp.dot(h2, wr_ref[...], preferred_element_type=_F32) + br_ref[...]
    lane = lax.broadcasted_iota(jnp.int32, cur.shape, 1).astype(_F32)
    vals, idxs = [], []
    for _ in range(TOP_K_EXPERTS):
        m = jnp.max(cur, axis=1, keepdims=True)
        idx = jnp.min(jnp.where(cur == m, lane, float(LANES)), axis=1, keepdims=True)
        vals.append(m)
        idxs.append(idx)
        cur = jnp.where(lane == idx, -jnp.inf, cur)
    es = [jnp.exp(v - vals[0]) for v in vals]
    denom = es[0] + es[1] + es[2] + es[3]
    route = jnp.zeros(cur.shape, _F32)
    for k in range(TOP_K_EXPERTS):
        route = jnp.where(lane == float(k), idxs[k], route)
        route = jnp.where(lane == float(TOP_K_EXPERTS + k), es[k] / denom, route)
    route_o[...] = route


def _post(x, oa, ob, ga, gb, w_pa, w_pb_exp, w_o, g_ffn, w_router, b_router, T, tm):
    B, _, D = x.shape
    assert T % tm == 0
    row = lambda w: pl.BlockSpec((None, tm, w), lambda b, i: (b, i, 0))
    wr = jnp.concatenate([w_router, jnp.zeros((D, LANES - N_EXPERTS), w_router.dtype)], axis=1).astype(_MXU)
    br = jnp.concatenate([b_router.astype(_F32), jnp.full((LANES - N_EXPERTS,), _NEG, _F32)]).reshape(1, LANES)
    return pl.pallas_call(
        _post_kernel,
        grid=(B, T // tm),
        in_specs=[row(D), row(VA_W), row(QB_EXP_W), row(D), row(D),
                  _const_spec((VA_W, D)), _const_spec((QB_EXP_W, D)), _const_spec((D, D)),
                  _const_spec((1, D)), _const_spec((D, LANES)), _const_spec((1, LANES))],
        out_specs=[row(D), row(D), row(LANES)],
        out_shape=[jax.ShapeDtypeStruct((B, T, D), _F32), jax.ShapeDtypeStruct((B, T, D), _MXU),
                   jax.ShapeDtypeStruct((B, T, LANES), _F32)],
        compiler_params=_cparams(("parallel", "parallel")),
        name="post",
    )(x, oa, ob, ga, gb, w_pa.astype(_MXU), w_pb_exp, w_o.astype(_MXU), g_ffn.reshape(1, D), wr, br)


def _moe_kernel(be_ref, nu_ref, x_ref, wgu_ref, bgu_ref, wdn_ref, bdn_ref, y_ref, wgu_sc, wdn_sc):
    i = pl.program_id(0)
    used = i < nu_ref[0]
    new_expert = (i == 0) | (be_ref[i] != be_ref[jnp.maximum(i - 1, 0)])

    @pl.when(used & new_expert)
    def _():
        wgu_sc[...] = wgu_ref[...].astype(wgu_sc.dtype)
        wdn_sc[...] = wdn_ref[...].astype(wdn_sc.dtype)

    @pl.when(used)
    def _():
        gu = jnp.dot(x_ref[...], wgu_sc[...], preferred_element_type=_F32) + bgu_ref[...]
        glu = jnp.minimum(gu[:, :D_EXPERT], SWIGLU_LIMIT)
        lin = jnp.clip(gu[:, D_EXPERT:], -SWIGLU_LIMIT, SWIGLU_LIMIT)
        act = glu * jax.nn.sigmoid(SWIGLU_ALPHA * glu) * (lin + 1.0)
        y_ref[...] = jnp.dot(act.astype(wdn_sc.dtype), wdn_sc[...],
                             preferred_element_type=_F32) + bdn_ref[...]

    @pl.when(jnp.logical_not(used))
    def _():
        y_ref[...] = jnp.zeros(y_ref.shape, y_ref.dtype)


def _moe(h2, route, w_gu, b_gu, w_dn, b_dn):
    T, D = h2.shape
    K = TOP_K_EXPERTS
    A = T * K
    NB = -(-A // MOE_BLOCK) + N_EXPERTS
    NS = NB * MOE_BLOCK
    e_flat = route[:, :K].astype(jnp.int32).reshape(-1)
    experts = jnp.arange(N_EXPERTS, dtype=jnp.int32)
    counts = jnp.sum((e_flat[:, None] == experts[None, :]).astype(jnp.int32), axis=0)
    blocks_per = (counts + MOE_BLOCK - 1) // MOE_BLOCK
    cum_blocks = jnp.cumsum(blocks_per)
    blk = jnp.arange(NB, dtype=jnp.int32)
    block_expert = jnp.minimum(jnp.sum((cum_blocks[None, :] <= blk[:, None]).astype(jnp.int32), axis=1),
                               N_EXPERTS - 1)
    n_used = cum_blocks[-1:].astype(jnp.int32)
    cum_pad = jnp.cumsum(blocks_per * MOE_BLOCK - counts)
    r = jnp.arange(NS - A, dtype=jnp.int32)
    empty_expert = jnp.sum((cum_pad[None, :] <= r[:, None]).astype(jnp.int32), axis=1)
    src_bits = (NS - 1).bit_length()
    assert (2 * N_EXPERTS + 2) << src_bits < 2 ** 31
    group = jnp.concatenate([2 * e_flat, 2 * empty_expert + 1])
    slot_src = lax.sort((group << src_bits) | jnp.arange(NS, dtype=jnp.int32)) & ((1 << src_bits) - 1)
    slot_tok = jnp.where(slot_src < A, slot_src // K, (slot_src - A) % T)
    x_slots = h2[slot_tok]
    y_slots = pl.pallas_call(
        _moe_kernel,
        grid_spec=pltpu.PrefetchScalarGridSpec(
            num_scalar_prefetch=2,
            grid=(NB,),
            in_specs=[pl.BlockSpec((MOE_BLOCK, D), lambda i, be, nu: (i, 0)),
                      pl.BlockSpec((None, D, 2 * D_EXPERT), lambda i, be, nu: (be[i], 0, 0)),
                      pl.BlockSpec((None, 1, 2 * D_EXPERT), lambda i, be, nu: (be[i], 0, 0)),
                      pl.BlockSpec((None, D_EXPERT, D), lambda i, be, nu: (be[i], 0, 0)),
                      pl.BlockSpec((None, 1, D), lambda i, be, nu: (be[i], 0, 0))],
            out_specs=pl.BlockSpec((MOE_BLOCK, D), lambda i, be, nu: (i, 0)),
            scratch_shapes=[pltpu.VMEM((D, 2 * D_EXPERT), _MXU), pltpu.VMEM((D_EXPERT, D), _MXU)],
        ),
        out_shape=jax.ShapeDtypeStruct((NB * MOE_BLOCK, D), _F32),
        compiler_params=_cparams(("arbitrary",)),
        name="moe",
    )(block_expert, n_used, x_slots, w_gu, b_gu.reshape(N_EXPERTS, 1, -1).astype(_F32),
      w_dn, b_dn.reshape(N_EXPERTS, 1, -1).astype(_F32))
    _, dest = lax.sort((slot_src, jnp.arange(NS, dtype=jnp.int32)), num_keys=1)
    dest = dest[:A].reshape(T, K)
    return [y_slots[dest[:, k]] for k in range(K)]


def _final_kernel(x_ref, y0_ref, y1_ref, y2_ref, y3_ref, route_ref, g_ref, o_ref):
    K = TOP_K_EXPERTS
    x = x_ref[...]
    for k, y_ref in enumerate((y0_ref, y1_ref, y2_ref, y3_ref)):
        x = x + route_ref[:, K + k:K + k + 1] * y_ref[...]
    ms = jnp.mean(x * x, axis=-1, keepdims=True)
    o_ref[...] = x * lax.rsqrt(ms + RMS_EPS) * g_ref[...]


def _final(x1, ys, route, g_final, tm):
    T, D = x1.shape
    assert T % tm == 0
    row = pl.BlockSpec((tm, D), lambda i: (i, 0))
    return pl.pallas_call(
        _final_kernel, grid=(T // tm,),
        in_specs=[row] * 5 + [pl.BlockSpec((tm, LANES), lambda i: (i, 0)), _const_spec((1, D))],
        out_specs=row,
        out_shape=jax.ShapeDtypeStruct((T, D), _F32),
        compiler_params=_cparams(("parallel",)),
        name="final",
    )(x1, *ys, route, g_final.reshape(1, D))


def _group(x_rows, proj, keys, tails, qchunk, kchunk, korder, topk, Tq, tiles, weights, lam_init):
    (lam_vecs, subln_w, w_pa, w_pb_exp, w_o, g_ffn, w_router, b_router,
     w_gu, b_gu, w_dn, b_dn, g_final) = weights
    B = x_rows.shape[0]
    oa = _diff_attn(proj["qa"], keys["ka"], keys["va"], tails["ka"], tails["va"], qchunk, kchunk,
                    lam_vecs, subln_w, lam_init, Tq, tiles["tq_a"], tiles["tk"])
    ob = _dsa(proj["qi"], proj["wi"], proj["qb"], keys["kir"], keys["kb"], keys["vb"],
              tails["kir"], tails["kb"], tails["vb"], qchunk, kchunk, korder, topk, Tq,
              tiles["tq_b"], tiles["tk"])
    x1, h2, route = _post(x_rows, oa, ob, proj["ga"], proj["gb"], w_pa, w_pb_exp, w_o, g_ffn,
                          w_router, b_router, Tq, tiles["tm_post"])
    T = B * Tq
    route = route.reshape(T, LANES)
    ys = _moe(h2.reshape(T, D_MODEL), route, w_gu, b_gu, w_dn, b_dn)
    y = _final(x1.reshape(T, D_MODEL), ys, route, g_final, tiles["tm_post"])
    return y.reshape(B, Tq, D_MODEL)


_KEY_OPERANDS = ("ka", "va", "kb", "vb", "kir")


def kernel(x_prompt, x_sample, cache_a_k, cache_a_v, cache_b_k, cache_b_v, cache_idx_k,
           meta_tokens, g_mix, w_in, lambda_q1, lambda_k1, lambda_q2, lambda_k2, subln_w,
           w_pa, w_pb, w_o, g_ffn, w_router, b_router, w_gu, b_gu, w_dn, b_dn, g_final):
    B, S, D = x_prompt.shape
    Bs, S_dec, _ = x_sample.shape
    P = cache_a_k.shape[2]
    depth = g_mix.shape[0]
    assert depth == 1
    l = 0
    lam_init = 0.8 - 0.6 * math.exp(-0.3 * l)
    w_lay = _layout_w_in(w_in[l])
    w_pb_exp = _layout_w_pb(w_pb[l])
    lam_vecs = jnp.stack([lambda_q1[l], lambda_k1[l], lambda_q2[l], lambda_k2[l]]).astype(_F32)
    weights = (lam_vecs, subln_w[l], w_pa[l], w_pb_exp, w_o[l], g_ffn[l], w_router[l], b_router[l],
               w_gu[l], b_gu[l], w_dn[l], b_dn[l], g_final)

    r = np.arange(S + TAIL)
    chunk_p = np.where(r < S, r // CHUNK, np.where(r < S + N_META, -1, _BIG_CHUNK)).astype(np.int64)
    order_p = np.where(r < S, r + N_META, r - S).astype(np.int32)
    tiles_p = dict(tq_a=min(512, S), tq_b=min(128, S), tk=min(512, S), tm_post=min(512, S))
    meta_tile = jnp.concatenate([meta_tokens.astype(x_prompt.dtype),
                                 jnp.zeros((TAIL - N_META, D), x_prompt.dtype)])[None]
    pos_m = np.where(np.arange(TAIL) < N_META, np.arange(TAIL), 0).astype(np.int32)
    pm = _project(meta_tile, jnp.asarray(pos_m), g_mix[l], w_lay, TAIL)
    meta_rows = {n: pm[n][0, :N_META] for n in ("ka_f", "va_f", "kb_f", "vb_f", "kiwi")}
    pp = _project(x_prompt, jnp.asarray(N_META + np.arange(S, dtype=np.int32)), g_mix[l], w_lay,
                  tiles_p["tm_post"], head_rows=meta_rows)
    y_prompt = _group(x_prompt, pp, pp, pm, chunk_p[:S], chunk_p, order_p, min(TOPK_MAX, S // 4), S,
                      tiles_p, weights, lam_init)

    def rows_p(name, shape, width=None):
        return pp[name][:, :, :width].reshape((1, B, N_META + S) + shape)

    pos_s = (P + np.arange(S_dec)).astype(np.int32)
    ps = _project(x_sample, jnp.asarray(pos_s), g_mix[l], w_lay, S_dec)
    r = np.arange(P + TAIL)
    chunk_s = np.where(r < P + S_dec, r // CHUNK, _BIG_CHUNK).astype(np.int64)
    qchunk_s = (pos_s // CHUNK).astype(np.int64)
    cached = dict(ka=cache_a_k[l], va=cache_a_v[l], kb=cache_b_k[l], vb=cache_b_v[l],
                  kir=jnp.tile(cache_idx_k[l], (1, 1, N_IDX_HEADS)))
    keys_s = {n: cached[n].reshape(Bs, P, -1).astype(_MXU) for n in _KEY_OPERANDS}
    tails_s = {n: jnp.concatenate([ps[n], jnp.zeros((Bs, TAIL - S_dec, ps[n].shape[-1]), _MXU)], axis=1)
               for n in _KEY_OPERANDS}
    tiles_s = dict(tq_a=S_dec, tq_b=S_dec, tk=min(512, P), tm_post=S_dec)
    y_sample = _group(x_sample, ps, keys_s, tails_s, qchunk_s, chunk_s, r.astype(np.int32),
                      min(TOPK_MAX, (P + S_dec) // 4), S_dec, tiles_s, weights, lam_init)

    def rows_s(name, shape, width=None):
        return ps[name][:, :, :width].reshape((1, Bs, S_dec) + shape)

    ha, hb = (N_HEADS_A, 2 * HEAD_DIM_A), (N_KV_B, HEAD_DIM_B)
    return (y_prompt, y_sample,
            rows_p("ka_f", ha), rows_p("va_f", ha), rows_p("kb_f", hb), rows_p("vb_f", hb),
            rows_p("kiwi", (IDX_DIM,), IDX_DIM),
            rows_s("ka_f", ha), rows_s("va_f", ha), rows_s("kb_f", hb), rows_s("vb_f", hb),
            rows_s("kiwi", (IDX_DIM,), IDX_DIM))
```

```python
import functools
import math

import numpy as np
import jax
import jax.numpy as jnp
from jax import lax
from jax.experimental import pallas as pl
from jax.experimental.pallas import tpu as pltpu

D_MODEL = 1024
CHUNK = 64
N_META = 16
ROPE_THETA = 500000.0
RMS_EPS = 1e-6
N_HEADS_A = 4
HEAD_DIM_A = 64
N_HEADS_B = 8
N_KV_B = 2
HEAD_DIM_B = 64
N_IDX_HEADS = 8
IDX_DIM = 32
TOPK_MAX = 256
N_EXPERTS = 32
TOP_K_EXPERTS = 4
D_EXPERT = 1024
SWIGLU_LIMIT = 7.0
SWIGLU_ALPHA = 1.702
MOE_BLOCK = 512

QA_W = N_HEADS_A * 2 * HEAD_DIM_A
VA_W = QA_W
QB_W = N_HEADS_B * HEAD_DIM_B
KB_W = N_KV_B * HEAD_DIM_B
QI_W = N_IDX_HEADS * IDX_DIM
SPLITS = (QA_W, QA_W, VA_W, QB_W, KB_W, KB_W, QI_W, IDX_DIM, N_IDX_HEADS, D_MODEL, D_MODEL)

LANES = 128
TAIL = LANES
QB_EXP_W = N_HEADS_B * LANES

_SEC = {}
_off = 0
for _name, _w in (("qa", QA_W), ("ka", QA_W), ("va", VA_W), ("qb", QB_EXP_W), ("kb", KB_W),
                  ("vb", KB_W), ("qi", QI_W), ("kiwi", LANES), ("kir", QI_W),
                  ("ga", D_MODEL), ("gb", D_MODEL)):
    _SEC[_name] = (_off, _off + _w)
    _off += _w
W_LAYOUT = _off

_MXU = jnp.bfloat16
_F32 = jnp.float32
_NEG = -1e30
_BIG_CHUNK = 2 ** 30
_LOG2E = math.log2(math.e)
_HALF = 2 ** 15
_KEY_NEG_INF = int(np.int32(np.uint32(0xFF800000) ^ np.uint32(0x7FFFFFFF)))
_VMEM_LIMIT = 56 * 1024 * 1024


def _cparams(sem):
    return pltpu.CompilerParams(dimension_semantics=sem, vmem_limit_bytes=_VMEM_LIMIT)


def _const_spec(shape):
    nd = len(shape)
    return pl.BlockSpec(shape, lambda *_: (0,) * nd)


def _nt_dot(a, b):
    return lax.dot_general(a, b, (((1,), (1,)), ((), ())), preferred_element_type=_F32)


def _sortable(x):
    b = lax.bitcast_convert_type(x, jnp.int32)
    return b ^ ((b >> 31) & 0x7FFFFFFF)


def _rope_block(z, cos, sin_lo, sin_hi, half):
    return (z * cos + pltpu.roll(z, LANES - half, 1) * sin_lo + pltpu.roll(z, half, 1) * sin_hi)


def _proj_kernel(x_ref, g_ref, w_ref, tab_ref, *refs):
    (qa_o, kaf_o, kab_o, vaf_o, vab_o, qb_o, kbf_o, kbb_o, vbf_o, vbb_o,
     qi_o, kiwi_o, kir_o, ga_o, gb_o, wi_o) = refs[-16:]
    kaf_o, vaf_o, kbf_o, vbf_o, kiwi_o = (o.at[0] if len(o.shape) == 3 else o
                                          for o in (kaf_o, vaf_o, kbf_o, vbf_o, kiwi_o))
    x = x_ref[...]
    ms = jnp.mean(x * x, axis=-1, keepdims=True)
    h = (x * lax.rsqrt(ms + RMS_EPS) * g_ref[...]).astype(w_ref.dtype)

    def sec(name):
        a, b = _SEC[name]
        return jnp.dot(h, w_ref[:, a:b], preferred_element_type=_F32)

    t64 = tuple(tab_ref[:, k * LANES:(k + 1) * LANES] for k in (0, 1, 2))
    t32 = tuple(tab_ref[:, k * LANES:(k + 1) * LANES] for k in (3, 4, 5))
    tki = tuple(tab_ref[:, k * LANES:(k + 1) * LANES] for k in (6, 7, 8))

    def roped(z, tab, half):
        return [_rope_block(z[:, c * LANES:(c + 1) * LANES], *tab, half)
                for c in range(z.shape[1] // LANES)]

    scale_a = HEAD_DIM_A ** -0.5 * _LOG2E
    scale_b = HEAD_DIM_B ** -0.5 * _LOG2E
    for c, blk in enumerate(roped(sec("qa"), t64, 8)):
        qa_o[:, c * LANES:(c + 1) * LANES] = (blk * scale_a).astype(qa_o.dtype)
    for c, blk in enumerate(roped(sec("ka"), t64, 8)):
        kaf_o[:, c * LANES:(c + 1) * LANES] = blk
        kab_o[:, c * LANES:(c + 1) * LANES] = blk.astype(kab_o.dtype)
    va = sec("va")
    vaf_o[...] = va
    vab_o[...] = va.astype(vab_o.dtype)
    for c, blk in enumerate(roped(sec("qb"), t64, 8)):
        qb_o[:, c * LANES:(c + 1) * LANES] = (blk * scale_b).astype(qb_o.dtype)
    kb = roped(sec("kb"), t64, 8)[0]
    kbf_o[...] = kb
    kbb_o[...] = kb.astype(kbb_o.dtype)
    vb = sec("vb")
    vbf_o[...] = vb
    vbb_o[...] = vb.astype(vbb_o.dtype)
    for c, blk in enumerate(roped(sec("qi"), t32, 4)):
        qi_o[:, c * LANES:(c + 1) * LANES] = blk.astype(qi_o.dtype)
    kiwi = roped(sec("kiwi"), tki, 4)[0]
    kiwi_o[...] = kiwi
    wi_o[...] = kiwi
    for c, blk in enumerate(roped(sec("kir"), t32, 4)):
        kir_o[:, c * LANES:(c + 1) * LANES] = blk.astype(kir_o.dtype)
    ga_o[...] = jax.nn.sigmoid(sec("ga"))
    gb_o[...] = jax.nn.sigmoid(sec("gb"))


def _rope_tables(pos):
    def cs(d):
        rd = d // 4
        half = rd // 2
        inv_freq = ROPE_THETA ** (-jnp.arange(half, dtype=_F32) * 2.0 / rd)
        ang = pos.astype(_F32)[:, None] * inv_freq[None, :]
        return jnp.cos(ang), jnp.sin(ang), half

    T = pos.shape[0]

    def pattern(d, width):
        c, s, half = cs(d)
        one = jnp.ones((T, width - 2 * half), _F32)
        zero = lambda n: jnp.zeros((T, n), _F32)
        cos = jnp.concatenate([c, c, one], axis=1)
        lo = jnp.concatenate([-s, zero(width - half)], axis=1)
        hi = jnp.concatenate([zero(half), s, zero(width - 2 * half)], axis=1)
        return cos, lo, hi

    tabs = []
    for d, width in ((64, 64), (32, 32), (32, LANES)):
        tabs += [jnp.tile(t, (1, LANES // width)) for t in pattern(d, width)]
    return jnp.concatenate(tabs, axis=1)


def _layout_w_in(w_in):
    offs = np.cumsum((0,) + SPLITS)
    qa, ka, va, qb, kb, vb, qi, ki, wi, ga, gb = (w_in[:, offs[i]:offs[i + 1]] for i in range(11))
    D = w_in.shape[0]
    z64 = jnp.zeros((D, HEAD_DIM_B), w_in.dtype)
    per_kv = N_HEADS_B // N_KV_B
    qb_exp = []
    for hq in range(N_HEADS_B):
        col = qb[:, hq * HEAD_DIM_B:(hq + 1) * HEAD_DIM_B]
        qb_exp += [col, z64] if hq // per_kv == 0 else [z64, col]
    kiwi = jnp.concatenate([ki, wi, jnp.zeros((D, LANES - IDX_DIM - N_IDX_HEADS), w_in.dtype)], axis=1)
    kir = jnp.tile(ki, (1, N_IDX_HEADS))
    w = jnp.concatenate([qa, ka, va] + qb_exp + [kb, vb, qi, kiwi, kir, ga, gb], axis=1)
    assert w.shape[1] == W_LAYOUT
    return w.astype(_MXU)


def _layout_w_pb(w_pb):
    D = w_pb.shape[1]
    z64 = jnp.zeros((HEAD_DIM_B, D), w_pb.dtype)
    per_kv = N_HEADS_B // N_KV_B
    rows = []
    for hq in range(N_HEADS_B):
        r = w_pb[hq * HEAD_DIM_B:(hq + 1) * HEAD_DIM_B]
        rows += [r, z64] if hq // per_kv == 0 else [z64, r]
    return jnp.concatenate(rows, axis=0).astype(_MXU)


def _project(x, pos, g_mix, w_lay, tm, head_rows=None):
    B, T, D = x.shape
    assert T % tm == 0
    off = next(iter(head_rows.values())).shape[0] if head_rows else 0
    total = off + T
    assert off % 8 == 0
    tab = _rope_tables(pos)
    row = lambda w: pl.BlockSpec((None, tm, w), lambda b, i: (b, i, 0))
    shifted = lambda w: pl.BlockSpec((pl.Element(1), pl.Element(tm), pl.Element(w)),
                                     lambda b, i: (b, pl.multiple_of(off + i * tm, 8), 0))
    outs = [("qa", QA_W, _MXU), ("ka_f", QA_W, _F32), ("ka", QA_W, _MXU), ("va_f", VA_W, _F32),
            ("va", VA_W, _MXU), ("qb", QB_EXP_W, _MXU), ("kb_f", KB_W, _F32), ("kb", KB_W, _MXU),
            ("vb_f", KB_W, _F32), ("vb", KB_W, _MXU), ("qi", QI_W, _MXU), ("kiwi", LANES, _F32),
            ("kir", QI_W, _MXU), ("ga", D_MODEL, _F32), ("gb", D_MODEL, _F32), ("wi", LANES, _F32)]
    is_cache = lambda n: n.endswith("_f") or n == "kiwi"
    n_in = 4
    heads, aliases = [], {}
    for k, (n, w, _) in enumerate(outs):
        if head_rows and is_cache(n):
            rows = jnp.broadcast_to(head_rows[n][None], (B, off, w))
            heads.append(jnp.concatenate([rows, jnp.zeros((B, T, w), _F32)], axis=1))
            aliases[n_in + len(heads) - 1] = k
    res = pl.pallas_call(
        _proj_kernel,
        grid=(B, T // tm),
        in_specs=[row(D), _const_spec((1, D)),
                  pl.BlockSpec((D, W_LAYOUT), lambda b, i: (0, 0), pipeline_mode=pl.Buffered(1)),
                  pl.BlockSpec((tm, 9 * LANES), lambda b, i: (i, 0))]
                 + [pl.BlockSpec(memory_space=pl.ANY)] * len(heads),
        out_specs=[shifted(w) if (is_cache(n) and off) else row(w) for n, w, _ in outs],
        out_shape=[jax.ShapeDtypeStruct((B, total if is_cache(n) else T, w), dt) for n, w, dt in outs],
        input_output_aliases=aliases,
        compiler_params=_cparams(("parallel", "parallel")),
        name="proj",
    )(x, g_mix.reshape(1, D), w_lay, tab, *heads)
    return {n: r for (n, _, _), r in zip(outs, res)}


def _tile_lanes(v, n):
    return v if n == 1 else jnp.concatenate([v] * n, axis=1)


def _softmax_step(s, v, m_sc, l_sc, acc_sc):
    m_prev = m_sc[...]
    m_new = jnp.maximum(m_prev, jnp.max(s, axis=1, keepdims=True))
    alpha = jnp.exp2(m_prev - m_new)
    p = jnp.exp2(s - _tile_lanes(m_new, s.shape[1] // LANES))
    l_sc[...] = alpha * l_sc[...] + jnp.sum(p, axis=1, keepdims=True)
    acc_sc[...] = alpha * acc_sc[...] + jnp.dot(p.astype(v.dtype), v, preferred_element_type=_F32)
    m_sc[...] = m_new


def _attend_tiles(n, scores, values, s0_sc, s1_sc, m_sc, l_sc, acc_sc):
    s0_sc[...] = scores(0)

    def two(j):
        s1_sc[...] = scores(j + 1)
        _softmax_step(s0_sc[...], values(j), m_sc, l_sc, acc_sc)
        s0_sc[...] = scores(jnp.minimum(j + 2, n - 1))
        _softmax_step(s1_sc[...], values(j + 1), m_sc, l_sc, acc_sc)

    def octet(p, carry):
        for t in range(0, 8, 2):
            two(8 * p + t)
        return carry

    lax.fori_loop(0, n // 8, octet, 0)
    done = (n // 8) * 8

    @pl.when(n - done >= 4)
    def _():
        two(done)
        two(done + 2)

    done = (n // 4) * 4

    @pl.when(n - done >= 2)
    def _():
        two(done)

    @pl.when(n % 2 == 1)
    def _():
        _softmax_step(s0_sc[...], values(n - 1), m_sc, l_sc, acc_sc)


def _for_tiles(n, tile_fn):
    def octet(p, carry):
        for t in range(8):
            tile_fn(8 * p + t)
        return carry

    lax.fori_loop(0, n // 8, octet, 0)
    done = (n // 8) * 8

    @pl.when(n - done >= 4)
    def _():
        for t in range(4):
            tile_fn(done + t)

    done = (n // 4) * 4

    @pl.when(n - done >= 2)
    def _():
        tile_fn(done)
        tile_fn(done + 1)

    @pl.when(n % 2 == 1)
    def _():
        tile_fn(n - 1)


def _num_key_tiles(qchunk, kchunk_main, TQ, TK):
    first = kchunk_main.reshape(-1, TK)[:, 0]
    qmax = qchunk.reshape(-1, TQ).max(axis=1)
    return (first[None, :] <= qmax[:, None]).sum(axis=1).astype(np.int32)


def _diff_attn_kernel(nk_ref, q_ref, k_ref, v_ref, kt_ref, vt_ref, qc_ref, kcm_ref, kct_ref,
                      lam_ref, sub_ref, o_ref, s0_sc, s1_sc, m_sc, l_sc, acc_sc, *, TQ, TK, lam_init):
    i = pl.program_id(2)
    q = q_ref[...]
    lane = lax.broadcasted_iota(jnp.int32, q.shape, 1)
    zero = jnp.zeros_like(q)
    qs = jnp.concatenate([jnp.where(lane < HEAD_DIM_A, q, zero),
                          jnp.where(lane >= HEAD_DIM_A, q, zero)], axis=0)
    qc = qc_ref[...]
    qc2 = jnp.concatenate([qc, qc], axis=0)
    m_sc[...] = jnp.full(m_sc.shape, _NEG, _F32)
    l_sc[...] = jnp.zeros(l_sc.shape, _F32)
    acc_sc[...] = jnp.zeros(acc_sc.shape, _F32)

    def keys(j):
        return k_ref[pl.ds(pl.multiple_of(j * TK, TK), TK), :]

    def values(j):
        return v_ref[pl.ds(pl.multiple_of(j * TK, TK), TK), :]

    def masked_scores(k, kc):
        return jnp.where(kc <= qc2, _nt_dot(qs, k), _NEG)

    _attend_tiles(nk_ref[i], lambda j: masked_scores(keys(j), kcm_ref[j]), values,
                  s0_sc, s1_sc, m_sc, l_sc, acc_sc)
    _softmax_step(masked_scores(kt_ref[...], kct_ref[...]), vt_ref[...], m_sc, l_sc, acc_sc)

    lam = (jnp.exp(jnp.sum(lam_ref[0:1, :] * lam_ref[1:2, :], axis=1, keepdims=True))
           - jnp.exp(jnp.sum(lam_ref[2:3, :] * lam_ref[3:4, :], axis=1, keepdims=True))
           + lam_init)
    o = acc_sc[...] / l_sc[...]
    d = o[:TQ] - lam * o[TQ:]
    ms = jnp.mean(d * d, axis=-1, keepdims=True)
    o_ref[...] = ((d * lax.rsqrt(ms + RMS_EPS) * sub_ref[...]) * (1.0 - lam_init)).astype(o_ref.dtype)


def _tail_spec(tail, width, index):
    per_batch = tail.shape[0] > 1
    return pl.BlockSpec((None, TAIL, width), lambda b, *rest: (b if per_batch else 0, 0, index(*rest)))


def _diff_attn(q, k, v, kt, vt, qchunk, kchunk, lam_vecs, subln_w, lam_init, Tq, TQ, TK):
    B = q.shape[0]
    Lmain = k.shape[1]
    L = Lmain
    assert Tq % TQ == 0 and Lmain % TK == 0 and kchunk.shape[0] == Lmain + TAIL
    nq, nt = Tq // TQ, Lmain // TK
    nk = jnp.asarray(_num_key_tiles(qchunk[:Tq], kchunk[:Lmain], TQ, TK))
    qc = jnp.asarray(qchunk[:Tq].reshape(Tq, 1).astype(np.int32))
    kcm = jnp.asarray(kchunk[:Lmain].reshape(nt, 1, TK).astype(np.int32))
    kct = jnp.asarray(kchunk[Lmain:].reshape(1, TAIL).astype(np.int32))
    R = 2 * TQ
    kern = functools.partial(_diff_attn_kernel, TQ=TQ, TK=TK, lam_init=lam_init)
    head = lambda h, i, *_: h
    grid_spec = pltpu.PrefetchScalarGridSpec(
        num_scalar_prefetch=1,
        grid=(B, N_HEADS_A, nq),
        in_specs=[pl.BlockSpec((None, TQ, LANES), lambda b, h, i, *_: (b, i, h)),
                  pl.BlockSpec((None, L, LANES), lambda b, h, i, *_: (b, 0, h)),
                  pl.BlockSpec((None, L, LANES), lambda b, h, i, *_: (b, 0, h)),
                  _tail_spec(kt, LANES, head), _tail_spec(vt, LANES, head),
                  pl.BlockSpec((TQ, 1), lambda b, h, i, *_: (i, 0)),
                  pl.BlockSpec((nt, 1, TK), lambda b, h, i, *_: (0, 0, 0)),
                  pl.BlockSpec((1, TAIL), lambda b, h, i, *_: (0, 0)),
                  pl.BlockSpec((4, HEAD_DIM_A), lambda b, h, i, *_: (0, 0)),
                  pl.BlockSpec((1, LANES), lambda b, h, i, *_: (0, 0))],
        out_specs=pl.BlockSpec((None, TQ, LANES), lambda b, h, i, *_: (b, i, h)),
        scratch_shapes=[pltpu.VMEM((R, TK), _F32)] * 2 + [pltpu.VMEM((R, LANES), _F32)] * 3,
    )
    return pl.pallas_call(
        kern, grid_spec=grid_spec,
        out_shape=jax.ShapeDtypeStruct((B, Tq, VA_W), _MXU),
        compiler_params=_cparams(("parallel", "parallel", "parallel")),
        name="diff_attn",
    )(nk, q, k, v, kt, vt, qc, kcm, kct, lam_vecs, subln_w.reshape(1, LANES))


def _dsa_kernel(nk_ref, qi_ref, kiwi_ref, qb_ref, kir_ref, kb_ref, vb_ref, kirt_ref, kbt_ref, vbt_ref,
                qc_ref, kcm_ref, kct_ref, o_ref,
                key_sc, keyt_sc, hi_sc, lo_sc, hit_sc, lot_sc, s0_sc, s1_sc, m_sc, l_sc, acc_sc,
                *, TQ, TK, topk, tail_first):
    i = pl.program_id(1)
    nk = nk_ref[i]
    H = N_IDX_HEADS
    nblk = TK // LANES

    qi = qi_ref[...]
    head = lax.broadcasted_iota(jnp.int32, qi.shape, 1) // IDX_DIM
    zero = jnp.zeros_like(qi)
    qis = jnp.concatenate([jnp.where(head == h, qi, zero) for h in range(H)], axis=0)
    wi = kiwi_ref[:, IDX_DIM:IDX_DIM + H] * (H ** -0.5) * (IDX_DIM ** -0.5)
    wi_b = [jnp.broadcast_to(wi[:, h:h + 1], (TQ, LANES)) for h in range(H)]
    qc = qc_ref[...]

    def score_keys(lg, kc):
        n = lg.shape[1] // LANES
        sc = _tile_lanes(wi_b[0], n) * jnp.maximum(lg[0:TQ], 0.0)
        for h in range(1, H):
            sc = sc + _tile_lanes(wi_b[h], n) * jnp.maximum(lg[h * TQ:(h + 1) * TQ], 0.0)
        return _sortable(jnp.where(kc <= qc, sc, -jnp.inf))

    def split16(key):
        return (key >> 16).astype(jnp.int16), ((key & 0xFFFF) - _HALF).astype(jnp.int16)

    def fill(j):
        lg = _nt_dot(qis, kir_ref[pl.ds(pl.multiple_of(j * TK, TK), TK), :])
        key = score_keys(lg, kcm_ref[j])
        key_sc[j] = key
        hi_sc[j], lo_sc[j] = split16(key)

    _for_tiles(nk, fill)
    key_t = score_keys(_nt_dot(qis, kirt_ref[...]), kct_ref[...])
    keyt_sc[...] = key_t
    hit_sc[...], lot_sc[...] = split16(key_t)

    def count(ind_main, ind_tail, dtype):
        def body(j, c):
            ind = ind_main(j)
            for b in range(nblk):
                c = c + ind[:, b * LANES:(b + 1) * LANES]
            return c
        c = lax.fori_loop(0, nk, body, jnp.zeros((TQ, LANES), dtype)) + ind_tail()
        return jnp.sum(c.astype(_F32), axis=1, keepdims=True)

    one16, zero16 = jnp.int16(1), jnp.int16(0)

    def lanes16(v):
        return jnp.broadcast_to(v, (TQ, LANES)).astype(jnp.int16)

    def count16(main_sc, tail_sc, t16, strict=False):
        tm = _tile_lanes(t16, nblk)
        cmp = (lambda a, b: a > b) if strict else (lambda a, b: a >= b)
        return count(lambda j: jnp.where(cmp(main_sc[j], tm), one16, zero16),
                     lambda: jnp.where(cmp(tail_sc[...], t16), one16, zero16), jnp.int16)

    def kth_half(main_sc, tail_sc, target):
        def step(it, t_u):
            cand_u = t_u | jnp.left_shift(jnp.int32(1), 15 - it)
            cnt = count16(main_sc, tail_sc, lanes16(cand_u - _HALF))
            return jnp.where(cnt >= target, cand_u, t_u)
        return lax.fori_loop(0, 16, step, jnp.zeros((TQ, 1), jnp.int32))

    kf = float(topk)
    t_hi = kth_half(hi_sc, hit_sc, kf) - _HALF
    t_hi16 = lanes16(t_hi)
    n_above = count16(hi_sc, hit_sc, t_hi16, strict=True)
    t_hi16m = _tile_lanes(t_hi16, nblk)
    low16 = jnp.int16(-_HALF)

    def bucket(j, carry):
        lo_sc[j] = jnp.where(hi_sc[j] == t_hi16m, lo_sc[j], low16)
        return carry

    lax.fori_loop(0, nk, bucket, 0)
    lot_sc[...] = jnp.where(hit_sc[...] == t_hi16, lot_sc[...], low16)
    t_lo = kth_half(lo_sc, lot_sc, kf - n_above)
    n_ge = n_above + count16(lo_sc, lot_sc, lanes16(t_lo - _HALF))
    kth = (t_hi << 16) | t_lo
    thr1 = jnp.maximum(kth, _KEY_NEG_INF + 1)
    thr = jnp.broadcast_to(thr1, (TQ, LANES))
    thr_m = _tile_lanes(thr, nblk)
    excess = jnp.broadcast_to((n_ge > kf) & (kth > _KEY_NEG_INF), (TQ, LANES))

    @pl.when(jnp.max(jnp.where(excess, 1.0, 0.0)) > 0.0)
    def _():
        quota = kf - count(lambda j: jnp.where(key_sc[j] > thr_m, 1.0, 0.0),
                           lambda: jnp.where(keyt_sc[...] > thr, 1.0, 0.0), _F32)
        r = lax.broadcasted_iota(jnp.int32, (LANES, LANES), 0)
        c = lax.broadcasted_iota(jnp.int32, (LANES, LANES), 1)
        upto = jnp.where(r <= c, 1.0, 0.0).astype(_MXU)
        ones = jnp.ones((LANES, LANES), _MXU)

        def keep_first(keys, seen):
            tie = keys == thr
            e = jnp.where(tie, 1.0, 0.0).astype(_MXU)
            rank = seen + jnp.dot(e, upto, preferred_element_type=_F32)
            keys = jnp.where(tie & (rank > quota), _KEY_NEG_INF, keys)
            return keys, seen + jnp.dot(e, ones, preferred_element_type=_F32)

        def tail_block(seen):
            keyt_sc[...], seen = keep_first(keyt_sc[...], seen)
            return seen

        def main_tile(j, seen):
            kj = key_sc[j]
            blocks = []
            for b in range(nblk):
                blk, seen = keep_first(kj[:, b * LANES:(b + 1) * LANES], seen)
                blocks.append(blk)
            key_sc[j] = jnp.concatenate(blocks, axis=1)
            return seen

        seen = jnp.zeros((TQ, LANES), _F32)
        if tail_first:
            seen = tail_block(seen)
        seen = lax.fori_loop(0, nk, main_tile, seen)
        if not tail_first:
            tail_block(seen)

    qs = jnp.concatenate([qb_ref[:, h * LANES:(h + 1) * LANES] for h in range(N_HEADS_B)], axis=0)
    m_sc[...] = jnp.full(m_sc.shape, _NEG, _F32)
    l_sc[...] = jnp.zeros(l_sc.shape, _F32)
    acc_sc[...] = jnp.zeros(acc_sc.shape, _F32)

    def masked_scores(keys, kb):
        bias = jnp.where(keys >= _tile_lanes(thr, keys.shape[1] // LANES), 0.0, _NEG)
        return _nt_dot(qs, kb) + jnp.concatenate([bias] * N_HEADS_B, axis=0)

    def scores(j):
        return masked_scores(key_sc[j], kb_ref[pl.ds(pl.multiple_of(j * TK, TK), TK), :])

    def values(j):
        return vb_ref[pl.ds(pl.multiple_of(j * TK, TK), TK), :]

    _attend_tiles(nk, scores, values, s0_sc, s1_sc, m_sc, l_sc, acc_sc)
    _softmax_step(masked_scores(keyt_sc[...], kbt_ref[...]), vbt_ref[...], m_sc, l_sc, acc_sc)
    o = acc_sc[...] / l_sc[...]
    for h in range(N_HEADS_B):
        o_ref[:, h * LANES:(h + 1) * LANES] = o[h * TQ:(h + 1) * TQ].astype(o_ref.dtype)


def _dsa(qi, kiwi, qb, kir, kb, vb, kirt, kbt, vbt, qchunk, kchunk, korder, topk, Tq, TQ, TK):
    B = qi.shape[0]
    L = Lmain = kir.shape[1]
    assert Tq % TQ == 0 and Lmain % TK == 0 and kchunk.shape[0] == Lmain + TAIL
    real = kchunk < _BIG_CHUNK
    order_main, order_tail = korder[:Lmain], korder[Lmain:][real[Lmain:]]
    assert np.all(np.diff(order_main) > 0) and np.all(np.diff(order_tail) > 0)
    tail_first = bool(order_tail[-1] < order_main[0])
    assert tail_first or order_tail[0] > order_main[-1]
    nq, nt = Tq // TQ, Lmain // TK
    nk = jnp.asarray(_num_key_tiles(qchunk[:Tq], kchunk[:Lmain], TQ, TK))
    qc = jnp.asarray(qchunk[:Tq].reshape(Tq, 1).astype(np.int32))
    main = lambda a: jnp.asarray(a[:Lmain].reshape(nt, 1, TK).astype(np.int32))
    tail = lambda a: jnp.asarray(a[Lmain:].reshape(1, TAIL).astype(np.int32))
    R = N_HEADS_B * TQ
    kern = functools.partial(_dsa_kernel, TQ=TQ, TK=TK, topk=topk, tail_first=tail_first)
    first = lambda *_: 0
    qrow = lambda w: pl.BlockSpec((None, TQ, w), lambda b, i, nk: (b, i, 0))
    krow = lambda w: pl.BlockSpec((None, L, w), lambda b, i, nk: (b, 0, 0),
                                  pipeline_mode=pl.Buffered(1))
    cm = pl.BlockSpec((nt, 1, TK), lambda b, i, nk: (0, 0, 0))
    ct = pl.BlockSpec((1, TAIL), lambda b, i, nk: (0, 0))
    grid_spec = pltpu.PrefetchScalarGridSpec(
        num_scalar_prefetch=1,
        grid=(B, nq),
        in_specs=[qrow(QI_W), qrow(LANES), qrow(QB_EXP_W), krow(QI_W), krow(KB_W), krow(KB_W),
                  _tail_spec(kirt, QI_W, first), _tail_spec(kbt, KB_W, first), _tail_spec(vbt, KB_W, first),
                  pl.BlockSpec((TQ, 1), lambda b, i, nk: (i, 0)), cm, ct],
        out_specs=qrow(QB_EXP_W),
        scratch_shapes=[pltpu.VMEM((nt, TQ, TK), jnp.int32), pltpu.VMEM((TQ, TAIL), jnp.int32),
                        pltpu.VMEM((nt, TQ, TK), jnp.int16), pltpu.VMEM((nt, TQ, TK), jnp.int16),
                        pltpu.VMEM((TQ, TAIL), jnp.int16), pltpu.VMEM((TQ, TAIL), jnp.int16),
                        pltpu.VMEM((R, TK), _F32), pltpu.VMEM((R, TK), _F32),
                        pltpu.VMEM((R, LANES), _F32), pltpu.VMEM((R, LANES), _F32),
                        pltpu.VMEM((R, LANES), _F32)],
    )
    return pl.pallas_call(
        kern, grid_spec=grid_spec,
        out_shape=jax.ShapeDtypeStruct((B, Tq, QB_EXP_W), _MXU),
        compiler_params=_cparams(("parallel", "parallel")),
        name="dsa",
    )(nk, qi, kiwi, qb, kir, kb, vb, kirt, kbt, vbt, qc, main(kchunk), tail(kchunk))


def _post_kernel(x_ref, oa_ref, ob_ref, ga_ref, gb_ref, wpa_ref, wpb_ref, wo_ref, g_ref,
                 wr_ref, br_ref, x1_o, h2_o, route_o):
    pa = jnp.dot(oa_ref[...], wpa_ref[...], preferred_element_type=_F32)
    pb = jnp.dot(ob_ref[...], wpb_ref[...], preferred_element_type=_F32)
    merged = ga_ref[...] * pa + gb_ref[...] * pb
    x1 = x_ref[...] + jnp.dot(merged.astype(wo_ref.dtype), wo_ref[...], preferred_element_type=_F32)
    x1_o[...] = x1
    ms = jnp.mean(x1 * x1, axis=-1, keepdims=True)
    h2 = (x1 * lax.rsqrt(ms + RMS_EPS) * g_ref[...]).astype(h2_o.dtype)
    h2_o[...] = h2
    cur = jnp.dot(h2, wr_ref[...], preferred_element_type=_F32) + br_ref[...]
    lane = lax.broadcasted_iota(jnp.int32, cur.shape, 1).astype(_F32)
    vals, idxs = [], []
    for _ in range(TOP_K_EXPERTS):
        m = jnp.max(cur, axis=1, keepdims=True)
        idx = jnp.min(jnp.where(cur == m, lane, float(LANES)), axis=1, keepdims=True)
        vals.append(m)
        idxs.append(idx)
        cur = jnp.where(lane == idx, -jnp.inf, cur)
    es = [jnp.exp(v - vals[0]) for v in vals]
    denom = es[0] + es[1] + es[2] + es[3]
    route = jnp.zeros(cur.shape, _F32)
    for k in range(TOP_K_EXPERTS):
        route = jnp.where(lane == float(k), idxs[k], route)
        route = jnp.where(lane == float(TOP_K_EXPERTS + k), es[k] / denom, route)
    route_o[...] = route


def _post(x, oa, ob, ga, gb, w_pa, w_pb_exp, w_o, g_ffn, w_router, b_router, T, tm):
    B, _, D = x.shape
    assert T % tm == 0
    row = lambda w: pl.BlockSpec((None, tm, w), lambda b, i: (b, i, 0))
    wr = jnp.concatenate([w_router, jnp.zeros((D, LANES - N_EXPERTS), w_router.dtype)], axis=1).astype(_MXU)
    br = jnp.concatenate([b_router.astype(_F32), jnp.full((LANES - N_EXPERTS,), _NEG, _F32)]).reshape(1, LANES)
    return pl.pallas_call(
        _post_kernel,
        grid=(B, T // tm),
        in_specs=[row(D), row(VA_W), row(QB_EXP_W), row(D), row(D),
                  _const_spec((VA_W, D)), _const_spec((QB_EXP_W, D)), _const_spec((D, D)),
                  _const_spec((1, D)), _const_spec((D, LANES)), _const_spec((1, LANES))],
        out_specs=[row(D), row(D), row(LANES)],
        out_shape=[jax.ShapeDtypeStruct((B, T, D), _F32), jax.ShapeDtypeStruct((B, T, D), _MXU),
                   jax.ShapeDtypeStruct((B, T, LANES), _F32)],
        compiler_params=_cparams(("parallel", "parallel")),
        name="post",
    )(x, oa, ob, ga, gb, w_pa.astype(_MXU), w_pb_exp, w_o.astype(_MXU), g_ffn.reshape(1, D), wr, br)


def _moe_kernel(be_ref, nu_ref, x_ref, wgu_ref, bgu_ref, wdn_ref, bdn_ref, y_ref, wgu_sc, wdn_sc):
    i = pl.program_id(0)
    used = i < nu_ref[0]
    new_expert = (i == 0) | (be_ref[i] != be_ref[jnp.maximum(i - 1, 0)])

    @pl.when(used & new_expert)
    def _():
        wgu_sc[...] = wgu_ref[...].astype(wgu_sc.dtype)
        wdn_sc[...] = wdn_ref[...].astype(wdn_sc.dtype)

    @pl.when(used)
    def _():
        gu = jnp.dot(x_ref[...], wgu_sc[...], preferred_element_type=_F32) + bgu_ref[...]
        glu = jnp.minimum(gu[:, :D_EXPERT], SWIGLU_LIMIT)
        lin = jnp.clip(gu[:, D_EXPERT:], -SWIGLU_LIMIT, SWIGLU_LIMIT)
        act = glu * jax.nn.sigmoid(SWIGLU_ALPHA * glu) * (lin + 1.0)
        y_ref[...] = jnp.dot(act.astype(wdn_sc.dtype), wdn_sc[...],
                             preferred_element_type=_F32) + bdn_ref[...]

    @pl.when(jnp.logical_not(used))
    def _():
        y_ref[...] = jnp.zeros(y_ref.shape, y_ref.dtype)


def _moe(h2, route, w_gu, b_gu, w_dn, b_dn):
    T, D = h2.shape
    K = TOP_K_EXPERTS
    A = T * K
    NB = -(-A // MOE_BLOCK) + N_EXPERTS
    NS = NB * MOE_BLOCK
    e_flat = route[:, :K].astype(jnp.int32).reshape(-1)
    experts = jnp.arange(N_EXPERTS, dtype=jnp.int32)
    counts = jnp.sum((e_flat[:, None] == experts[None, :]).astype(jnp.int32), axis=0)
    blocks_per = (counts + MOE_BLOCK - 1) // MOE_BLOCK
    cum_blocks = jnp.cumsum(blocks_per)
    blk = jnp.arange(NB, dtype=jnp.int32)
    block_expert = jnp.minimum(jnp.sum((cum_blocks[None, :] <= blk[:, None]).astype(jnp.int32), axis=1),
                               N_EXPERTS - 1)
    n_used = cum_blocks[-1:].astype(jnp.int32)
    cum_pad = jnp.cumsum(blocks_per * MOE_BLOCK - counts)
    r = jnp.arange(NS - A, dtype=jnp.int32)
    empty_expert = jnp.sum((cum_pad[None, :] <= r[:, None]).astype(jnp.int32), axis=1)
    src_bits = (NS - 1).bit_length()
    assert (2 * N_EXPERTS + 2) << src_bits < 2 ** 31
    group = jnp.concatenate([2 * e_flat, 2 * empty_expert + 1])
    slot_src = lax.sort((group << src_bits) | jnp.arange(NS, dtype=jnp.int32)) & ((1 << src_bits) - 1)
    slot_tok = jnp.where(slot_src < A, slot_src // K, (slot_src - A) % T)
    x_slots = h2[slot_tok]
    y_slots = pl.pallas_call(
        _moe_kernel,
        grid_spec=pltpu.PrefetchScalarGridSpec(
            num_scalar_prefetch=2,
            grid=(NB,),
            in_specs=[pl.BlockSpec((MOE_BLOCK, D), lambda i, be, nu: (i, 0)),
                      pl.BlockSpec((None, D, 2 * D_EXPERT), lambda i, be, nu: (be[i], 0, 0)),
                      pl.BlockSpec((None, 1, 2 * D_EXPERT), lambda i, be, nu: (be[i], 0, 0)),
                      pl.BlockSpec((None, D_EXPERT, D), lambda i, be, nu: (be[i], 0, 0)),
                      pl.BlockSpec((None, 1, D), lambda i, be, nu: (be[i], 0, 0))],
            out_specs=pl.BlockSpec((MOE_BLOCK, D), lambda i, be, nu: (i, 0)),
            scratch_shapes=[pltpu.VMEM((D, 2 * D_EXPERT), _MXU), pltpu.VMEM((D_EXPERT, D), _MXU)],
        ),
        out_shape=jax.ShapeDtypeStruct((NB * MOE_BLOCK, D), _F32),
        compiler_params=_cparams(("arbitrary",)),
        name="moe",
    )(block_expert, n_used, x_slots, w_gu, b_gu.reshape(N_EXPERTS, 1, -1).astype(_F32),
      w_dn, b_dn.reshape(N_EXPERTS, 1, -1).astype(_F32))
    _, dest = lax.sort((slot_src, jnp.arange(NS, dtype=jnp.int32)), num_keys=1)
    dest = dest[:A].reshape(T, K)
    return [y_slots[dest[:, k]] for k in range(K)]


def _final_kernel(x_ref, y0_ref, y1_ref, y2_ref, y3_ref, route_ref, g_ref, o_ref):
    K = TOP_K_EXPERTS
    x = x_ref[...]
    for k, y_ref in enumerate((y0_ref, y1_ref, y2_ref, y3_ref)):
        x = x + route_ref[:, K + k:K + k + 1] * y_ref[...]
    ms = jnp.mean(x * x, axis=-1, keepdims=True)
    o_ref[...] = x * lax.rsqrt(ms + RMS_EPS) * g_ref[...]


def _final(x1, ys, route, g_final, tm):
    T, D = x1.shape
    assert T % tm == 0
    row = pl.BlockSpec((tm, D), lambda i: (i, 0))
    return pl.pallas_call(
        _final_kernel, grid=(T // tm,),
        in_specs=[row] * 5 + [pl.BlockSpec((tm, LANES), lambda i: (i, 0)), _const_spec((1, D))],
        out_specs=row,
        out_shape=jax.ShapeDtypeStruct((T, D), _F32),
        compiler_params=_cparams(("parallel",)),
        name="final",
    )(x1, *ys, route, g_final.reshape(1, D))


def _group(x_rows, proj, keys, tails, qchunk, kchunk, korder, topk, Tq, tiles, weights, lam_init):
    (lam_vecs, subln_w, w_pa, w_pb_exp, w_o, g_ffn, w_router, b_router,
     w_gu, b_gu, w_dn, b_dn, g_final) = weights
    B = x_rows.shape[0]
    oa = _diff_attn(proj["qa"], keys["ka"], keys["va"], tails["ka"], tails["va"], qchunk, kchunk,
                    lam_vecs, subln_w, lam_init, Tq, tiles["tq_a"], tiles["tk"])
    ob = _dsa(proj["qi"], proj["wi"], proj["qb"], keys["kir"], keys["kb"], keys["vb"],
              tails["kir"], tails["kb"], tails["vb"], qchunk, kchunk, korder, topk, Tq,
              tiles["tq_b"], tiles["tk"])
    x1, h2, route = _post(x_rows, oa, ob, proj["ga"], proj["gb"], w_pa, w_pb_exp, w_o, g_ffn,
                          w_router, b_router, Tq, tiles["tm_post"])
    T = B * Tq
    route = route.reshape(T, LANES)
    ys = _moe(h2.reshape(T, D_MODEL), route, w_gu, b_gu, w_dn, b_dn)
    y = _final(x1.reshape(T, D_MODEL), ys, route, g_final, tiles["tm_post"])
    return y.reshape(B, Tq, D_MODEL)


_KEY_OPERANDS = ("ka", "va", "kb", "vb", "kir")


def kernel(x_prompt, x_sample, cache_a_k, cache_a_v, cache_b_k, cache_b_v, cache_idx_k,
           meta_tokens, g_mix, w_in, lambda_q1, lambda_k1, lambda_q2, lambda_k2, subln_w,
           w_pa, w_pb, w_o, g_ffn, w_router, b_router, w_gu, b_gu, w_dn, b_dn, g_final):
    B, S, D = x_prompt.shape
    Bs, S_dec, _ = x_sample.shape
    P = cache_a_k.shape[2]
    depth = g_mix.shape[0]
    assert depth == 1
    l = 0
    lam_init = 0.8 - 0.6 * math.exp(-0.3 * l)
    w_lay = _layout_w_in(w_in[l])
    w_pb_exp = _layout_w_pb(w_pb[l])
    lam_vecs = jnp.stack([lambda_q1[l], lambda_k1[l], lambda_q2[l], lambda_k2[l]]).astype(_F32)
    weights = (lam_vecs, subln_w[l], w_pa[l], w_pb_exp, w_o[l], g_ffn[l], w_router[l], b_router[l],
               w_gu[l], b_gu[l], w_dn[l], b_dn[l], g_final)

    r = np.arange(S + TAIL)
    chunk_p = np.where(r < S, r // CHUNK, np.where(r < S + N_META, -1, _BIG_CHUNK)).astype(np.int64)
    order_p = np.where(r < S, r + N_META, r - S).astype(np.int32)
    tiles_p = dict(tq_a=min(512, S), tq_b=min(128, S), tk=min(512, S), tm_post=min(512, S))
    meta_tile = jnp.concatenate([meta_tokens.astype(x_prompt.dtype),
                                 jnp.zeros((TAIL - N_META, D), x_prompt.dtype)])[None]
    pos_m = np.where(np.arange(TAIL) < N_META, np.arange(TAIL), 0).astype(np.int32)
    pm = _project(meta_tile, jnp.asarray(pos_m), g_mix[l], w_lay, TAIL)
    meta_rows = {n: pm[n][0, :N_META] for n in ("ka_f", "va_f", "kb_f", "vb_f", "kiwi")}
    pp = _project(x_prompt, jnp.asarray(N_META + np.arange(S, dtype=np.int32)), g_mix[l], w_lay,
                  tiles_p["tm_post"], head_rows=meta_rows)
    y_prompt = _group(x_prompt, pp, pp, pm, chunk_p[:S], chunk_p, order_p, min(TOPK_MAX, S // 4), S,
                      tiles_p, weights, lam_init)

    def rows_p(name, shape, width=None):
        return pp[name][:, :, :width].reshape((1, B, N_META + S) + shape)

    pos_s = (P + np.arange(S_dec)).astype(np.int32)
    ps = _project(x_sample, jnp.asarray(pos_s), g_mix[l], w_lay, S_dec)
    r = np.arange(P + TAIL)
    chunk_s = np.where(r < P + S_dec, r // CHUNK, _BIG_CHUNK).astype(np.int64)
    qchunk_s = (pos_s // CHUNK).astype(np.int64)
    cached = dict(ka=cache_a_k[l], va=cache_a_v[l], kb=cache_b_k[l], vb=cache_b_v[l],
                  kir=jnp.tile(cache_idx_k[l], (1, 1, N_IDX_HEADS)))
    keys_s = {n: cached[n].reshape(Bs, P, -1).astype(_MXU) for n in _KEY_OPERANDS}
    tails_s = {n: jnp.concatenate([ps[n], jnp.zeros((Bs, TAIL - S_dec, ps[n].shape[-1]), _MXU)], axis=1)
               for n in _KEY_OPERANDS}
    tiles_s = dict(tq_a=S_dec, tq_b=S_dec, tk=min(512, P), tm_post=S_dec)
    y_sample = _group(x_sample, ps, keys_s, tails_s, qchunk_s, chunk_s, r.astype(np.int32),
                      min(TOPK_MAX, (P + S_dec) // 4), S_dec, tiles_s, weights, lam_init)

    def rows_s(name, shape, width=None):
        return ps[name][:, :, :width].reshape((1, Bs, S_dec) + shape)

    ha, hb = (N_HEADS_A, 2 * HEAD_DIM_A), (N_KV_B, HEAD_DIM_B)
    return (y_prompt, y_sample,
            rows_p("ka_f", ha), rows_p("va_f", ha), rows_p("kb_f", hb), rows_p("vb_f", hb),
            rows_p("kiwi", (IDX_DIM,), IDX_DIM),
            rows_s("ka_f", ha), rows_s("va_f", ha), rows_s("kb_f", hb), rows_s("vb_f", hb),
            rows_s("kiwi", (IDX_DIM,), IDX_DIM))
```

```python
import functools
import math

import numpy as np
import jax
import jax.numpy as jnp
from jax import lax
from jax.experimental import pallas as pl
from jax.experimental.pallas import tpu as pltpu

D_MODEL = 1024
CHUNK = 64
N_META = 16
ROPE_THETA = 500000.0
RMS_EPS = 1e-6
N_HEADS_A = 4
HEAD_DIM_A = 64
N_HEADS_B = 8
N_KV_B = 2
HEAD_DIM_B = 64
N_IDX_HEADS = 8
IDX_DIM = 32
TOPK_MAX = 256
N_EXPERTS = 32
TOP_K_EXPERTS = 4
D_EXPERT = 1024
SWIGLU_LIMIT = 7.0
SWIGLU_ALPHA = 1.702
MOE_BLOCK = 512

QA_W = N_HEADS_A * 2 * HEAD_DIM_A
VA_W = QA_W
QB_W = N_HEADS_B * HEAD_DIM_B
KB_W = N_KV_B * HEAD_DIM_B
QI_W = N_IDX_HEADS * IDX_DIM
SPLITS = (QA_W, QA_W, VA_W, QB_W, KB_W, KB_W, QI_W, IDX_DIM, N_IDX_HEADS, D_MODEL, D_MODEL)

LANES = 128
TAIL = LANES
QB_EXP_W = N_HEADS_B * LANES

_SEC = {}
_off = 0
for _name, _w in (("qa", QA_W), ("ka", QA_W), ("va", VA_W), ("qb", QB_EXP_W), ("kb", KB_W),
                  ("vb", KB_W), ("qi", QI_W), ("kiwi", LANES), ("kir", QI_W),
                  ("ga", D_MODEL), ("gb", D_MODEL)):
    _SEC[_name] = (_off, _off + _w)
    _off += _w
W_LAYOUT = _off

_MXU = jnp.bfloat16
_F32 = jnp.float32
_NEG = -1e30
_BIG_CHUNK = 2 ** 30
_LOG2E = math.log2(math.e)
_HALF = 2 ** 15
_KEY_NEG_INF = int(np.int32(np.uint32(0xFF800000) ^ np.uint32(0x7FFFFFFF)))
_VMEM_LIMIT = 56 * 1024 * 1024


def _cparams(sem):
    return pltpu.CompilerParams(dimension_semantics=sem, vmem_limit_bytes=_VMEM_LIMIT)


def _const_spec(shape):
    nd = len(shape)
    return pl.BlockSpec(shape, lambda *_: (0,) * nd)


def _nt_dot(a, b):
    return lax.dot_general(a, b, (((1,), (1,)), ((), ())), preferred_element_type=_F32)


def _sortable(x):
    b = lax.bitcast_convert_type(x, jnp.int32)
    return b ^ ((b >> 31) & 0x7FFFFFFF)


def _rope_block(z, cos, sin_lo, sin_hi, half):
    return (z * cos + pltpu.roll(z, LANES - half, 1) * sin_lo + pltpu.roll(z, half, 1) * sin_hi)


def _proj_kernel(x_ref, g_ref, w_ref, tab_ref, *refs):
    (qa_o, kaf_o, kab_o, vaf_o, vab_o, qb_o, kbf_o, kbb_o, vbf_o, vbb_o,
     qi_o, kiwi_o, kir_o, ga_o, gb_o, wi_o) = refs[-16:]
    kaf_o, vaf_o, kbf_o, vbf_o, kiwi_o = (o.at[0] if len(o.shape) == 3 else o
                                          for o in (kaf_o, vaf_o, kbf_o, vbf_o, kiwi_o))
    x = x_ref[...]
    ms = jnp.mean(x * x, axis=-1, keepdims=True)
    h = (x * lax.rsqrt(ms + RMS_EPS) * g_ref[...]).astype(w_ref.dtype)

    def sec(name):
        a, b = _SEC[name]
        return jnp.dot(h, w_ref[:, a:b], preferred_element_type=_F32)

    t64 = tuple(tab_ref[:, k * LANES:(k + 1) * LANES] for k in (0, 1, 2))
    t32 = tuple(tab_ref[:, k * LANES:(k + 1) * LANES] for k in (3, 4, 5))
    tki = tuple(tab_ref[:, k * LANES:(k + 1) * LANES] for k in (6, 7, 8))

    def roped(z, tab, half):
        return [_rope_block(z[:, c * LANES:(c + 1) * LANES], *tab, half)
                for c in range(z.shape[1] // LANES)]

    scale_a = HEAD_DIM_A ** -0.5 * _LOG2E
    scale_b = HEAD_DIM_B ** -0.5 * _LOG2E
    for c, blk in enumerate(roped(sec("qa"), t64, 8)):
        qa_o[:, c * LANES:(c + 1) * LANES] = (blk * scale_a).astype(qa_o.dtype)
    for c, blk in enumerate(roped(sec("ka"), t64, 8)):
        kaf_o[:, c * LANES:(c + 1) * LANES] = blk
        kab_o[:, c * LANES:(c + 1) * LANES] = blk.astype(kab_o.dtype)
    va = sec("va")
    vaf_o[...] = va
    vab_o[...] = va.astype(vab_o.dtype)
    for c, blk in enumerate(roped(sec("qb"), t64, 8)):
        qb_o[:, c * LANES:(c + 1) * LANES] = (blk * scale_b).astype(qb_o.dtype)
    kb = roped(sec("kb"), t64, 8)[0]
    kbf_o[...] = kb
    kbb_o[...] = kb.astype(kbb_o.dtype)
    vb = sec("vb")
    vbf_o[...] = vb
    vbb_o[...] = vb.astype(vbb_o.dtype)
    for c, blk in enumerate(roped(sec("qi"), t32, 4)):
        qi_o[:, c * LANES:(c + 1) * LANES] = blk.astype(qi_o.dtype)
    kiwi = roped(sec("kiwi"), tki, 4)[0]
    kiwi_o[...] = kiwi
    wi_o[...] = kiwi
    for c, blk in enumerate(roped(sec("kir"), t32, 4)):
        kir_o[:, c * LANES:(c + 1) * LANES] = blk.astype(kir_o.dtype)
    ga_o[...] = jax.nn.sigmoid(sec("ga"))
    gb_o[...] = jax.nn.sigmoid(sec("gb"))


def _rope_tables(pos):
    def cs(d):
        rd = d // 4
        half = rd // 2
        inv_freq = ROPE_THETA ** (-jnp.arange(half, dtype=_F32) * 2.0 / rd)
        ang = pos.astype(_F32)[:, None] * inv_freq[None, :]
        return jnp.cos(ang), jnp.sin(ang), half

    T = pos.shape[0]

    def pattern(d, width):
        c, s, half = cs(d)
        one = jnp.ones((T, width - 2 * half), _F32)
        zero = lambda n: jnp.zeros((T, n), _F32)
        cos = jnp.concatenate([c, c, one], axis=1)
        lo = jnp.concatenate([-s, zero(width - half)], axis=1)
        hi = jnp.concatenate([zero(half), s, zero(width - 2 * half)], axis=1)
        return cos, lo, hi

    tabs = []
    for d, width in ((64, 64), (32, 32), (32, LANES)):
        tabs += [jnp.tile(t, (1, LANES // width)) for t in pattern(d, width)]
    return jnp.concatenate(tabs, axis=1)


def _layout_w_in(w_in):
    offs = np.cumsum((0,) + SPLITS)
    qa, ka, va, qb, kb, vb, qi, ki, wi, ga, gb = (w_in[:, offs[i]:offs[i + 1]] for i in range(11))
    D = w_in.shape[0]
    z64 = jnp.zeros((D, HEAD_DIM_B), w_in.dtype)
    per_kv = N_HEADS_B // N_KV_B
    qb_exp = []
    for hq in range(N_HEADS_B):
        col = qb[:, hq * HEAD_DIM_B:(hq + 1) * HEAD_DIM_B]
        qb_exp += [col, z64] if hq // per_kv == 0 else [z64, col]
    kiwi = jnp.concatenate([ki, wi, jnp.zeros((D, LANES - IDX_DIM - N_IDX_HEADS), w_in.dtype)], axis=1)
    kir = jnp.tile(ki, (1, N_IDX_HEADS))
    w = jnp.concatenate([qa, ka, va] + qb_exp + [kb, vb, qi, kiwi, kir, ga, gb], axis=1)
    assert w.shape[1] == W_LAYOUT
    return w.astype(_MXU)


def _layout_w_pb(w_pb):
    D = w_pb.shape[1]
    z64 = jnp.zeros((HEAD_DIM_B, D), w_pb.dtype)
    per_kv = N_HEADS_B // N_KV_B
    rows = []
    for hq in range(N_HEADS_B):
        r = w_pb[hq * HEAD_DIM_B:(hq + 1) * HEAD_DIM_B]
        rows += [r, z64] if hq // per_kv == 0 else [z64, r]
    return jnp.concatenate(rows, axis=0).astype(_MXU)


def _project(x, pos, g_mix, w_lay, tm, head_rows=None):
    B, T, D = x.shape
    assert T % tm == 0
    off = next(iter(head_rows.values())).shape[0] if head_rows else 0
    total = off + T
    assert off % 8 == 0
    tab = _rope_tables(pos)
    row = lambda w: pl.BlockSpec((None, tm, w), lambda b, i: (b, i, 0))
    shifted = lambda w: pl.BlockSpec((pl.Element(1), pl.Element(tm), pl.Element(w)),
                                     lambda b, i: (b, pl.multiple_of(off + i * tm, 8), 0))
    outs = [("qa", QA_W, _MXU), ("ka_f", QA_W, _F32), ("ka", QA_W, _MXU), ("va_f", VA_W, _F32),
            ("va", VA_W, _MXU), ("qb", QB_EXP_W, _MXU), ("kb_f", KB_W, _F32), ("kb", KB_W, _MXU),
            ("vb_f", KB_W, _F32), ("vb", KB_W, _MXU), ("qi", QI_W, _MXU), ("kiwi", LANES, _F32),
            ("kir", QI_W, _MXU), ("ga", D_MODEL, _F32), ("gb", D_MODEL, _F32), ("wi", LANES, _F32)]
    is_cache = lambda n: n.endswith("_f") or n == "kiwi"
    n_in = 4
    heads, aliases = [], {}
    for k, (n, w, _) in enumerate(outs):
        if head_rows and is_cache(n):
            rows = jnp.broadcast_to(head_rows[n][None], (B, off, w))
            heads.append(jnp.concatenate([rows, jnp.zeros((B, T, w), _F32)], axis=1))
            aliases[n_in + len(heads) - 1] = k
    res = pl.pallas_call(
        _proj_kernel,
        grid=(B, T // tm),
        in_specs=[row(D), _const_spec((1, D)),
                  pl.BlockSpec((D, W_LAYOUT), lambda b, i: (0, 0), pipeline_mode=pl.Buffered(1)),
                  pl.BlockSpec((tm, 9 * LANES), lambda b, i: (i, 0))]
                 + [pl.BlockSpec(memory_space=pl.ANY)] * len(heads),
        out_specs=[shifted(w) if (is_cache(n) and off) else row(w) for n, w, _ in outs],
        out_shape=[jax.ShapeDtypeStruct((B, total if is_cache(n) else T, w), dt) for n, w, dt in outs],
        input_output_aliases=aliases,
        compiler_params=_cparams(("parallel", "parallel")),
        name="proj",
    )(x, g_mix.reshape(1, D), w_lay, tab, *heads)
    return {n: r for (n, _, _), r in zip(outs, res)}


def _tile_lanes(v, n):
    return v if n == 1 else jnp.concatenate([v] * n, axis=1)


def _softmax_step(s, v, m_sc, l_sc, acc_sc):
    m_prev = m_sc[...]
    m_new = jnp.maximum(m_prev, jnp.max(s, axis=1, keepdims=True))
    alpha = jnp.exp2(m_prev - m_new)
    p = jnp.exp2(s - _tile_lanes(m_new, s.shape[1] // LANES))
    l_sc[...] = alpha * l_sc[...] + jnp.sum(p, axis=1, keepdims=True)
    acc_sc[...] = alpha * acc_sc[...] + jnp.dot(p.astype(v.dtype), v, preferred_element_type=_F32)
    m_sc[...] = m_new


def _attend_tiles(n, scores, values, s0_sc, s1_sc, m_sc, l_sc, acc_sc, first=None):
    s0_sc[...] = scores(0) if first is None else first()

    def two(j):
        s1_sc[...] = scores(j + 1)
        _softmax_step(s0_sc[...], values(j), m_sc, l_sc, acc_sc)
        s0_sc[...] = scores(jnp.minimum(j + 2, n - 1))
        _softmax_step(s1_sc[...], values(j + 1), m_sc, l_sc, acc_sc)

    def octet(p, carry):
        for t in range(0, 8, 2):
            two(8 * p + t)
        return carry

    lax.fori_loop(0, n // 8, octet, 0)
    done = (n // 8) * 8

    @pl.when(n - done >= 4)
    def _():
        two(done)
        two(done + 2)

    done = (n // 4) * 4

    @pl.when(n - done >= 2)
    def _():
        two(done)

    @pl.when(n % 2 == 1)
    def _():
        _softmax_step(s0_sc[...], values(n - 1), m_sc, l_sc, acc_sc)


def _for_tiles(n, tile_fn):
    def octet(p, carry):
        for t in range(8):
            tile_fn(8 * p + t)
        return carry

    lax.fori_loop(0, n // 8, octet, 0)
    done = (n // 8) * 8

    @pl.when(n - done >= 4)
    def _():
        for t in range(4):
            tile_fn(done + t)

    done = (n // 4) * 4

    @pl.when(n - done >= 2)
    def _():
        tile_fn(done)
        tile_fn(done + 1)

    @pl.when(n % 2 == 1)
    def _():
        tile_fn(n - 1)


def _num_key_tiles(qchunk, kchunk_main, TQ, TK):
    first = kchunk_main.reshape(-1, TK)[:, 0]
    qmax = qchunk.reshape(-1, TQ).max(axis=1)
    return (first[None, :] <= qmax[:, None]).sum(axis=1).astype(np.int32)


def _diff_attn_kernel(nk_ref, q_ref, k_ref, v_ref, kt_ref, vt_ref, qc_ref, kcm_ref, kct_ref,
                      lam_ref, sub_ref, o_ref, s0_sc, s1_sc, m_sc, l_sc, acc_sc, *, TQ, TK, lam_init):
    i = pl.program_id(2)
    q = q_ref[...]
    lane = lax.broadcasted_iota(jnp.int32, q.shape, 1)
    zero = jnp.zeros_like(q)
    qs = jnp.concatenate([jnp.where(lane < HEAD_DIM_A, q, zero),
                          jnp.where(lane >= HEAD_DIM_A, q, zero)], axis=0)
    qc = qc_ref[...]
    qc2 = jnp.concatenate([qc, qc], axis=0)
    m_sc[...] = jnp.full(m_sc.shape, _NEG, _F32)
    l_sc[...] = jnp.zeros(l_sc.shape, _F32)
    acc_sc[...] = jnp.zeros(acc_sc.shape, _F32)

    def keys(j):
        return k_ref[pl.ds(pl.multiple_of(j * TK, TK), TK), :]

    def values(j):
        return v_ref[pl.ds(pl.multiple_of(j * TK, TK), TK), :]

    def masked_scores(k, kc):
        return jnp.where(kc <= qc2, _nt_dot(qs, k), _NEG)

    _attend_tiles(nk_ref[i], lambda j: masked_scores(keys(j), kcm_ref[j]), values,
                  s0_sc, s1_sc, m_sc, l_sc, acc_sc)
    _softmax_step(masked_scores(kt_ref[...], kct_ref[...]), vt_ref[...], m_sc, l_sc, acc_sc)

    lam = (jnp.exp(jnp.sum(lam_ref[0:1, :] * lam_ref[1:2, :], axis=1, keepdims=True))
           - jnp.exp(jnp.sum(lam_ref[2:3, :] * lam_ref[3:4, :], axis=1, keepdims=True))
           + lam_init)
    o = acc_sc[...] / l_sc[...]
    d = o[:TQ] - lam * o[TQ:]
    ms = jnp.mean(d * d, axis=-1, keepdims=True)
    o_ref[...] = ((d * lax.rsqrt(ms + RMS_EPS) * sub_ref[...]) * (1.0 - lam_init)).astype(o_ref.dtype)


def _tail_spec(tail, width, index):
    per_batch = tail.shape[0] > 1
    return pl.BlockSpec((None, TAIL, width), lambda b, *rest: (b if per_batch else 0, 0, index(*rest)))


def _diff_attn(q, k, v, kt, vt, qchunk, kchunk, lam_vecs, subln_w, lam_init, Tq, TQ, TK):
    B = q.shape[0]
    Lmain = k.shape[1]
    L = Lmain
    assert Tq % TQ == 0 and Lmain % TK == 0 and kchunk.shape[0] == Lmain + TAIL
    nq, nt = Tq // TQ, Lmain // TK
    nk = jnp.asarray(_num_key_tiles(qchunk[:Tq], kchunk[:Lmain], TQ, TK))
    qc = jnp.asarray(qchunk[:Tq].reshape(Tq, 1).astype(np.int32))
    kcm = jnp.asarray(kchunk[:Lmain].reshape(nt, 1, TK).astype(np.int32))
    kct = jnp.asarray(kchunk[Lmain:].reshape(1, TAIL).astype(np.int32))
    R = 2 * TQ
    kern = functools.partial(_diff_attn_kernel, TQ=TQ, TK=TK, lam_init=lam_init)
    head = lambda h, i, *_: h
    grid_spec = pltpu.PrefetchScalarGridSpec(
        num_scalar_prefetch=1,
        grid=(B, N_HEADS_A, nq),
        in_specs=[pl.BlockSpec((None, TQ, LANES), lambda b, h, i, *_: (b, i, h)),
                  pl.BlockSpec((None, L, LANES), lambda b, h, i, *_: (b, 0, h)),
                  pl.BlockSpec((None, L, LANES), lambda b, h, i, *_: (b, 0, h)),
                  _tail_spec(kt, LANES, head), _tail_spec(vt, LANES, head),
                  pl.BlockSpec((TQ, 1), lambda b, h, i, *_: (i, 0)),
                  pl.BlockSpec((nt, 1, TK), lambda b, h, i, *_: (0, 0, 0)),
                  pl.BlockSpec((1, TAIL), lambda b, h, i, *_: (0, 0)),
                  pl.BlockSpec((4, HEAD_DIM_A), lambda b, h, i, *_: (0, 0)),
                  pl.BlockSpec((1, LANES), lambda b, h, i, *_: (0, 0))],
        out_specs=pl.BlockSpec((None, TQ, LANES), lambda b, h, i, *_: (b, i, h)),
        scratch_shapes=[pltpu.VMEM((R, TK), _F32)] * 2 + [pltpu.VMEM((R, LANES), _F32)] * 3,
    )
    return pl.pallas_call(
        kern, grid_spec=grid_spec,
        out_shape=jax.ShapeDtypeStruct((B, Tq, VA_W), _MXU),
        compiler_params=_cparams(("parallel", "parallel", "parallel")),
        name="diff_attn",
    )(nk, q, k, v, kt, vt, qc, kcm, kct, lam_vecs, subln_w.reshape(1, LANES))


def _dsa_kernel(nk_ref, qi_ref, kiwi_ref, qb_ref, kir_ref, kb_ref, vb_ref, kirt_ref, kbt_ref, vbt_ref,
                qc_ref, kcm_ref, kct_ref, o_ref,
                key_sc, keyt_sc, hi_sc, lo_sc, hit_sc, lot_sc, s0_sc, s1_sc, m_sc, l_sc, acc_sc,
                *, TQ, TK, topk, tail_first):
    i = pl.program_id(1)
    nk = nk_ref[i]
    H = N_IDX_HEADS
    nblk = TK // LANES

    qi = qi_ref[...]
    head = lax.broadcasted_iota(jnp.int32, qi.shape, 1) // IDX_DIM
    zero = jnp.zeros_like(qi)
    qis = jnp.concatenate([jnp.where(head == h, qi, zero) for h in range(H)], axis=0)
    wi = kiwi_ref[:, IDX_DIM:IDX_DIM + H] * (H ** -0.5) * (IDX_DIM ** -0.5)
    wi_b = [jnp.broadcast_to(wi[:, h:h + 1], (TQ, LANES)) for h in range(H)]
    qc = qc_ref[...]

    def score_keys(lg, kc):
        n = lg.shape[1] // LANES
        sc = _tile_lanes(wi_b[0], n) * jnp.maximum(lg[0:TQ], 0.0)
        for h in range(1, H):
            sc = sc + _tile_lanes(wi_b[h], n) * jnp.maximum(lg[h * TQ:(h + 1) * TQ], 0.0)
        return _sortable(jnp.where(kc <= qc, sc, -jnp.inf))

    def split16(key):
        return (key >> 16).astype(jnp.int16), ((key & 0xFFFF) - _HALF).astype(jnp.int16)

    def fill(j):
        lg = _nt_dot(qis, kir_ref[pl.ds(pl.multiple_of(j * TK, TK), TK), :])
        key = score_keys(lg, kcm_ref[j])
        key_sc[j] = key
        hi_sc[j], lo_sc[j] = split16(key)

    _for_tiles(nk, fill)
    key_t = score_keys(_nt_dot(qis, kirt_ref[...]), kct_ref[...])
    keyt_sc[...] = key_t
    hit_sc[...], lot_sc[...] = split16(key_t)

    qs = jnp.concatenate([qb_ref[:, h * LANES:(h + 1) * LANES] for h in range(N_HEADS_B)], axis=0)
    s0_sc[...] = _nt_dot(qs, kb_ref[0:TK, :])

    def count(ind_main, ind_tail, dtype):
        def body(j, c):
            ind = ind_main(j)
            for b in range(nblk):
                c = c + ind[:, b * LANES:(b + 1) * LANES]
            return c
        c = lax.fori_loop(0, nk, body, jnp.zeros((TQ, LANES), dtype)) + ind_tail()
        return jnp.sum(c.astype(_F32), axis=1, keepdims=True)

    one16, zero16 = jnp.int16(1), jnp.int16(0)

    def lanes16(v):
        return jnp.broadcast_to(v, (TQ, LANES)).astype(jnp.int16)

    def count16(main_sc, tail_sc, t16, strict=False):
        tm = _tile_lanes(t16, nblk)
        cmp = (lambda a, b: a > b) if strict else (lambda a, b: a >= b)
        return count(lambda j: jnp.where(cmp(main_sc[j], tm), one16, zero16),
                     lambda: jnp.where(cmp(tail_sc[...], t16), one16, zero16), jnp.int16)

    def kth_half(main_sc, tail_sc, target):
        def step(it, t_u):
            cand_u = t_u | jnp.left_shift(jnp.int32(1), 15 - it)
            cnt = count16(main_sc, tail_sc, lanes16(cand_u - _HALF))
            return jnp.where(cnt >= target, cand_u, t_u)
        return lax.fori_loop(0, 16, step, jnp.zeros((TQ, 1), jnp.int32))

    kf = float(topk)
    t_hi = kth_half(hi_sc, hit_sc, kf) - _HALF
    t_hi16 = lanes16(t_hi)
    n_above = count16(hi_sc, hit_sc, t_hi16, strict=True)
    t_hi16m = _tile_lanes(t_hi16, nblk)
    low16 = jnp.int16(-_HALF)

    def bucket(j, carry):
        lo_sc[j] = jnp.where(hi_sc[j] == t_hi16m, lo_sc[j], low16)
        return carry

    lax.fori_loop(0, nk, bucket, 0)
    lot_sc[...] = jnp.where(hit_sc[...] == t_hi16, lot_sc[...], low16)
    t_lo = kth_half(lo_sc, lot_sc, kf - n_above)
    n_ge = n_above + count16(lo_sc, lot_sc, lanes16(t_lo - _HALF))
    kth = (t_hi << 16) | t_lo
    thr1 = jnp.maximum(kth, _KEY_NEG_INF + 1)
    thr = jnp.broadcast_to(thr1, (TQ, LANES))
    thr_m = _tile_lanes(thr, nblk)
    excess = jnp.broadcast_to((n_ge > kf) & (kth > _KEY_NEG_INF), (TQ, LANES))

    @pl.when(jnp.max(jnp.where(excess, 1.0, 0.0)) > 0.0)
    def _():
        quota = kf - count(lambda j: jnp.where(key_sc[j] > thr_m, 1.0, 0.0),
                           lambda: jnp.where(keyt_sc[...] > thr, 1.0, 0.0), _F32)
        r = lax.broadcasted_iota(jnp.int32, (LANES, LANES), 0)
        c = lax.broadcasted_iota(jnp.int32, (LANES, LANES), 1)
        upto = jnp.where(r <= c, 1.0, 0.0).astype(_MXU)
        ones = jnp.ones((LANES, LANES), _MXU)

        def keep_first(keys, seen):
            tie = keys == thr
            e = jnp.where(tie, 1.0, 0.0).astype(_MXU)
            rank = seen + jnp.dot(e, upto, preferred_element_type=_F32)
            keys = jnp.where(tie & (rank > quota), _KEY_NEG_INF, keys)
            return keys, seen + jnp.dot(e, ones, preferred_element_type=_F32)

        def tail_block(seen):
            keyt_sc[...], seen = keep_first(keyt_sc[...], seen)
            return seen

        def main_tile(j, seen):
            kj = key_sc[j]
            blocks = []
            for b in range(nblk):
                blk, seen = keep_first(kj[:, b * LANES:(b + 1) * LANES], seen)
                blocks.append(blk)
            key_sc[j] = jnp.concatenate(blocks, axis=1)
            return seen

        seen = jnp.zeros((TQ, LANES), _F32)
        if tail_first:
            seen = tail_block(seen)
        seen = lax.fori_loop(0, nk, main_tile, seen)
        if not tail_first:
            tail_block(seen)

    m_sc[...] = jnp.full(m_sc.shape, _NEG, _F32)
    l_sc[...] = jnp.zeros(l_sc.shape, _F32)
    acc_sc[...] = jnp.zeros(acc_sc.shape, _F32)

    def select_bias(keys):
        bias = jnp.where(keys >= _tile_lanes(thr, keys.shape[1] // LANES), 0.0, _NEG)
        return jnp.concatenate([bias] * N_HEADS_B, axis=0)

    def masked_scores(keys, kb):
        return _nt_dot(qs, kb) + select_bias(keys)

    def scores(j):
        return masked_scores(key_sc[j], kb_ref[pl.ds(pl.multiple_of(j * TK, TK), TK), :])

    def values(j):
        return vb_ref[pl.ds(pl.multiple_of(j * TK, TK), TK), :]

    _attend_tiles(nk, scores, values, s0_sc, s1_sc, m_sc, l_sc, acc_sc,
                  first=lambda: s0_sc[...] + select_bias(key_sc[0]))
    _softmax_step(masked_scores(keyt_sc[...], kbt_ref[...]), vbt_ref[...], m_sc, l_sc, acc_sc)
    o = acc_sc[...] / l_sc[...]
    for h in range(N_HEADS_B):
        o_ref[:, h * LANES:(h + 1) * LANES] = o[h * TQ:(h + 1) * TQ].astype(o_ref.dtype)


def _dsa(qi, kiwi, qb, kir, kb, vb, kirt, kbt, vbt, qchunk, kchunk, korder, topk, Tq, TQ, TK):
    B = qi.shape[0]
    L = Lmain = kir.shape[1]
    assert Tq % TQ == 0 and Lmain % TK == 0 and kchunk.shape[0] == Lmain + TAIL
    real = kchunk < _BIG_CHUNK
    order_main, order_tail = korder[:Lmain], korder[Lmain:][real[Lmain:]]
    assert np.all(np.diff(order_main) > 0) and np.all(np.diff(order_tail) > 0)
    tail_first = bool(order_tail[-1] < order_main[0])
    assert tail_first or order_tail[0] > order_main[-1]
    nq, nt = Tq // TQ, Lmain // TK
    nk = jnp.asarray(_num_key_tiles(qchunk[:Tq], kchunk[:Lmain], TQ, TK))
    qc = jnp.asarray(qchunk[:Tq].reshape(Tq, 1).astype(np.int32))
    main = lambda a: jnp.asarray(a[:Lmain].reshape(nt, 1, TK).astype(np.int32))
    tail = lambda a: jnp.asarray(a[Lmain:].reshape(1, TAIL).astype(np.int32))
    R = N_HEADS_B * TQ
    kern = functools.partial(_dsa_kernel, TQ=TQ, TK=TK, topk=topk, tail_first=tail_first)
    first = lambda *_: 0
    qrow = lambda w: pl.BlockSpec((None, TQ, w), lambda b, i, nk: (b, i, 0))
    krow = lambda w: pl.BlockSpec((None, L, w), lambda b, i, nk: (b, 0, 0),
                                  pipeline_mode=pl.Buffered(1))
    cm = pl.BlockSpec((nt, 1, TK), lambda b, i, nk: (0, 0, 0))
    ct = pl.BlockSpec((1, TAIL), lambda b, i, nk: (0, 0))
    grid_spec = pltpu.PrefetchScalarGridSpec(
        num_scalar_prefetch=1,
        grid=(B, nq),
        in_specs=[qrow(QI_W), qrow(LANES), qrow(QB_EXP_W), krow(QI_W), krow(KB_W), krow(KB_W),
                  _tail_spec(kirt, QI_W, first), _tail_spec(kbt, KB_W, first), _tail_spec(vbt, KB_W, first),
                  pl.BlockSpec((TQ, 1), lambda b, i, nk: (i, 0)), cm, ct],
        out_specs=qrow(QB_EXP_W),
        scratch_shapes=[pltpu.VMEM((nt, TQ, TK), jnp.int32), pltpu.VMEM((TQ, TAIL), jnp.int32),
                        pltpu.VMEM((nt, TQ, TK), jnp.int16), pltpu.VMEM((nt, TQ, TK), jnp.int16),
                        pltpu.VMEM((TQ, TAIL), jnp.int16), pltpu.VMEM((TQ, TAIL), jnp.int16),
                        pltpu.VMEM((R, TK), _F32), pltpu.VMEM((R, TK), _F32),
                        pltpu.VMEM((R, LANES), _F32), pltpu.VMEM((R, LANES), _F32),
                        pltpu.VMEM((R, LANES), _F32)],
    )
    return pl.pallas_call(
        kern, grid_spec=grid_spec,
        out_shape=jax.ShapeDtypeStruct((B, Tq, QB_EXP_W), _MXU),
        compiler_params=_cparams(("parallel", "parallel")),
        name="dsa",
    )(nk, qi, kiwi, qb, kir, kb, vb, kirt, kbt, vbt, qc, main(kchunk), tail(kchunk))


def _post_kernel(x_ref, oa_ref, ob_ref, ga_ref, gb_ref, wpa_ref, wpb_ref, wo_ref, g_ref,
                 wr_ref, br_ref, x1_o, h2_o, route_o):
    pa = jnp.dot(oa_ref[...], wpa_ref[...], preferred_element_type=_F32)
    pb = jnp.dot(ob_ref[...], wpb_ref[...], preferred_element_type=_F32)
    merged = ga_ref[...] * pa + gb_ref[...] * pb
    x1 = x_ref[...] + jnp.dot(merged.astype(wo_ref.dtype), wo_ref[...], preferred_element_type=_F32)
    x1_o[...] = x1
    ms = jnp.mean(x1 * x1, axis=-1, keepdims=True)
    h2 = (x1 * lax.rsqrt(ms + RMS_EPS) * g_ref[...]).astype(h2_o.dtype)
    h2_o[...] = h2
    cur = jnp.dot(h2, wr_ref[...], preferred_element_type=_F32) + br_ref[...]
    lane = lax.broadcasted_iota(jnp.int32, cur.shape, 1).astype(_F32)
    vals, idxs = [], []
    for _ in range(TOP_K_EXPERTS):
        m = jnp.max(cur, axis=1, keepdims=True)
        idx = jnp.min(jnp.where(cur == m, lane, float(LANES)), axis=1, keepdims=True)
        vals.append(m)
        idxs.append(idx)
        cur = jnp.where(lane == idx, -jnp.inf, cur)
    es = [jnp.exp(v - vals[0]) for v in vals]
    denom = es[0] + es[1] + es[2] + es[3]
    route = jnp.zeros(cur.shape, _F32)
    for k in range(TOP_K_EXPERTS):
        route = jnp.where(lane == float(k), idxs[k], route)
        route = jnp.where(lane == float(TOP_K_EXPERTS + k), es[k] / denom, route)
    route_o[...] = route


def _post(x, oa, ob, ga, gb, w_pa, w_pb_exp, w_o, g_ffn, w_router, b_router, T, tm):
    B, _, D = x.shape
    assert T % tm == 0
    row = lambda w: pl.BlockSpec((None, tm, w), lambda b, i: (b, i, 0))
    wr = jnp.concatenate([w_router, jnp.zeros((D, LANES - N_EXPERTS), w_router.dtype)], axis=1).astype(_MXU)
    br = jnp.concatenate([b_router.astype(_F32), jnp.full((LANES - N_EXPERTS,), _NEG, _F32)]).reshape(1, LANES)
    return pl.pallas_call(
        _post_kernel,
        grid=(B, T // tm),
        in_specs=[row(D), row(VA_W), row(QB_EXP_W), row(D), row(D),
                  _const_spec((VA_W, D)), _const_spec((QB_EXP_W, D)), _const_spec((D, D)),
                  _const_spec((1, D)), _const_spec((D, LANES)), _const_spec((1, LANES))],
        out_specs=[row(D), row(D), row(LANES)],
        out_shape=[jax.ShapeDtypeStruct((B, T, D), _F32), jax.ShapeDtypeStruct((B, T, D), _MXU),
                   jax.ShapeDtypeStruct((B, T, LANES), _F32)],
        compiler_params=_cparams(("parallel", "parallel")),
        name="post",
    )(x, oa, ob, ga, gb, w_pa.astype(_MXU), w_pb_exp, w_o.astype(_MXU), g_ffn.reshape(1, D), wr, br)


def _moe_kernel(be_ref, nu_ref, x_ref, wgu_ref, bgu_ref, wdn_ref, bdn_ref, y_ref, wgu_sc, wdn_sc):
    i = pl.program_id(0)
    used = i < nu_ref[0]
    new_expert = (i == 0) | (be_ref[i] != be_ref[jnp.maximum(i - 1, 0)])

    @pl.when(used & new_expert)
    def _():
        wgu_sc[...] = wgu_ref[...].astype(wgu_sc.dtype)
        wdn_sc[...] = wdn_ref[...].astype(wdn_sc.dtype)

    @pl.when(used)
    def _():
        gu = jnp.dot(x_ref[...], wgu_sc[...], preferred_element_type=_F32) + bgu_ref[...]
        glu = jnp.minimum(gu[:, :D_EXPERT], SWIGLU_LIMIT)
        lin = jnp.clip(gu[:, D_EXPERT:], -SWIGLU_LIMIT, SWIGLU_LIMIT)
        act = glu * jax.nn.sigmoid(SWIGLU_ALPHA * glu) * (lin + 1.0)
        y_ref[...] = jnp.dot(act.astype(wdn_sc.dtype), wdn_sc[...],
                             preferred_element_type=_F32) + bdn_ref[...]

    @pl.when(jnp.logical_not(used))
    def _():
        y_ref[...] = jnp.zeros(y_ref.shape, y_ref.dtype)


def _moe(h2, route, w_gu, b_gu, w_dn, b_dn):
    T, D = h2.shape
    K = TOP_K_EXPERTS
    A = T * K
    NB = -(-A // MOE_BLOCK) + N_EXPERTS
    NS = NB * MOE_BLOCK
    e_flat = route[:, :K].astype(jnp.int32).reshape(-1)
    experts = jnp.arange(N_EXPERTS, dtype=jnp.int32)
    counts = jnp.sum((e_flat[:, None] == experts[None, :]).astype(jnp.int32), axis=0)
    blocks_per = (counts + MOE_BLOCK - 1) // MOE_BLOCK
    cum_blocks = jnp.cumsum(blocks_per)
    blk = jnp.arange(NB, dtype=jnp.int32)
    block_expert = jnp.minimum(jnp.sum((cum_blocks[None, :] <= blk[:, None]).astype(jnp.int32), axis=1),
                               N_EXPERTS - 1)
    n_used = cum_blocks[-1:].astype(jnp.int32)
    cum_pad = jnp.cumsum(blocks_per * MOE_BLOCK - counts)
    r = jnp.arange(NS - A, dtype=jnp.int32)
    empty_expert = jnp.sum((cum_pad[None, :] <= r[:, None]).astype(jnp.int32), axis=1)
    src_bits = (NS - 1).bit_length()
    assert (2 * N_EXPERTS + 2) << src_bits < 2 ** 31
    group = jnp.concatenate([2 * e_flat, 2 * empty_expert + 1])
    slot_src = lax.sort((group << src_bits) | jnp.arange(NS, dtype=jnp.int32)) & ((1 << src_bits) - 1)
    slot_tok = jnp.where(slot_src < A, slot_src // K, (slot_src - A) % T)
    x_slots = h2[slot_tok]
    y_slots = pl.pallas_call(
        _moe_kernel,
        grid_spec=pltpu.PrefetchScalarGridSpec(
            num_scalar_prefetch=2,
            grid=(NB,),
            in_specs=[pl.BlockSpec((MOE_BLOCK, D), lambda i, be, nu: (i, 0)),
                      pl.BlockSpec((None, D, 2 * D_EXPERT), lambda i, be, nu: (be[i], 0, 0)),
                      pl.BlockSpec((None, 1, 2 * D_EXPERT), lambda i, be, nu: (be[i], 0, 0)),
                      pl.BlockSpec((None, D_EXPERT, D), lambda i, be, nu: (be[i], 0, 0)),
                      pl.BlockSpec((None, 1, D), lambda i, be, nu: (be[i], 0, 0))],
            out_specs=pl.BlockSpec((MOE_BLOCK, D), lambda i, be, nu: (i, 0)),
            scratch_shapes=[pltpu.VMEM((D, 2 * D_EXPERT), _MXU), pltpu.VMEM((D_EXPERT, D), _MXU)],
        ),
        out_shape=jax.ShapeDtypeStruct((NB * MOE_BLOCK, D), _F32),
        compiler_params=_cparams(("arbitrary",)),
        name="moe",
    )(block_expert, n_used, x_slots, w_gu, b_gu.reshape(N_EXPERTS, 1, -1).astype(_F32),
      w_dn, b_dn.reshape(N_EXPERTS, 1, -1).astype(_F32))
    _, dest = lax.sort((slot_src, jnp.arange(NS, dtype=jnp.int32)), num_keys=1)
    dest = dest[:A].reshape(T, K)
    return [y_slots[dest[:, k]] for k in range(K)]


def _final_kernel(x_ref, y0_ref, y1_ref, y2_ref, y3_ref, route_ref, g_ref, o_ref):
    K = TOP_K_EXPERTS
    x = x_ref[...]
    for k, y_ref in enumerate((y0_ref, y1_ref, y2_ref, y3_ref)):
        x = x + route_ref[:, K + k:K + k + 1] * y_ref[...]
    ms = jnp.mean(x * x, axis=-1, keepdims=True)
    o_ref[...] = x * lax.rsqrt(ms + RMS_EPS) * g_ref[...]


def _final(x1, ys, route, g_final, tm):
    T, D = x1.shape
    assert T % tm == 0
    row = pl.BlockSpec((tm, D), lambda i: (i, 0))
    return pl.pallas_call(
        _final_kernel, grid=(T // tm,),
        in_specs=[row] * 5 + [pl.BlockSpec((tm, LANES), lambda i: (i, 0)), _const_spec((1, D))],
        out_specs=row,
        out_shape=jax.ShapeDtypeStruct((T, D), _F32),
        compiler_params=_cparams(("parallel",)),
        name="final",
    )(x1, *ys, route, g_final.reshape(1, D))


def _group(x_rows, proj, keys, tails, qchunk, kchunk, korder, topk, Tq, tiles, weights, lam_init):
    (lam_vecs, subln_w, w_pa, w_pb_exp, w_o, g_ffn, w_router, b_router,
     w_gu, b_gu, w_dn, b_dn, g_final) = weights
    B = x_rows.shape[0]
    oa = _diff_attn(proj["qa"], keys["ka"], keys["va"], tails["ka"], tails["va"], qchunk, kchunk,
                    lam_vecs, subln_w, lam_init, Tq, tiles["tq_a"], tiles["tk"])
    ob = _dsa(proj["qi"], proj["wi"], proj["qb"], keys["kir"], keys["kb"], keys["vb"],
              tails["kir"], tails["kb"], tails["vb"], qchunk, kchunk, korder, topk, Tq,
              tiles["tq_b"], tiles["tk"])
    x1, h2, route = _post(x_rows, oa, ob, proj["ga"], proj["gb"], w_pa, w_pb_exp, w_o, g_ffn,
                          w_router, b_router, Tq, tiles["tm_post"])
    T = B * Tq
    route = route.reshape(T, LANES)
    ys = _moe(h2.reshape(T, D_MODEL), route, w_gu, b_gu, w_dn, b_dn)
    y = _final(x1.reshape(T, D_MODEL), ys, route, g_final, tiles["tm_post"])
    return y.reshape(B, Tq, D_MODEL)


_KEY_OPERANDS = ("ka", "va", "kb", "vb", "kir")


def kernel(x_prompt, x_sample, cache_a_k, cache_a_v, cache_b_k, cache_b_v, cache_idx_k,
           meta_tokens, g_mix, w_in, lambda_q1, lambda_k1, lambda_q2, lambda_k2, subln_w,
           w_pa, w_pb, w_o, g_ffn, w_router, b_router, w_gu, b_gu, w_dn, b_dn, g_final):
    B, S, D = x_prompt.shape
    Bs, S_dec, _ = x_sample.shape
    P = cache_a_k.shape[2]
    depth = g_mix.shape[0]
    assert depth == 1
    l = 0
    lam_init = 0.8 - 0.6 * math.exp(-0.3 * l)
    w_lay = _layout_w_in(w_in[l])
    w_pb_exp = _layout_w_pb(w_pb[l])
    lam_vecs = jnp.stack([lambda_q1[l], lambda_k1[l], lambda_q2[l], lambda_k2[l]]).astype(_F32)
    weights = (lam_vecs, subln_w[l], w_pa[l], w_pb_exp, w_o[l], g_ffn[l], w_router[l], b_router[l],
               w_gu[l], b_gu[l], w_dn[l], b_dn[l], g_final)

    r = np.arange(S + TAIL)
    chunk_p = np.where(r < S, r // CHUNK, np.where(r < S + N_META, -1, _BIG_CHUNK)).astype(np.int64)
    order_p = np.where(r < S, r + N_META, r - S).astype(np.int32)
    tiles_p = dict(tq_a=min(512, S), tq_b=min(128, S), tk=min(512, S), tm_post=min(512, S))
    meta_tile = jnp.concatenate([meta_tokens.astype(x_prompt.dtype),
                                 jnp.zeros((TAIL - N_META, D), x_prompt.dtype)])[None]
    pos_m = np.where(np.arange(TAIL) < N_META, np.arange(TAIL), 0).astype(np.int32)
    pm = _project(meta_tile, jnp.asarray(pos_m), g_mix[l], w_lay, TAIL)
    meta_rows = {n: pm[n][0, :N_META] for n in ("ka_f", "va_f", "kb_f", "vb_f", "kiwi")}
    pp = _project(x_prompt, jnp.asarray(N_META + np.arange(S, dtype=np.int32)), g_mix[l], w_lay,
                  tiles_p["tm_post"], head_rows=meta_rows)
    y_prompt = _group(x_prompt, pp, pp, pm, chunk_p[:S], chunk_p, order_p, min(TOPK_MAX, S // 4), S,
                      tiles_p, weights, lam_init)

    def rows_p(name, shape, width=None):
        return pp[name][:, :, :width].reshape((1, B, N_META + S) + shape)

    pos_s = (P + np.arange(S_dec)).astype(np.int32)
    ps = _project(x_sample, jnp.asarray(pos_s), g_mix[l], w_lay, S_dec)
    r = np.arange(P + TAIL)
    chunk_s = np.where(r < P + S_dec, r // CHUNK, _BIG_CHUNK).astype(np.int64)
    qchunk_s = (pos_s // CHUNK).astype(np.int64)
    cached = dict(ka=cache_a_k[l], va=cache_a_v[l], kb=cache_b_k[l], vb=cache_b_v[l],
                  kir=jnp.tile(cache_idx_k[l], (1, 1, N_IDX_HEADS)))
    keys_s = {n: cached[n].reshape(Bs, P, -1).astype(_MXU) for n in _KEY_OPERANDS}
    tails_s = {n: jnp.concatenate([ps[n], jnp.zeros((Bs, TAIL - S_dec, ps[n].shape[-1]), _MXU)], axis=1)
               for n in _KEY_OPERANDS}
    tiles_s = dict(tq_a=S_dec, tq_b=S_dec, tk=min(512, P), tm_post=S_dec)
    y_sample = _group(x_sample, ps, keys_s, tails_s, qchunk_s, chunk_s, r.astype(np.int32),
                      min(TOPK_MAX, (P + S_dec) // 4), S_dec, tiles_s, weights, lam_init)

    def rows_s(name, shape, width=None):
        return ps[name][:, :, :width].reshape((1, Bs, S_dec) + shape)

    ha, hb = (N_HEADS_A, 2 * HEAD_DIM_A), (N_KV_B, HEAD_DIM_B)
    return (y_prompt, y_sample,
            rows_p("ka_f", ha), rows_p("va_f", ha), rows_p("kb_f", hb), rows_p("vb_f", hb),
            rows_p("kiwi", (IDX_DIM,), IDX_DIM),
            rows_s("ka_f", ha), rows_s("va_f", ha), rows_s("kb_f", hb), rows_s("vb_f", hb),
            rows_s("kiwi", (IDX_DIM,), IDX_DIM))
```
